```python
import math
import jax, jax.numpy as jnp
from jax import lax
import numpy as np

D_MODEL = 2048
BATCH = 2
SEQ = 4096
DEPTH = 4
DEC_BATCH = 32
DEC_SEQ = 8
PAST_LEN = 16384
PAGE_SIZE = 128

D_MIX = D_MODEL
CHUNK = 128
W_A = D_MIX // 4
H_A = 4
C_A = W_A // H_A
W_B = D_MIX // 2
HD_B = 64
H_B = W_B // HD_B
KV_B = 4
REP_B = H_B // KV_B
WINDOW = 128
ROPE_DIM = HD_B // 4
ROPE_THETA = 500000.0
W_C = D_MIX - W_A - W_B
GC = 16
G_C = W_C // GC
P_C = 64
D_IN = 3 * W_A + 2 * W_B + 2 * KV_B * HD_B + 2 * W_C
EPS = 1e-5
NEG = -1e30

kernel_name = "hymba_chunkmlp_swa_s5_step"


def rmsnorm(x, g):
    xf = x.astype(jnp.float32)
    y = xf * lax.rsqrt(jnp.mean(xf * xf, axis=-1, keepdims=True) + EPS)
    return (y * g.astype(jnp.float32)).astype(x.dtype)


def rope(x, pos):
    half = ROPE_DIM // 2
    inv = ROPE_THETA ** (-jnp.arange(half, dtype=jnp.float32) * 2.0 / ROPE_DIM)
    ang = pos[:, None] * inv[None, :]
    cos = jnp.cos(ang)[None, :, None, :]
    sin = jnp.sin(ang)[None, :, None, :]
    xf = x.astype(jnp.float32)
    x1, x2, rest = xf[..., :half], xf[..., half:ROPE_DIM], xf[..., ROPE_DIM:]
    out = jnp.concatenate([x1 * cos - x2 * sin, x2 * cos + x1 * sin, rest], axis=-1)
    return out.astype(x.dtype)


def chunk_mlp(u, v, w_s, b_s):
    bsz, L, _ = v.shape
    n = min(L, CHUNK)
    mask = jnp.tril(jnp.ones((n, n), dtype=bool))
    ws = jnp.where(mask[None], w_s[:, :n, :n].astype(jnp.float32), 0.0)
    vc = v.astype(jnp.float32).reshape(bsz, L // n, n, H_A, C_A)
    z = jnp.einsum('hts,bnshc->bnthc', ws, vc)
    z = z + b_s[:, :n].astype(jnp.float32).T[None, None, :, :, None]
    return u.astype(jnp.float32) * z.reshape(bsz, L, W_A)


def attend_with_sinks(qb, kb, vb, mask, sinks):
    s = jnp.einsum('bnqgrd,bnkgd->bngrqk', qb.astype(jnp.float32), kb.astype(jnp.float32)) * (HD_B ** -0.5)
    s = jnp.where(mask[None, :, None, None], s, NEG)
    sink = sinks.astype(jnp.float32).reshape(KV_B, REP_B)[None, None, :, :, None, None]
    m = jnp.maximum(jnp.max(s, axis=-1, keepdims=True), sink)
    p = jnp.exp(s - m)
    w = p / (jnp.sum(p, axis=-1, keepdims=True) + jnp.exp(sink - m))
    return jnp.einsum('bngrqk,bnkgd->bnqgrd', w, vb.astype(jnp.float32))


def swa_prompt(q, k, v, sinks):
    bsz, L = q.shape[:2]
    nb = L // WINDOW
    qb = q.reshape(bsz, nb, WINDOW, KV_B, REP_B, HD_B)
    pad = ((0, 0), (WINDOW, 0), (0, 0), (0, 0))
    kp = jnp.pad(k, pad).reshape(bsz, nb + 1, WINDOW, KV_B, HD_B)
    vp = jnp.pad(v, pad).reshape(bsz, nb + 1, WINDOW, KV_B, HD_B)
    kb = jnp.concatenate([kp[:, :-1], kp[:, 1:]], axis=2)
    vb = jnp.concatenate([vp[:, :-1], vp[:, 1:]], axis=2)
    i = jnp.arange(WINDOW)[None, :, None]
    j = jnp.arange(2 * WINDOW)[None, None, :]
    blk = jnp.arange(nb)[:, None, None]
    diff = i + WINDOW - j
    mask = (diff >= 0) & (diff < WINDOW) & (blk * WINDOW + j - WINDOW >= 0)
    o = attend_with_sinks(qb, kb, vb, mask, sinks)
    return o.reshape(bsz, L, W_B)


def swa_sample(q, k, v, k_past, v_past, sinks):
    bsz, S = q.shape[:2]
    kb = jnp.concatenate([k_past.astype(k.dtype), k], axis=1)[:, None]
    vb = jnp.concatenate([v_past.astype(v.dtype), v], axis=1)[:, None]
    qb = q.reshape(bsz, 1, S, KV_B, REP_B, HD_B)
    i = jnp.arange(S)[:, None]
    j = jnp.arange(WINDOW + S)[None, :]
    diff = i + WINDOW - j
    mask = ((diff >= 0) & (diff < WINDOW))[None]
    o = attend_with_sinks(qb, kb, vb, mask, sinks)
    return o.reshape(bsz, S, W_B)


def s5(u, h0_re, h0_im, a_re, a_im, log_dt, b_re, b_im, c_re, c_im, d_skip, w_glu, b_glu):
    bsz, L, _ = u.shape
    f32 = jnp.float32
    a_re, a_im = a_re.astype(f32), a_im.astype(f32)
    dt = jnp.exp(log_dt.astype(f32))[:, None]
    mag = jnp.exp(a_re * dt)
    ab_re, ab_im = mag * jnp.cos(a_im * dt), mag * jnp.sin(a_im * dt)
    nr, ni = ab_re - 1.0, ab_im
    den = a_re * a_re + a_im * a_im
    f_re = (nr * a_re + ni * a_im) / den
    f_im = (ni * a_re - nr * a_im) / den
    br, bi = b_re.astype(f32), b_im.astype(f32)
    bb_re = f_re[..., None] * br - f_im[..., None] * bi
    bb_im = f_re[..., None] * bi + f_im[..., None] * br
    uf = u.astype(f32)
    ug = uf.reshape(bsz, L, G_C, GC)
    bu_re = jnp.einsum('gpc,blgc->blgp', bb_re, ug)
    bu_im = jnp.einsum('gpc,blgc->blgp', bb_im, ug)
    h0r, h0i = h0_re.astype(f32), h0_im.astype(f32)
    bu_re = bu_re.at[:, 0].add(ab_re * h0r - ab_im * h0i)
    bu_im = bu_im.at[:, 0].add(ab_re * h0i + ab_im * h0r)
    ar_b = jnp.broadcast_to(ab_re, bu_re.shape)
    ai_b = jnp.broadcast_to(ab_im, bu_re.shape)

    def combine(e1, e2):
        a1r, a1i, b1r, b1i = e1
        a2r, a2i, b2r, b2i = e2
        return (a2r * a1r - a2i * a1i,
                a2r * a1i + a2i * a1r,
                a2r * b1r - a2i * b1i + b2r,
                a2r * b1i + a2i * b1r + b2i)

    _, _, h_re, h_im = lax.associative_scan(combine, (ar_b, ai_b, bu_re, bu_im), axis=1)
    y = (jnp.einsum('gcp,blgp->blgc', c_re.astype(f32), h_re)
         - jnp.einsum('gcp,blgp->blgc', c_im.astype(f32), h_im))
    y = y.reshape(bsz, L, W_C) + d_skip.astype(f32) * uf
    y = jax.nn.gelu(y)
    y = y * jax.nn.sigmoid(y @ w_glu.astype(f32) + b_glu.astype(f32))
    return y, h_re[:, -1], h_im[:, -1]


def mixer_layer(x, pos, k_past, v_past, h0_re, h0_im, norm_g, w_in, w_out, w_s, b_s, sinks,
                a_re, a_im, log_dt, b_re, b_im, c_re, c_im, d_skip, w_glu, b_glu):
    bsz, L, _ = x.shape
    h = rmsnorm(x, norm_g)
    z = h @ w_in
    sizes = [W_A, W_A, W_A, W_B, KV_B * HD_B, KV_B * HD_B, W_B, W_C, W_C]
    splits = [int(s) for s in np.cumsum(sizes)[:-1]]
    u_a, v_a, g_a, q, k, v, g_b, u_c, g_c = jnp.split(z, splits, axis=-1)
    q = rope(q.reshape(bsz, L, H_B, HD_B), pos)
    k = rope(k.reshape(bsz, L, KV_B, HD_B), pos)
    v = v.reshape(bsz, L, KV_B, HD_B)
    out_a = chunk_mlp(u_a, v_a, w_s, b_s)
    if k_past is None:
        out_b = swa_prompt(q, k, v, sinks)
        k_rows, v_rows = k[:, -WINDOW:], v[:, -WINDOW:]
        h0_re = jnp.zeros((bsz, G_C, P_C), jnp.float32)
        h0_im = jnp.zeros((bsz, G_C, P_C), jnp.float32)
    else:
        out_b = swa_sample(q, k, v, k_past, v_past, sinks)
        k_rows, v_rows = k, v
    out_c, h_re, h_im = s5(u_c, h0_re, h0_im, a_re, a_im, log_dt, b_re, b_im, c_re, c_im,
                           d_skip, w_glu, b_glu)
    mixed = jnp.concatenate([
        out_a * jax.nn.silu(g_a.astype(jnp.float32)),
        out_b * jax.nn.silu(g_b.astype(jnp.float32)),
        out_c * jax.nn.silu(g_c.astype(jnp.float32))], axis=-1).astype(x.dtype)
    y = x + mixed @ w_out
    return y, k_rows, v_rows, h_re.astype(x.dtype), h_im.astype(x.dtype), v_a


def setup_inputs(seed: int = 0) -> dict:
    key = jax.random.key(seed)
    ks = jax.random.split(key, 24)
    f32 = jnp.float32
    nrm = lambda k, shape, s: jax.random.normal(k, shape, f32) * s
    n_idx = jnp.arange(P_C, dtype=f32)
    return {
        "x_prompt": nrm(ks[0], (BATCH, SEQ, D_MODEL), 1.0),
        "x_sample": nrm(ks[1], (DEC_BATCH, DEC_SEQ, D_MODEL), 1.0),
        "cache_swa_k": nrm(ks[2], (DEPTH, DEC_BATCH, WINDOW, KV_B, HD_B), 1.0),
        "cache_swa_v": nrm(ks[3], (DEPTH, DEC_BATCH, WINDOW, KV_B, HD_B), 1.0),
        "state_ssm_re": nrm(ks[4], (DEPTH, DEC_BATCH, G_C, P_C), 0.5),
        "state_ssm_im": nrm(ks[5], (DEPTH, DEC_BATCH, G_C, P_C), 0.5),
        "norm_g": 1.0 + nrm(ks[6], (DEPTH, D_MODEL), 0.02),
        "final_norm_g": 1.0 + nrm(ks[7], (D_MODEL,), 0.02),
        "w_in": nrm(ks[8], (DEPTH, D_MODEL, D_IN), D_MODEL ** -0.5),
        "w_out": nrm(ks[9], (DEPTH, D_MIX, D_MODEL), D_MIX ** -0.5),
        "chunk_w_s": nrm(ks[10], (DEPTH, H_A, CHUNK, CHUNK), CHUNK ** -0.5),
        "chunk_b_s": 1.0 + nrm(ks[11], (DEPTH, H_A, CHUNK), 0.02),
        "attn_sinks": nrm(ks[12], (DEPTH, H_B), 1.0),
        "ssm_a_re": -0.5 + nrm(ks[13], (DEPTH, G_C, P_C), 0.01),
        "ssm_a_im": math.pi * n_idx + nrm(ks[14], (DEPTH, G_C, P_C), 0.01),
        "ssm_log_dt": jax.random.uniform(ks[15], (DEPTH, G_C), f32, math.log(1e-3), math.log(1e-1)),
        "ssm_b_re": nrm(ks[16], (DEPTH, G_C, P_C, GC), (2.0 * GC) ** -0.5),
        "ssm_b_im": nrm(ks[17], (DEPTH, G_C, P_C, GC), (2.0 * GC) ** -0.5),
        "ssm_c_re": nrm(ks[18], (DEPTH, G_C, GC, P_C), (2.0 * P_C) ** -0.5),
        "ssm_c_im": nrm(ks[19], (DEPTH, G_C, GC, P_C), (2.0 * P_C) ** -0.5),
        "ssm_d": nrm(ks[20], (DEPTH, W_C), 0.5),
        "glu_w": nrm(ks[21], (DEPTH, W_C, W_C), W_C ** -0.5),
        "glu_b": nrm(ks[22], (DEPTH, W_C), 0.01),
    }


def reference(x_prompt, x_sample, cache_swa_k, cache_swa_v, state_ssm_re, state_ssm_im,
              norm_g, final_norm_g, w_in, w_out, chunk_w_s, chunk_b_s, attn_sinks,
              ssm_a_re, ssm_a_im, ssm_log_dt, ssm_b_re, ssm_b_im, ssm_c_re, ssm_c_im,
              ssm_d, glu_w, glu_b):
    pos_p = jnp.arange(SEQ, dtype=jnp.float32)
    pos_s = jnp.arange(DEC_SEQ, dtype=jnp.float32) + PAST_LEN
    xp, xs = x_prompt, x_sample
    kp_l, vp_l, hrp_l, hip_l = [], [], [], []
    ks_l, vs_l, hrs_l, his_l, va_l = [], [], [], [], []
    for l in range(DEPTH):
        wl = (norm_g[l], w_in[l], w_out[l], chunk_w_s[l], chunk_b_s[l], attn_sinks[l],
              ssm_a_re[l], ssm_a_im[l], ssm_log_dt[l], ssm_b_re[l], ssm_b_im[l],
              ssm_c_re[l], ssm_c_im[l], ssm_d[l], glu_w[l], glu_b[l])
        xp, kp, vp, hrp, hip, _ = mixer_layer(xp, pos_p, None, None, None, None, *wl)
        xs, kss, vss, hrs, his, vas = mixer_layer(xs, pos_s, cache_swa_k[l], cache_swa_v[l],
                                                  state_ssm_re[l], state_ssm_im[l], *wl)
        kp_l.append(kp); vp_l.append(vp); hrp_l.append(hrp); hip_l.append(hip)
        ks_l.append(kss); vs_l.append(vss); hrs_l.append(hrs); his_l.append(his); va_l.append(vas)
    y_prompt = rmsnorm(xp, final_norm_g)
    y_sample = rmsnorm(xs, final_norm_g)
    return (y_prompt, y_sample,
            jnp.stack(kp_l), jnp.stack(vp_l), jnp.stack(ks_l), jnp.stack(vs_l),
            jnp.stack(hrp_l), jnp.stack(hip_l), jnp.stack(hrs_l), jnp.stack(his_l),
            jnp.stack(va_l))
```

```python
import functools
import math

import jax
import jax.numpy as jnp
import numpy as np
from jax import lax
from jax.experimental import pallas as pl
from jax.experimental.pallas import tpu as pltpu

F32 = jnp.float32
BF16 = jnp.bfloat16

D_MODEL = 2048
DEPTH = 4
PAST_LEN = 16384
CHUNK = 128
W_A = 512
H_A = 4
C_A = W_A // H_A
W_B = 1024
HD_B = 64
H_B = W_B // HD_B
KV_B = 4
REP_B = H_B // KV_B
KV_W = KV_B * HD_B
WINDOW = 128
ROPE_DIM = HD_B // 4
ROPE_THETA = 500000.0
W_C = 512
GC = 16
G_C = W_C // GC
P_C = 64
N_STATE = G_C * P_C
D_IN = 3 * W_A + 2 * W_B + 2 * KV_W + 2 * W_C
EPS = 1e-5
NEG = -1e30

O_UA, O_VA, O_GA = 0, W_A, 2 * W_A
O_Q = 3 * W_A
O_K = O_Q + W_B
O_V = O_K + KV_W
O_GB = O_V + KV_W
O_UC = O_GB + W_B
O_GC = O_UC + W_C

LANES = 128
SUBLANES = 8
N_TILE = N_STATE // LANES
VMEM_LIMIT = 56 * 1024 * 1024

PROJ_ROWS = 256
PROJ_COLS = 512
SCAN_VREGS = 4
SUB_T = CHUNK // SUBLANES


def _rms(x, g):
    return x * lax.rsqrt(jnp.mean(x * x, axis=-1, keepdims=True) + EPS) * g


def _in_proj_kernel(x_ref, g_ref, w_ref, z_ref):
    xn = _rms(x_ref[...], g_ref[...]).astype(BF16)
    for n0 in range(0, D_IN, PROJ_COLS):
        z_ref[:, n0:n0 + PROJ_COLS] = jnp.dot(
            xn, w_ref[:, n0:n0 + PROJ_COLS], preferred_element_type=F32)


def _in_proj(x2d, g, w):
    m = x2d.shape[0]
    tm = min(PROJ_ROWS, m)
    return pl.pallas_call(
        _in_proj_kernel,
        grid=(m // tm,),
        in_specs=[
            pl.BlockSpec((tm, D_MODEL), lambda i: (i, 0)),
            pl.BlockSpec((1, D_MODEL), lambda i: (0, 0)),
            pl.BlockSpec((D_MODEL, D_IN), lambda i: (0, 0), pipeline_mode=pl.Buffered(1)),
        ],
        out_specs=pl.BlockSpec((tm, D_IN), lambda i: (i, 0)),
        out_shape=jax.ShapeDtypeStruct((m, D_IN), F32),
        compiler_params=pltpu.CompilerParams(
            dimension_semantics=("arbitrary",), vmem_limit_bytes=VMEM_LIMIT),
        name="in_proj",
    )(x2d, g, w)


def _out_proj_kernel(m_ref, x_ref, w_ref, fg_ref, y_ref, *, final):
    y = x_ref[...] + jnp.dot(m_ref[...].astype(BF16), w_ref[...], preferred_element_type=F32)
    if final:
        y = _rms(y, fg_ref[...])
    y_ref[...] = y


def _out_proj(mixed, x2d, w, fg, final):
    m = x2d.shape[0]
    tm = min(PROJ_ROWS, m)
    return pl.pallas_call(
        functools.partial(_out_proj_kernel, final=final),
        grid=(m // tm,),
        in_specs=[
            pl.BlockSpec((tm, D_MODEL), lambda i: (i, 0)),
            pl.BlockSpec((tm, D_MODEL), lambda i: (i, 0)),
            pl.BlockSpec((D_MODEL, D_MODEL), lambda i: (0, 0), pipeline_mode=pl.Buffered(1)),
            pl.BlockSpec((1, D_MODEL), lambda i: (0, 0)),
        ],
        out_specs=pl.BlockSpec((tm, D_MODEL), lambda i: (i, 0)),
        out_shape=jax.ShapeDtypeStruct((m, D_MODEL), F32),
        compiler_params=pltpu.CompilerParams(
            dimension_semantics=("arbitrary",), vmem_limit_bytes=VMEM_LIMIT),
        name="out_proj_final" if final else "out_proj",
    )(mixed, x2d, w, fg)


def _s5_prep_kernel(are_ref, aim_ref, ldt_ref, btr_ref, bti_ref,
                    abr_ref, abi_ref, pwr_ref, pwi_ref, bbr_ref, bbi_ref):
    a_re, a_im = are_ref[...], aim_ref[...]
    dt = jnp.exp(ldt_ref[...])
    mag = jnp.exp(a_re * dt)
    ab_re = mag * jnp.cos(a_im * dt)
    ab_im = mag * jnp.sin(a_im * dt)
    nr, ni = ab_re - 1.0, ab_im
    den = a_re * a_re + a_im * a_im
    f_re = (nr * a_re + ni * a_im) / den
    f_im = (ni * a_re - nr * a_im) / den
    br, bi = btr_ref[...], bti_ref[...]
    bbr_ref[...] = f_re * br - f_im * bi
    bbi_ref[...] = f_re * bi + f_im * br
    abr_ref[...] = ab_re
    abi_ref[...] = ab_im
    p_re, p_im = ab_re, ab_im
    for j in range(SUB_T):
        pwr_ref[:, j:j + 1, :] = p_re
        pwi_ref[:, j:j + 1, :] = p_im
        p_re, p_im = p_re * ab_re - p_im * ab_im, p_re * ab_im + p_im * ab_re


def _s5_prep(a_re, a_im, log_dt, b_re, b_im):
    a4 = lambda a: a.reshape(DEPTH, G_C, 1, P_C)
    bt = lambda b: jnp.transpose(b, (0, 1, 3, 2))
    vec = pl.BlockSpec((None, G_C, 1, P_C), lambda l: (l, 0, 0, 0))
    dts = pl.BlockSpec((None, G_C, 1, 1), lambda l: (l, 0, 0, 0))
    mat = pl.BlockSpec((None, G_C, GC, P_C), lambda l: (l, 0, 0, 0))
    pws = pl.BlockSpec((None, G_C, SUB_T, P_C), lambda l: (l, 0, 0, 0))
    vshape = jax.ShapeDtypeStruct((DEPTH, G_C, 1, P_C), F32)
    mshape = jax.ShapeDtypeStruct((DEPTH, G_C, GC, P_C), F32)
    pshape = jax.ShapeDtypeStruct((DEPTH, G_C, SUB_T, P_C), F32)
    return pl.pallas_call(
        _s5_prep_kernel,
        grid=(DEPTH,),
        in_specs=[vec, vec, dts, mat, mat],
        out_specs=[vec, vec, pws, pws, mat, mat],
        out_shape=[vshape, vshape, pshape, pshape, mshape, mshape],
        name="s5_prep",
    )(a4(a_re), a4(a_im), log_dt.reshape(DEPTH, G_C, 1, 1), bt(b_re), bt(b_im))


def _rope(x, tab_ref):
    c, s_lo, s_hi = tab_ref[0], tab_ref[1], tab_ref[2]
    half = ROPE_DIM // 2
    tiles = []
    for j in range(x.shape[1] // LANES):
        t = x[:, j * LANES:(j + 1) * LANES]
        tiles.append(t * c + pltpu.roll(t, half, 1) * s_lo + pltpu.roll(t, LANES - half, 1) * s_hi)
    return jnp.concatenate(tiles, axis=1) if len(tiles) > 1 else tiles[0]


def _chunk_mlp(ws_ref, bias_ref, z_ref, rows, keep):
    outs = []
    for h in range(H_A):
        cols = slice(h * C_A, (h + 1) * C_A)
        w = jnp.where(keep, ws_ref[h], 0.0).astype(BF16)
        v = z_ref[rows, O_VA + h * C_A:O_VA + (h + 1) * C_A].astype(BF16)
        zz = jnp.dot(w, v, preferred_element_type=F32) + bias_ref[:, h:h + 1]
        u = z_ref[rows, O_UA + h * C_A:O_UA + (h + 1) * C_A]
        g = z_ref[rows, O_GA + h * C_A:O_GA + (h + 1) * C_A]
        del cols
        outs.append(u * zz * jax.nn.silu(g))
    return jnp.concatenate(outs, axis=1)


def _attention(q, kcat, vcat, sinks_ref, prev_off):
    t_len = q.shape[0]
    lane_grp = lax.broadcasted_iota(jnp.int32, (t_len, KV_W), 1) // HD_B
    pieces = []
    for r in range(REP_B):
        chunk = q[:, r * KV_W:(r + 1) * KV_W]
        for g in range(KV_B):
            pieces.append(jnp.where(lane_grp == g, chunk, 0.0))
    qbd = jnp.concatenate(pieces, axis=0).astype(BF16)
    s_all = lax.dot_general(qbd, kcat, (((1,), (1,)), ((), ())),
                            preferred_element_type=F32)
    row = lax.broadcasted_iota(jnp.int32, (t_len, 2 * WINDOW), 0)
    col = lax.broadcasted_iota(jnp.int32, (t_len, 2 * WINDOW), 1)
    visible = jnp.where(col < WINDOW, col - row - prev_off, row - col + WINDOW + 1) > 0
    probs = []
    for r in range(REP_B):
        for g in range(KV_B):
            hh = r * KV_B + g
            sink = sinks_ref[g * REP_B + r]
            s = jnp.where(visible, s_all[hh * t_len:(hh + 1) * t_len], NEG)
            m = jnp.maximum(jnp.max(s, axis=-1, keepdims=True), sink)
            p = jnp.exp(s - m)
            den = jnp.sum(p, axis=-1, keepdims=True) + jnp.exp(sink - m)
            probs.append((p * (1.0 / den)).astype(BF16))
    p_all = jnp.concatenate(probs, axis=0)
    o_all = jnp.dot(p_all, vcat, preferred_element_type=F32)
    outs = []
    for r in range(REP_B):
        acc = None
        for g in range(KV_B):
            hh = r * KV_B + g
            o = o_all[hh * t_len:(hh + 1) * t_len]
            acc = o if acc is None else jnp.where(lane_grp == g, o, acc)
        outs.append(acc)
    return jnp.concatenate(outs, axis=1)


def _lane_tile(n):
    return slice(n * LANES, (n + 1) * LANES)


def _s5_drive(h_ref, z_ref, bcat_ref):
    bu = jnp.dot(z_ref[:, O_UC:O_UC + W_C].astype(BF16), bcat_ref[...],
                 preferred_element_type=F32)
    for n in range(2 * N_TILE):
        h_ref[n] = bu[:, _lane_tile(n)]


def _s5_scan(h_ref, a_ref, n_seq, n_step, init):
    group = max(1, SCAN_VREGS * SUBLANES // n_seq)
    finals = [None] * N_TILE
    for n0 in range(0, N_TILE, group):
        tiles = range(n0, n0 + group)
        state = {n: init(n) for n in tiles}
        for j in range(n_step):
            rows = pl.ds(j, n_seq, stride=n_step)
            for n in tiles:
                ar, ai = a_ref[0:1, _lane_tile(n)], a_ref[1:2, _lane_tile(n)]
                hr, hi = state[n]
                hr, hi = (ar * hr - ai * hi + h_ref[n, rows, :],
                          ar * hi + ai * hr + h_ref[N_TILE + n, rows, :])
                h_ref[n, rows, :] = hr
                h_ref[N_TILE + n, rows, :] = hi
                state[n] = (hr, hi)
        for n in tiles:
            finals[n] = state[n]
    return finals


def _s5_readout(h_ref, z_ref, rows, ccat_ref, d_ref, wglu_ref, bglu_ref):
    uc = z_ref[rows, O_UC:O_UC + W_C]
    hs = jnp.concatenate([h_ref[n].astype(BF16) for n in range(2 * N_TILE)], axis=1)
    y = jnp.dot(hs, ccat_ref[...], preferred_element_type=F32)
    y = jax.nn.gelu(y + d_ref[...] * uc)
    gate = jnp.dot(y.astype(BF16), wglu_ref[...], preferred_element_type=F32) + bglu_ref[...]
    y = y * jax.nn.sigmoid(gate)
    return y * jax.nn.silu(z_ref[rows, O_GC:O_GC + W_C])


def _mix_prompt_kernel(sinks_ref, z_ref, rope_ref, ws_ref, bias_ref, a_ref, pwr_ref, pwi_ref,
                       bcat_ref, ccat_ref, d_ref, wglu_ref, bglu_ref,
                       mixed_ref, klast_ref, vlast_ref, hfin_ref,
                       kprev_ref, vprev_ref, h_ref, hin_ref, carry_ref):
    i = pl.program_id(1)
    rows = slice(None)

    @pl.when(i == 0)
    def _():
        kprev_ref[...] = jnp.zeros_like(kprev_ref)
        vprev_ref[...] = jnp.zeros_like(vprev_ref)
        carry_ref[...] = jnp.zeros_like(carry_ref)

    r_i = lax.broadcasted_iota(jnp.int32, (CHUNK, CHUNK), 0)
    c_i = lax.broadcasted_iota(jnp.int32, (CHUNK, CHUNK), 1)
    mixed_ref[:, 0:W_A] = _chunk_mlp(ws_ref, bias_ref, z_ref, rows, c_i <= r_i).astype(BF16)

    q = _rope(z_ref[:, O_Q:O_Q + W_B], rope_ref) * (HD_B ** -0.5)
    k = _rope(z_ref[:, O_K:O_K + KV_W], rope_ref)
    v = z_ref[:, O_V:O_V + KV_W]
    klast_ref[...] = k
    vlast_ref[...] = v
    kb, vb = k.astype(BF16), v.astype(BF16)
    kcat = jnp.concatenate([kprev_ref[...], kb], axis=0)
    vcat = jnp.concatenate([vprev_ref[...], vb], axis=0)
    prev_off = jnp.where(i > 0, 0, 2 * WINDOW)
    o_b = _attention(q, kcat, vcat, sinks_ref, prev_off)
    mixed_ref[:, W_A:W_A + W_B] = (o_b * jax.nn.silu(z_ref[:, O_GB:O_GB + W_B])).astype(BF16)
    kprev_ref[...] = kb
    vprev_ref[...] = vb

    _s5_drive(h_ref, z_ref, bcat_ref)
    zero = jnp.zeros((SUBLANES, LANES), F32)
    finals = _s5_scan(h_ref, a_ref, SUBLANES, SUB_T, lambda n: (zero, zero))
    for n in range(N_TILE):
        re_l = _lane_tile(n)
        im_l = _lane_tile(N_TILE + n)
        end_r, end_i = finals[n]
        a_t_r, a_t_i = pwr_ref[SUB_T - 1:SUB_T, re_l], pwi_ref[SUB_T - 1:SUB_T, re_l]
        cr, ci = carry_ref[0:1, re_l], carry_ref[1:2, re_l]
        for r in range(SUBLANES):
            hin_ref[r:r + 1, re_l] = cr
            hin_ref[r:r + 1, im_l] = ci
            cr, ci = (end_r[r:r + 1] + a_t_r * cr - a_t_i * ci,
                      end_i[r:r + 1] + a_t_r * ci + a_t_i * cr)
        carry_ref[0:1, re_l] = cr
        carry_ref[1:2, re_l] = ci
        in_r, in_i = hin_ref[:, re_l], hin_ref[:, im_l]
        for j in range(SUB_T):
            seq = pl.ds(j, SUBLANES, stride=SUB_T)
            p_r, p_i = pwr_ref[j:j + 1, re_l], pwi_ref[j:j + 1, re_l]
            h_ref[n, seq, :] = h_ref[n, seq, :] + (p_r * in_r - p_i * in_i)
            h_ref[N_TILE + n, seq, :] = h_ref[N_TILE + n, seq, :] + (p_r * in_i + p_i * in_r)
    hfin_ref[...] = carry_ref[...]
    mixed_ref[:, W_A + W_B:] = _s5_readout(
        h_ref, z_ref, rows, ccat_ref, d_ref, wglu_ref, bglu_ref).astype(BF16)


def _mix_prompt(z3, sinks, rope_tab, ws, bias_t, a_row, pw_re, pw_im, bcat, ccat, d, wglu, bglu):
    bsz, seq, _ = z3.shape
    nb = seq // CHUNK
    const = lambda shape: pl.BlockSpec(shape, lambda b, i: (0,) * len(shape))
    return pl.pallas_call(
        _mix_prompt_kernel,
        grid=(bsz, nb),
        in_specs=[
            pl.BlockSpec(memory_space=pltpu.SMEM),
            pl.BlockSpec((None, CHUNK, D_IN), lambda b, i: (b, i, 0)),
            pl.BlockSpec((3, CHUNK, LANES), lambda b, i: (0, i, 0)),
            const((H_A, CHUNK, CHUNK)),
            const((CHUNK, H_A)),
            const((2, N_STATE)),
            const((SUB_T, N_STATE)),
            const((SUB_T, N_STATE)),
            const((W_C, 2 * N_STATE)),
            const((2 * N_STATE, W_C)),
            const((1, W_C)),
            const((W_C, W_C)),
            const((1, W_C)),
        ],
        out_specs=[
            pl.BlockSpec((None, CHUNK, D_MODEL), lambda b, i: (b, i, 0)),
            pl.BlockSpec((None, WINDOW, KV_W), lambda b, i: (b, 0, 0)),
            pl.BlockSpec((None, WINDOW, KV_W), lambda b, i: (b, 0, 0)),
            pl.BlockSpec((None, 2, N_STATE), lambda b, i: (b, 0, 0)),
        ],
        out_shape=[
            jax.ShapeDtypeStruct((bsz, seq, D_MODEL), BF16),
            jax.ShapeDtypeStruct((bsz, WINDOW, KV_W), F32),
            jax.ShapeDtypeStruct((bsz, WINDOW, KV_W), F32),
            jax.ShapeDtypeStruct((bsz, 2, N_STATE), F32),
        ],
        scratch_shapes=[
            pltpu.VMEM((WINDOW, KV_W), BF16),
            pltpu.VMEM((WINDOW, KV_W), BF16),
            pltpu.VMEM((2 * N_TILE, CHUNK, LANES), F32),
            pltpu.VMEM((SUBLANES, 2 * N_STATE), F32),
            pltpu.VMEM((2, N_STATE), F32),
        ],
        compiler_params=pltpu.CompilerParams(
            dimension_semantics=("arbitrary", "arbitrary"), vmem_limit_bytes=VMEM_LIMIT),
        name="mix_prompt",
    )(sinks, z3, rope_tab, ws, bias_t, a_row, pw_re, pw_im, bcat, ccat, d, wglu, bglu)


def _mix_sample_kernel(sinks_ref, z_ref, rope_ref, ck_ref, cv_ref, ws_ref, bias_ref, a_ref,
                       h0r_ref, h0i_ref, bcat_ref, ccat_ref, d_ref, wglu_ref, bglu_ref,
                       mixed_ref, knew_ref, hre_ref, him_ref,
                       q_ref, h_ref, *, n_batch, n_tok):
    b = pl.program_id(0)
    m = n_batch * n_tok

    @pl.when(b == 0)
    def _():
        rows = slice(None)
        r_i = lax.broadcasted_iota(jnp.int32, (m, m), 0)
        c_i = lax.broadcasted_iota(jnp.int32, (m, m), 1)
        keep = (r_i // n_tok == c_i // n_tok) & (c_i <= r_i)
        mixed_ref[:, 0:W_A] = _chunk_mlp(ws_ref, bias_ref, z_ref, rows, keep)
        q_ref[...] = _rope(z_ref[:, O_Q:O_Q + W_B], rope_ref) * (HD_B ** -0.5)
        knew_ref[...] = _rope(z_ref[:, O_K:O_K + KV_W], rope_ref)
        _s5_drive(h_ref, z_ref, bcat_ref)
        finals = _s5_scan(h_ref, a_ref, n_batch, n_tok,
                          lambda n: (h0r_ref[:, _lane_tile(n)], h0i_ref[:, _lane_tile(n)]))
        for n in range(N_TILE):
            hre_ref[:, _lane_tile(n)] = finals[n][0]
            him_ref[:, _lane_tile(n)] = finals[n][1]
        mixed_ref[:, W_A + W_B:] = _s5_readout(
            h_ref, z_ref, rows, ccat_ref, d_ref, wglu_ref, bglu_ref)

    rows = pl.ds(pl.multiple_of(b * n_tok, n_tok), n_tok)
    pad = jnp.zeros((WINDOW - n_tok, KV_W), F32)
    kcat = jnp.concatenate([ck_ref[...], knew_ref[rows, :], pad], axis=0).astype(BF16)
    vcat = jnp.concatenate([cv_ref[...], z_ref[rows, O_V:O_V + KV_W], pad], axis=0).astype(BF16)
    o_b = _attention(q_ref[rows, :], kcat, vcat, sinks_ref, 0)
    mixed_ref[rows, W_A:W_A + W_B] = o_b * jax.nn.silu(z_ref[rows, O_GB:O_GB + W_B])


def _mix_sample(z2, sinks, rope_tab, ck, cv, ws_t, bias_t, a_row, h0r, h0i,
                bcat, ccat, d, wglu, bglu, n_batch, n_tok):
    m = n_batch * n_tok
    const = lambda shape: pl.BlockSpec(shape, lambda b: (0,) * len(shape))
    return pl.pallas_call(
        functools.partial(_mix_sample_kernel, n_batch=n_batch, n_tok=n_tok),
        grid=(n_batch,),
        in_specs=[
            pl.BlockSpec(memory_space=pltpu.SMEM),
            const((m, D_IN)),
            const((3, m, LANES)),
            pl.BlockSpec((None, WINDOW, KV_W), lambda b: (b, 0, 0)),
            pl.BlockSpec((None, WINDOW, KV_W), lambda b: (b, 0, 0)),
            const((H_A, m, m)),
            const((m, H_A)),
            const((2, N_STATE)),
            const((n_batch, N_STATE)),
            const((n_batch, N_STATE)),
            const((W_C, 2 * N_STATE)),
            const((2 * N_STATE, W_C)),
            const((1, W_C)),
            const((W_C, W_C)),
            const((1, W_C)),
        ],
        out_specs=[
            const((m, D_MODEL)),
            const((m, KV_W)),
            const((n_batch, N_STATE)),
            const((n_batch, N_STATE)),
        ],
        out_shape=[
            jax.ShapeDtypeStruct((m, D_MODEL), F32),
            jax.ShapeDtypeStruct((m, KV_W), F32),
            jax.ShapeDtypeStruct((n_batch, N_STATE), F32),
            jax.ShapeDtypeStruct((n_batch, N_STATE), F32),
        ],
        scratch_shapes=[
            pltpu.VMEM((m, W_B), F32),
            pltpu.VMEM((2 * N_TILE, m, LANES), F32),
        ],
        compiler_params=pltpu.CompilerParams(
            dimension_semantics=("arbitrary",), vmem_limit_bytes=VMEM_LIMIT),
        name="mix_sample",
    )(sinks, z2, rope_tab, ck, cv, ws_t, bias_t, a_row, h0r, h0i, bcat, ccat, d, wglu, bglu)


def _rope_table(pos):
    half = ROPE_DIM // 2
    inv = ROPE_THETA ** (-jnp.arange(half, dtype=F32) * 2.0 / ROPE_DIM)
    ang = pos[:, None] * inv[None, :]
    cos, sin = jnp.cos(ang), jnp.sin(ang)
    n = pos.shape[0]
    rest = HD_B - ROPE_DIM
    c = jnp.concatenate([cos, cos, jnp.ones((n, rest), F32)], axis=1)
    s_lo = jnp.concatenate([jnp.zeros((n, half), F32), sin, jnp.zeros((n, rest), F32)], axis=1)
    s_hi = jnp.concatenate([-sin, jnp.zeros((n, half + rest), F32)], axis=1)
    tab = jnp.stack([c, s_lo, s_hi])
    return jnp.concatenate([tab] * (LANES // HD_B), axis=2)


def kernel(x_prompt, x_sample, cache_swa_k, cache_swa_v, state_ssm_re, state_ssm_im, norm_g,
           final_norm_g, w_in, w_out, chunk_w_s, chunk_b_s, attn_sinks, ssm_a_re, ssm_a_im,
           ssm_log_dt, ssm_b_re, ssm_b_im, ssm_c_re, ssm_c_im, ssm_d, glu_w, glu_b):
    bsz, seq, _ = x_prompt.shape
    n_batch, n_tok, _ = x_sample.shape
    m_s = n_batch * n_tok

    def heads_last(w):
        lead = w.shape[:-1]
        w = w.reshape(*lead, KV_B, REP_B, HD_B)
        return jnp.swapaxes(w, -3, -2).reshape(*lead, W_B)

    w_in_b = jnp.concatenate([
        w_in[:, :, :O_Q], heads_last(w_in[:, :, O_Q:O_K]), w_in[:, :, O_K:O_GB],
        heads_last(w_in[:, :, O_GB:O_UC]), w_in[:, :, O_UC:]], axis=2).astype(BF16)
    w_out_mid = jnp.swapaxes(heads_last(jnp.swapaxes(w_out[:, W_A:W_A + W_B], 1, 2)), 1, 2)
    w_out_b = jnp.concatenate(
        [w_out[:, :W_A], w_out_mid, w_out[:, W_A + W_B:]], axis=1).astype(BF16)
    wglu_b = glu_w.astype(BF16)

    ab_re, ab_im, pw_re, pw_im, bb_re, bb_im = _s5_prep(
        ssm_a_re, ssm_a_im, ssm_log_dt, ssm_b_re, ssm_b_im)
    a_rows = jnp.concatenate([ab_re.reshape(DEPTH, 1, N_STATE),
                              ab_im.reshape(DEPTH, 1, N_STATE)], axis=1)
    pw_re = jnp.transpose(pw_re, (0, 2, 1, 3)).reshape(DEPTH, SUB_T, N_STATE)
    pw_im = jnp.transpose(pw_im, (0, 2, 1, 3)).reshape(DEPTH, SUB_T, N_STATE)
    eye = jnp.eye(G_C, dtype=F32)
    b_dense = lambda bb: jnp.einsum('lgcp,gh->lgchp', bb, eye).reshape(DEPTH, W_C, N_STATE)
    bcat = jnp.concatenate([b_dense(bb_re), b_dense(bb_im)], axis=2).astype(BF16)
    c_dense = lambda c: jnp.einsum('lgcp,gh->lgphc', c, eye).reshape(DEPTH, N_STATE, W_C)
    ccat = jnp.concatenate([c_dense(ssm_c_re), -c_dense(ssm_c_im)], axis=1).astype(BF16)

    rope_p = _rope_table(jnp.arange(seq, dtype=F32))
    rope_s = jnp.tile(_rope_table(jnp.arange(n_tok, dtype=F32) + PAST_LEN), (1, n_batch, 1))
    bias_p = jnp.transpose(chunk_b_s, (0, 2, 1))
    bias_s = jnp.tile(bias_p[:, :n_tok], (1, n_batch, 1))
    ws_s = jnp.tile(chunk_w_s[:, :, :n_tok, :n_tok], (1, 1, n_batch, n_batch))
    ck = cache_swa_k.reshape(DEPTH, n_batch, WINDOW, KV_W)
    cv = cache_swa_v.reshape(DEPTH, n_batch, WINDOW, KV_W)
    h0r = state_ssm_re.reshape(DEPTH, n_batch, N_STATE)
    h0i = state_ssm_im.reshape(DEPTH, n_batch, N_STATE)
    fg = final_norm_g.reshape(1, D_MODEL)

    xp = x_prompt.reshape(bsz * seq, D_MODEL)
    xs = x_sample.reshape(m_s, D_MODEL)
    outs = [[] for _ in range(9)]
    for l in range(DEPTH):
        g_l = norm_g[l].reshape(1, D_MODEL)
        d_l = ssm_d[l].reshape(1, W_C)
        bglu_l = glu_b[l].reshape(1, W_C)
        last = l == DEPTH - 1

        zp = _in_proj(xp, g_l, w_in_b[l])
        mixed_p, k_last, v_last, h_fin = _mix_prompt(
            zp.reshape(bsz, seq, D_IN), attn_sinks[l], rope_p, chunk_w_s[l], bias_p[l],
            a_rows[l], pw_re[l], pw_im[l], bcat[l], ccat[l], d_l, wglu_b[l], bglu_l)
        xp = _out_proj(mixed_p.reshape(bsz * seq, D_MODEL), xp, w_out_b[l], fg, last)

        zs = _in_proj(xs, g_l, w_in_b[l])
        mixed_s, k_new, h_re_s, h_im_s = _mix_sample(
            zs, attn_sinks[l], rope_s, ck[l], cv[l], ws_s[l], bias_s[l], a_rows[l],
            h0r[l], h0i[l], bcat[l], ccat[l], d_l, wglu_b[l], bglu_l, n_batch, n_tok)
        xs = _out_proj(mixed_s, xs, w_out_b[l], fg, last)

        outs[0].append(k_last.reshape(bsz, WINDOW, KV_B, HD_B))
        outs[1].append(v_last.reshape(bsz, WINDOW, KV_B, HD_B))
        outs[2].append(k_new.reshape(n_batch, n_tok, KV_B, HD_B))
        outs[3].append(zs[:, O_V:O_V + KV_W].reshape(n_batch, n_tok, KV_B, HD_B))
        outs[4].append(h_fin[:, 0].reshape(bsz, G_C, P_C))
        outs[5].append(h_fin[:, 1].reshape(bsz, G_C, P_C))
        outs[6].append(h_re_s.reshape(n_batch, G_C, P_C))
        outs[7].append(h_im_s.reshape(n_batch, G_C, P_C))
        outs[8].append(zs[:, O_VA:O_VA + W_A].reshape(n_batch, n_tok, W_A))

    return (xp.reshape(bsz, seq, D_MODEL), xs.reshape(n_batch, n_tok, D_MODEL),
            *[jnp.stack(o) for o in outs])
```

```python
import functools

import jax
import jax.numpy as jnp
from jax import lax
from jax.experimental import pallas as pl
from jax.experimental.pallas import tpu as pltpu

F32 = jnp.float32
BF16 = jnp.bfloat16

D_MODEL = 2048
DEPTH = 4
PAST_LEN = 16384
CHUNK = 128
W_A = 512
H_A = 4
C_A = W_A // H_A
W_B = 1024
HD_B = 64
H_B = W_B // HD_B
KV_B = 4
REP_B = H_B // KV_B
KV_W = KV_B * HD_B
WINDOW = 128
ROPE_DIM = HD_B // 4
ROPE_THETA = 500000.0
W_C = 512
GC = 16
G_C = W_C // GC
P_C = 64
N_STATE = G_C * P_C
D_IN = 3 * W_A + 2 * W_B + 2 * KV_W + 2 * W_C
EPS = 1e-5
NEG = -1e30

O_UA, O_VA, O_GA = 0, W_A, 2 * W_A
O_Q = 3 * W_A
O_K = O_Q + W_B
O_V = O_K + KV_W
O_GB = O_V + KV_W
O_UC = O_GB + W_B
O_GC = O_UC + W_C

LANES = 128
SUBLANES = 8
N_TILE = N_STATE // LANES
VMEM_LIMIT = 56 * 1024 * 1024

PROJ_ROWS = 256
PROJ_COLS = 512
SCAN_VREGS = 4
SUB_T = CHUNK // SUBLANES

_NT = (((1,), (1,)), ((), ()))


def _rms(x, g):
    return x * lax.rsqrt(jnp.mean(x * x, axis=-1, keepdims=True) + EPS) * g


def _layer_spec(shape, layer, **kw):
    zeros = (0,) * len(shape)
    return pl.BlockSpec((None,) + tuple(shape), lambda *_: (layer,) + zeros, **kw)


def _in_proj_kernel(x_ref, g_ref, w_ref, z_ref):
    xn = _rms(x_ref[...], g_ref[...]).astype(BF16)
    for n0 in range(0, D_IN, PROJ_COLS):
        z_ref[:, n0:n0 + PROJ_COLS] = jnp.dot(
            xn, w_ref[:, n0:n0 + PROJ_COLS], preferred_element_type=F32)


def _in_proj(x2d, g_all, w_all, layer):
    m = x2d.shape[0]
    tm = min(PROJ_ROWS, m)
    return pl.pallas_call(
        _in_proj_kernel,
        grid=(m // tm,),
        in_specs=[
            pl.BlockSpec((tm, D_MODEL), lambda i: (i, 0)),
            _layer_spec((1, D_MODEL), layer),
            _layer_spec((D_MODEL, D_IN), layer, pipeline_mode=pl.Buffered(1)),
        ],
        out_specs=pl.BlockSpec((tm, D_IN), lambda i: (i, 0)),
        out_shape=jax.ShapeDtypeStruct((m, D_IN), F32),
        compiler_params=pltpu.CompilerParams(
            dimension_semantics=("arbitrary",), vmem_limit_bytes=VMEM_LIMIT),
        name="in_proj",
    )(x2d, g_all, w_all)


def _out_proj_kernel(m_ref, x_ref, w_ref, fg_ref, y_ref, *, final):
    y = x_ref[...] + jnp.dot(m_ref[...].astype(BF16), w_ref[...], preferred_element_type=F32)
    if final:
        y = _rms(y, fg_ref[...])
    y_ref[...] = y


def _out_proj(mixed, x2d, w_all, fg, layer):
    m = x2d.shape[0]
    tm = min(PROJ_ROWS, m)
    final = layer == DEPTH - 1
    return pl.pallas_call(
        functools.partial(_out_proj_kernel, final=final),
        grid=(m // tm,),
        in_specs=[
            pl.BlockSpec((tm, D_MODEL), lambda i: (i, 0)),
            pl.BlockSpec((tm, D_MODEL), lambda i: (i, 0)),
            _layer_spec((D_MODEL, D_MODEL), layer, pipeline_mode=pl.Buffered(1)),
            pl.BlockSpec((1, D_MODEL), lambda i: (0, 0)),
        ],
        out_specs=pl.BlockSpec((tm, D_MODEL), lambda i: (i, 0)),
        out_shape=jax.ShapeDtypeStruct((m, D_MODEL), F32),
        compiler_params=pltpu.CompilerParams(
            dimension_semantics=("arbitrary",), vmem_limit_bytes=VMEM_LIMIT),
        name="out_proj_final" if final else "out_proj",
    )(mixed, x2d, w_all, fg)


def _store_block_diag(out_ref, lane0, x3):
    per_tile = LANES // P_C
    lane_slot = lax.broadcasted_iota(jnp.int32, (GC, LANES), 1) // P_C
    for g in range(G_C):
        xg = x3[g]
        pair = jnp.concatenate([xg] * per_tile, axis=1)
        tile = jnp.where(lane_slot == g % per_tile, pair, 0.0)
        l0 = lane0 + (g // per_tile) * LANES
        out_ref[g * GC:(g + 1) * GC, l0:l0 + LANES] = tile.astype(BF16)


def _s5_prep_kernel(are_ref, aim_ref, ldt_ref, btr_ref, bti_ref, cre_ref, cim_ref,
                    abr_ref, abi_ref, pwr_ref, pwi_ref, bcat_ref, ccat_ref):
    a_re, a_im = are_ref[...], aim_ref[...]
    dt = jnp.exp(ldt_ref[...])
    mag = jnp.exp(a_re * dt)
    ab_re = mag * jnp.cos(a_im * dt)
    ab_im = mag * jnp.sin(a_im * dt)
    nr, ni = ab_re - 1.0, ab_im
    den = a_re * a_re + a_im * a_im
    f_re = (nr * a_re + ni * a_im) / den
    f_im = (ni * a_re - nr * a_im) / den
    br, bi = btr_ref[...], bti_ref[...]
    bcat_ref[...] = jnp.zeros_like(bcat_ref)
    ccat_ref[...] = jnp.zeros_like(ccat_ref)
    _store_block_diag(bcat_ref, 0, f_re * br - f_im * bi)
    _store_block_diag(bcat_ref, N_STATE, f_re * bi + f_im * br)
    _store_block_diag(ccat_ref, 0, cre_ref[...])
    _store_block_diag(ccat_ref, N_STATE, -cim_ref[...])
    abr_ref[...] = ab_re
    abi_ref[...] = ab_im
    p_re, p_im = ab_re, ab_im
    for j in range(SUB_T):
        pwr_ref[:, j:j + 1, :] = p_re
        pwi_ref[:, j:j + 1, :] = p_im
        p_re, p_im = p_re * ab_re - p_im * ab_im, p_re * ab_im + p_im * ab_re


def _s5_prep(a_re, a_im, log_dt, b_re, b_im, c_re, c_im):
    a4 = lambda a: a.reshape(DEPTH, G_C, 1, P_C)
    bt = lambda b: jnp.transpose(b, (0, 1, 3, 2))
    vec = pl.BlockSpec((None, G_C, 1, P_C), lambda l: (l, 0, 0, 0))
    dts = pl.BlockSpec((None, G_C, 1, 1), lambda l: (l, 0, 0, 0))
    mat = pl.BlockSpec((None, G_C, GC, P_C), lambda l: (l, 0, 0, 0))
    pws = pl.BlockSpec((None, G_C, SUB_T, P_C), lambda l: (l, 0, 0, 0))
    dense = pl.BlockSpec((None, W_C, 2 * N_STATE), lambda l: (l, 0, 0))
    vshape = jax.ShapeDtypeStruct((DEPTH, G_C, 1, P_C), F32)
    pshape = jax.ShapeDtypeStruct((DEPTH, G_C, SUB_T, P_C), F32)
    dshape = jax.ShapeDtypeStruct((DEPTH, W_C, 2 * N_STATE), BF16)
    return pl.pallas_call(
        _s5_prep_kernel,
        grid=(DEPTH,),
        in_specs=[vec, vec, dts, mat, mat, mat, mat],
        out_specs=[vec, vec, pws, pws, dense, dense],
        out_shape=[vshape, vshape, pshape, pshape, dshape, dshape],
        name="s5_prep",
    )(a4(a_re), a4(a_im), log_dt.reshape(DEPTH, G_C, 1, 1), bt(b_re), bt(b_im), c_re, c_im)


def _rope(x, tab_ref):
    c, s_lo, s_hi = tab_ref[0], tab_ref[1], tab_ref[2]
    half = ROPE_DIM // 2
    tiles = []
    for j in range(x.shape[1] // LANES):
        t = x[:, j * LANES:(j + 1) * LANES]
        tiles.append(t * c + pltpu.roll(t, half, 1) * s_lo + pltpu.roll(t, LANES - half, 1) * s_hi)
    return jnp.concatenate(tiles, axis=1) if len(tiles) > 1 else tiles[0]


def _chunk_mlp(mix_w, bias_ref, z_ref):
    outs = []
    for h in range(H_A):
        v = z_ref[:, O_VA + h * C_A:O_VA + (h + 1) * C_A].astype(BF16)
        zz = jnp.dot(mix_w(h), v, preferred_element_type=F32) + bias_ref[:, h:h + 1]
        u = z_ref[:, O_UA + h * C_A:O_UA + (h + 1) * C_A]
        g = z_ref[:, O_GA + h * C_A:O_GA + (h + 1) * C_A]
        outs.append(u * zz * jax.nn.silu(g))
    return jnp.concatenate(outs, axis=1)


def _attention(q, kcat, vcat, sink, prev_off):
    t_len = q.shape[0]
    lane_grp = lax.broadcasted_iota(jnp.int32, (t_len, KV_W), 1) // HD_B
    pieces = []
    for r in range(REP_B):
        chunk = q[:, r * KV_W:(r + 1) * KV_W]
        for g in range(KV_B):
            pieces.append(jnp.where(lane_grp == g, chunk, 0.0))
    qbd = jnp.concatenate(pieces, axis=0).astype(BF16)
    s_all = lax.dot_general(qbd, kcat, _NT, preferred_element_type=F32)
    row = lax.broadcasted_iota(jnp.int32, (t_len, 2 * WINDOW), 0)
    col = lax.broadcasted_iota(jnp.int32, (t_len, 2 * WINDOW), 1)
    visible = jnp.where(col < WINDOW, col - row - prev_off, row - col + WINDOW + 1) > 0
    probs = []
    for r in range(REP_B):
        for g in range(KV_B):
            hh = r * KV_B + g
            snk = sink(g, r)
            s = jnp.where(visible, s_all[hh * t_len:(hh + 1) * t_len], NEG)
            m = jnp.maximum(jnp.max(s, axis=-1, keepdims=True), snk)
            p = jnp.exp(s - m)
            den = jnp.sum(p, axis=-1, keepdims=True) + jnp.exp(snk - m)
            probs.append((p * (1.0 / den)).astype(BF16))
    p_all = jnp.concatenate(probs, axis=0)
    o_all = jnp.dot(p_all, vcat, preferred_element_type=F32)
    outs = []
    for r in range(REP_B):
        acc = None
        for g in range(KV_B):
            hh = r * KV_B + g
            o = o_all[hh * t_len:(hh + 1) * t_len]
            acc = o if acc is None else jnp.where(lane_grp == g, o, acc)
        outs.append(acc)
    return jnp.concatenate(outs, axis=1)


def _lane_tile(n):
    return slice(n * LANES, (n + 1) * LANES)


def _seq_major_perm(n_seq, n_step, transpose):
    m = n_seq * n_step
    row = lax.broadcasted_iota(jnp.int32, (m, m), 0)
    col = lax.broadcasted_iota(jnp.int32, (m, m), 1)
    if transpose:
        hit = row == (col % n_seq) * n_step + col // n_seq
    else:
        hit = col == (row % n_seq) * n_step + row // n_seq
    return jnp.where(hit, 1.0, 0.0).astype(BF16)


def _s5_drive(h_ref, z_ref, bcat_ref, n_seq, n_step):
    u = z_ref[:, O_UC:O_UC + W_C].astype(BF16)
    u = jnp.dot(_seq_major_perm(n_seq, n_step, False), u, preferred_element_type=F32).astype(BF16)
    h_ref[...] = jnp.dot(u, bcat_ref[...], preferred_element_type=F32)


def _s5_scan(h_ref, a_ref, n_seq, n_step, init):
    group = max(1, SCAN_VREGS * SUBLANES // n_seq)
    finals = [None] * N_TILE
    for n0 in range(0, N_TILE, group):
        tiles = range(n0, n0 + group)
        state = {n: init(n) for n in tiles}
        for j in range(n_step):
            rows = slice(j * n_seq, (j + 1) * n_seq)
            for n in tiles:
                re_l, im_l = _lane_tile(n), _lane_tile(N_TILE + n)
                ar, ai = a_ref[0:1, re_l], a_ref[1:2, re_l]
                hr, hi = state[n]
                hr, hi = (ar * hr - ai * hi + h_ref[rows, re_l],
                          ar * hi + ai * hr + h_ref[rows, im_l])
                h_ref[rows, re_l] = hr
                h_ref[rows, im_l] = hi
                state[n] = (hr, hi)
        for n in tiles:
            finals[n] = state[n]
    return finals


def _s5_readout(h_ref, z_ref, ccat_ref, d_ref, wglu_ref, bglu_ref, n_seq, n_step):
    y = lax.dot_general(h_ref[...].astype(BF16), ccat_ref[...], _NT, preferred_element_type=F32)
    y_hi = y.astype(BF16)
    y_lo = (y - y_hi.astype(F32)).astype(BF16)
    back = _seq_major_perm(n_seq, n_step, True)
    y = (jnp.dot(back, y_hi, preferred_element_type=F32)
         + jnp.dot(back, y_lo, preferred_element_type=F32))
    y = jax.nn.gelu(y + d_ref[...] * z_ref[:, O_UC:O_UC + W_C])
    gate = jnp.dot(y.astype(BF16), wglu_ref[...], preferred_element_type=F32) + bglu_ref[...]
    y = y * jax.nn.sigmoid(gate)
    return y * jax.nn.silu(z_ref[:, O_GC:O_GC + W_C])


def _mix_prompt_kernel(sinks_ref, z_ref, rope_ref, ws_ref, bias_ref, a_ref, pwr_ref, pwi_ref,
                       bcat_ref, ccat_ref, d_ref, wglu_ref, bglu_ref,
                       mixed_ref, klast_ref, vlast_ref, hfin_ref,
                       kprev_ref, vprev_ref, h_ref, hin_ref, carry_ref, *, layer):
    i = pl.program_id(1)

    @pl.when(i == 0)
    def _():
        kprev_ref[...] = jnp.zeros_like(kprev_ref)
        vprev_ref[...] = jnp.zeros_like(vprev_ref)
        carry_ref[...] = jnp.zeros_like(carry_ref)

    r_i = lax.broadcasted_iota(jnp.int32, (CHUNK, CHUNK), 0)
    c_i = lax.broadcasted_iota(jnp.int32, (CHUNK, CHUNK), 1)
    causal_w = lambda h: jnp.where(c_i <= r_i, ws_ref[h], 0.0).astype(BF16)
    mixed_ref[:, 0:W_A] = _chunk_mlp(causal_w, bias_ref, z_ref).astype(BF16)

    q = _rope(z_ref[:, O_Q:O_Q + W_B], rope_ref) * (HD_B ** -0.5)
    k = _rope(z_ref[:, O_K:O_K + KV_W], rope_ref)
    v = z_ref[:, O_V:O_V + KV_W]
    klast_ref[...] = k
    vlast_ref[...] = v
    kb, vb = k.astype(BF16), v.astype(BF16)
    kcat = jnp.concatenate([kprev_ref[...], kb], axis=0)
    vcat = jnp.concatenate([vprev_ref[...], vb], axis=0)
    prev_off = jnp.where(i > 0, 0, 2 * WINDOW)
    sink = lambda g, r: sinks_ref[layer, g * REP_B + r]
    o_b = _attention(q, kcat, vcat, sink, prev_off)
    mixed_ref[:, W_A:W_A + W_B] = (o_b * jax.nn.silu(z_ref[:, O_GB:O_GB + W_B])).astype(BF16)
    kprev_ref[...] = kb
    vprev_ref[...] = vb

    _s5_drive(h_ref, z_ref, bcat_ref, SUBLANES, SUB_T)
    zero = jnp.zeros((SUBLANES, LANES), F32)
    finals = _s5_scan(h_ref, a_ref, SUBLANES, SUB_T, lambda n: (zero, zero))
    for n in range(N_TILE):
        re_l, im_l = _lane_tile(n), _lane_tile(N_TILE + n)
        end_r, end_i = finals[n]
        a_t_r, a_t_i = pwr_ref[SUB_T - 1:SUB_T, re_l], pwi_ref[SUB_T - 1:SUB_T, re_l]
        cr, ci = carry_ref[0:1, re_l], carry_ref[1:2, re_l]
        for r in range(SUBLANES):
            hin_ref[r:r + 1, re_l] = cr
            hin_ref[r:r + 1, im_l] = ci
            cr, ci = (end_r[r:r + 1] + a_t_r * cr - a_t_i * ci,
                      end_i[r:r + 1] + a_t_r * ci + a_t_i * cr)
        carry_ref[0:1, re_l] = cr
        carry_ref[1:2, re_l] = ci
        in_r, in_i = hin_ref[:, re_l], hin_ref[:, im_l]
        for j in range(SUB_T):
            rows = slice(j * SUBLANES, (j + 1) * SUBLANES)
            p_r, p_i = pwr_ref[j:j + 1, re_l], pwi_ref[j:j + 1, re_l]
            h_ref[rows, re_l] = h_ref[rows, re_l] + (p_r * in_r - p_i * in_i)
            h_ref[rows, im_l] = h_ref[rows, im_l] + (p_r * in_i + p_i * in_r)
    hfin_ref[...] = carry_ref[...]
    mixed_ref[:, W_A + W_B:] = _s5_readout(
        h_ref, z_ref, ccat_ref, d_ref, wglu_ref, bglu_ref, SUBLANES, SUB_T).astype(BF16)


def _mix_prompt(z3, layer, sinks, rope_tab, ws, bias_t, a_rows, pw_re, pw_im, bcat, ccat,
                d, wglu, bglu):
    bsz, seq, _ = z3.shape
    nb = seq // CHUNK
    per_layer = lambda *shape: _layer_spec(shape, layer)
    return pl.pallas_call(
        functools.partial(_mix_prompt_kernel, layer=layer),
        grid=(bsz, nb),
        in_specs=[
            pl.BlockSpec(memory_space=pltpu.SMEM),
            pl.BlockSpec((None, CHUNK, D_IN), lambda b, i: (b, i, 0)),
            pl.BlockSpec((3, CHUNK, LANES), lambda b, i: (0, i, 0)),
            per_layer(H_A, CHUNK, CHUNK),
            per_layer(CHUNK, H_A),
            per_layer(2, N_STATE),
            per_layer(SUB_T, N_STATE),
            per_layer(SUB_T, N_STATE),
            per_layer(W_C, 2 * N_STATE),
            per_layer(W_C, 2 * N_STATE),
            per_layer(1, W_C),
            per_layer(W_C, W_C),
            per_layer(1, W_C),
        ],
        out_specs=[
            pl.BlockSpec((None, CHUNK, D_MODEL), lambda b, i: (b, i, 0)),
            pl.BlockSpec((None, WINDOW, KV_W), lambda b, i: (b, 0, 0)),
            pl.BlockSpec((None, WINDOW, KV_W), lambda b, i: (b, 0, 0)),
            pl.BlockSpec((None, 2, N_STATE), lambda b, i: (b, 0, 0)),
        ],
        out_shape=[
            jax.ShapeDtypeStruct((bsz, seq, D_MODEL), BF16),
            jax.ShapeDtypeStruct((bsz, WINDOW, KV_W), F32),
            jax.ShapeDtypeStruct((bsz, WINDOW, KV_W), F32),
            jax.ShapeDtypeStruct((bsz, 2, N_STATE), F32),
        ],
        scratch_shapes=[
            pltpu.VMEM((WINDOW, KV_W), BF16),
            pltpu.VMEM((WINDOW, KV_W), BF16),
            pltpu.VMEM((CHUNK, 2 * N_STATE), F32),
            pltpu.VMEM((SUBLANES, 2 * N_STATE), F32),
            pltpu.VMEM((2, N_STATE), F32),
        ],
        compiler_params=pltpu.CompilerParams(
            dimension_semantics=("arbitrary", "arbitrary"), vmem_limit_bytes=VMEM_LIMIT),
        name="mix_prompt",
    )(sinks, z3, rope_tab, ws, bias_t, a_rows, pw_re, pw_im, bcat, ccat, d, wglu, bglu)


def _mix_sample_kernel(sinks_ref, z_ref, rope_ref, ck_ref, cv_ref, ws_ref, bias_ref, a_ref,
                       h0r_ref, h0i_ref, bcat_ref, ccat_ref, d_ref, wglu_ref, bglu_ref,
                       mixed_ref, knew_ref, hre_ref, him_ref,
                       q_ref, h_ref, *, layer, n_batch, n_tok):
    b = pl.program_id(0)
    m = n_batch * n_tok

    @pl.when(b == 0)
    def _():
        r_i = lax.broadcasted_iota(jnp.int32, (m, m), 0)
        c_i = lax.broadcasted_iota(jnp.int32, (m, m), 1)
        keep = (r_i // n_tok == c_i // n_tok) & (c_i <= r_i)
        spread = jnp.where(lax.broadcasted_iota(jnp.int32, (m, CHUNK), 0) % n_tok
                           == lax.broadcasted_iota(jnp.int32, (m, CHUNK), 1), 1.0, 0.0).astype(BF16)
        spread_t = jnp.where(lax.broadcasted_iota(jnp.int32, (CHUNK, m), 1) % n_tok
                             == lax.broadcasted_iota(jnp.int32, (CHUNK, m), 0), 1.0, 0.0).astype(BF16)

        def tiled_w(h):
            rows = jnp.dot(spread, ws_ref[h].astype(BF16), preferred_element_type=F32)
            full = jnp.dot(rows.astype(BF16), spread_t, preferred_element_type=F32)
            return jnp.where(keep, full, 0.0).astype(BF16)

        mixed_ref[:, 0:W_A] = _chunk_mlp(tiled_w, bias_ref, z_ref)
        q_ref[...] = _rope(z_ref[:, O_Q:O_Q + W_B], rope_ref) * (HD_B ** -0.5)
        knew_ref[...] = _rope(z_ref[:, O_K:O_K + KV_W], rope_ref)
        _s5_drive(h_ref, z_ref, bcat_ref, n_batch, n_tok)
        finals = _s5_scan(h_ref, a_ref, n_batch, n_tok,
                          lambda n: (h0r_ref[:, _lane_tile(n)], h0i_ref[:, _lane_tile(n)]))
        for n in range(N_TILE):
            hre_ref[:, _lane_tile(n)] = finals[n][0]
            him_ref[:, _lane_tile(n)] = finals[n][1]
        mixed_ref[:, W_A + W_B:] = _s5_readout(
            h_ref, z_ref, ccat_ref, d_ref, wglu_ref, bglu_ref, n_batch, n_tok)

    rows = pl.ds(pl.multiple_of(b * n_tok, n_tok), n_tok)
    pad = jnp.zeros((WINDOW - n_tok, KV_W), F32)
    kcat = jnp.concatenate([ck_ref[...], knew_ref[rows, :], pad], axis=0).astype(BF16)
    vcat = jnp.concatenate([cv_ref[...], z_ref[rows, O_V:O_V + KV_W], pad], axis=0).astype(BF16)
    sink = lambda g, r: sinks_ref[layer, g * REP_B + r]
    o_b = _attention(q_ref[rows, :], kcat, vcat, sink, 0)
    mixed_ref[rows, W_A:W_A + W_B] = o_b * jax.nn.silu(z_ref[rows, O_GB:O_GB + W_B])


def _mix_sample(z2, layer, sinks, rope_tab, ck, cv, ws, bias_t, a_rows, h0r, h0i,
                bcat, ccat, d, wglu, bglu, n_batch, n_tok):
    m = n_batch * n_tok
    const = lambda *shape: pl.BlockSpec(shape, lambda b: (0,) * len(shape))
    per_layer = lambda *shape: _layer_spec(shape, layer)
    return pl.pallas_call(
        functools.partial(_mix_sample_kernel, layer=layer, n_batch=n_batch, n_tok=n_tok),
        grid=(n_batch,),
        in_specs=[
            pl.BlockSpec(memory_space=pltpu.SMEM),
            const(m, D_IN),
            const(3, m, LANES),
            pl.BlockSpec((None, None, WINDOW, KV_W), lambda b: (layer, b, 0, 0)),
            pl.BlockSpec((None, None, WINDOW, KV_W), lambda b: (layer, b, 0, 0)),
            per_layer(H_A, CHUNK, CHUNK),
            per_layer(m, H_A),
            per_layer(2, N_STATE),
            per_layer(n_batch, N_STATE),
            per_layer(n_batch, N_STATE),
            per_layer(W_C, 2 * N_STATE),
            per_layer(W_C, 2 * N_STATE),
            per_layer(1, W_C),
            per_layer(W_C, W_C),
            per_layer(1, W_C),
        ],
        out_specs=[
            const(m, D_MODEL),
            const(m, KV_W),
            const(n_batch, N_STATE),
            const(n_batch, N_STATE),
        ],
        out_shape=[
            jax.ShapeDtypeStruct((m, D_MODEL), F32),
            jax.ShapeDtypeStruct((m, KV_W), F32),
            jax.ShapeDtypeStruct((n_batch, N_STATE), F32),
            jax.ShapeDtypeStruct((n_batch, N_STATE), F32),
        ],
        scratch_shapes=[
            pltpu.VMEM((m, W_B), F32),
            pltpu.VMEM((m, 2 * N_STATE), F32),
        ],
        compiler_params=pltpu.CompilerParams(
            dimension_semantics=("arbitrary",), vmem_limit_bytes=VMEM_LIMIT),
        name="mix_sample",
    )(sinks, z2, rope_tab, ck, cv, ws, bias_t, a_rows, h0r, h0i, bcat, ccat, d, wglu, bglu)


def _rope_table(pos):
    half = ROPE_DIM // 2
    inv = ROPE_THETA ** (-jnp.arange(half, dtype=F32) * 2.0 / ROPE_DIM)
    ang = pos[:, None] * inv[None, :]
    cos, sin = jnp.cos(ang), jnp.sin(ang)
    n = pos.shape[0]
    rest = HD_B - ROPE_DIM
    c = jnp.concatenate([cos, cos, jnp.ones((n, rest), F32)], axis=1)
    s_lo = jnp.concatenate([jnp.zeros((n, half), F32), sin, jnp.zeros((n, rest), F32)], axis=1)
    s_hi = jnp.concatenate([-sin, jnp.zeros((n, half + rest), F32)], axis=1)
    tab = jnp.stack([c, s_lo, s_hi])
    return jnp.concatenate([tab] * (LANES // HD_B), axis=2)


def kernel(x_prompt, x_sample, cache_swa_k, cache_swa_v, state_ssm_re, state_ssm_im, norm_g,
           final_norm_g, w_in, w_out, chunk_w_s, chunk_b_s, attn_sinks, ssm_a_re, ssm_a_im,
           ssm_log_dt, ssm_b_re, ssm_b_im, ssm_c_re, ssm_c_im, ssm_d, glu_w, glu_b):
    bsz, seq, _ = x_prompt.shape
    n_batch, n_tok, _ = x_sample.shape
    m_s = n_batch * n_tok

    def heads_last(w):
        lead = w.shape[:-1]
        w = w.reshape(*lead, KV_B, REP_B, HD_B)
        return jnp.swapaxes(w, -3, -2).reshape(*lead, W_B)

    w_in_b = jnp.concatenate([
        w_in[:, :, :O_Q], heads_last(w_in[:, :, O_Q:O_K]), w_in[:, :, O_K:O_GB],
        heads_last(w_in[:, :, O_GB:O_UC]), w_in[:, :, O_UC:]], axis=2).astype(BF16)
    w_out_mid = jnp.swapaxes(heads_last(jnp.swapaxes(w_out[:, W_A:W_A + W_B], 1, 2)), 1, 2)
    w_out_b = jnp.concatenate(
        [w_out[:, :W_A], w_out_mid, w_out[:, W_A + W_B:]], axis=1).astype(BF16)
    wglu_b = glu_w.astype(BF16)

    ab_re, ab_im, pw_re, pw_im, bcat, ccat = _s5_prep(
        ssm_a_re, ssm_a_im, ssm_log_dt, ssm_b_re, ssm_b_im, ssm_c_re, ssm_c_im)
    a_rows = jnp.concatenate([ab_re.reshape(DEPTH, 1, N_STATE),
                              ab_im.reshape(DEPTH, 1, N_STATE)], axis=1)
    pw_re = jnp.transpose(pw_re, (0, 2, 1, 3)).reshape(DEPTH, SUB_T, N_STATE)
    pw_im = jnp.transpose(pw_im, (0, 2, 1, 3)).reshape(DEPTH, SUB_T, N_STATE)

    rope_p = _rope_table(jnp.arange(seq, dtype=F32))
    rope_s = jnp.tile(_rope_table(jnp.arange(n_tok, dtype=F32) + PAST_LEN), (1, n_batch, 1))
    bias_p = jnp.transpose(chunk_b_s, (0, 2, 1))
    bias_s = jnp.tile(bias_p[:, :n_tok], (1, n_batch, 1))
    ck = cache_swa_k.reshape(DEPTH, n_batch, WINDOW, KV_W)
    cv = cache_swa_v.reshape(DEPTH, n_batch, WINDOW, KV_W)
    h0r = state_ssm_re.reshape(DEPTH, n_batch, N_STATE)
    h0i = state_ssm_im.reshape(DEPTH, n_batch, N_STATE)
    g_all = norm_g.reshape(DEPTH, 1, D_MODEL)
    d_all = ssm_d.reshape(DEPTH, 1, W_C)
    bglu_all = glu_b.reshape(DEPTH, 1, W_C)
    fg = final_norm_g.reshape(1, D_MODEL)

    xp = x_prompt.reshape(bsz * seq, D_MODEL)
    xs = x_sample.reshape(m_s, D_MODEL)
    outs = [[] for _ in range(9)]
    for l in range(DEPTH):
        zp = _in_proj(xp, g_all, w_in_b, l)
        mixed_p, k_last, v_last, h_fin = _mix_prompt(
            zp.reshape(bsz, seq, D_IN), l, attn_sinks, rope_p, chunk_w_s, bias_p,
            a_rows, pw_re, pw_im, bcat, ccat, d_all, wglu_b, bglu_all)
        xp = _out_proj(mixed_p.reshape(bsz * seq, D_MODEL), xp, w_out_b, fg, l)

        zs = _in_proj(xs, g_all, w_in_b, l)
        mixed_s, k_new, h_re_s, h_im_s = _mix_sample(
            zs, l, attn_sinks, rope_s, ck, cv, chunk_w_s, bias_s, a_rows, h0r, h0i,
            bcat, ccat, d_all, wglu_b, bglu_all, n_batch, n_tok)
        xs = _out_proj(mixed_s, xs, w_out_b, fg, l)

        outs[0].append(k_last.reshape(bsz, WINDOW, KV_B, HD_B))
        outs[1].append(v_last.reshape(bsz, WINDOW, KV_B, HD_B))
        outs[2].append(k_new.reshape(n_batch, n_tok, KV_B, HD_B))
        outs[3].append(zs[:, O_V:O_V + KV_W].reshape(n_batch, n_tok, KV_B, HD_B))
        outs[4].append(h_fin[:, 0].reshape(bsz, G_C, P_C))
        outs[5].append(h_fin[:, 1].reshape(bsz, G_C, P_C))
        outs[6].append(h_re_s.reshape(n_batch, G_C, P_C))
        outs[7].append(h_im_s.reshape(n_batch, G_C, P_C))
        outs[8].append(zs[:, O_VA:O_VA + W_A].reshape(n_batch, n_tok, W_A))

    return (xp.reshape(bsz, seq, D_MODEL), xs.reshape(n_batch, n_tok, D_MODEL),
            *[jnp.stack(o) for o in outs])
```

```python
import functools

import jax
import jax.numpy as jnp
from jax import lax
from jax.experimental import pallas as pl
from jax.experimental.pallas import tpu as pltpu

F32 = jnp.float32
BF16 = jnp.bfloat16

D_MODEL = 2048
DEPTH = 4
PAST_LEN = 16384
CHUNK = 128
W_A = 512
H_A = 4
C_A = W_A // H_A
W_B = 1024
HD_B = 64
H_B = W_B // HD_B
KV_B = 4
REP_B = H_B // KV_B
KV_W = KV_B * HD_B
WINDOW = 128
ROPE_DIM = HD_B // 4
ROPE_THETA = 500000.0
W_C = 512
GC = 16
G_C = W_C // GC
P_C = 64
N_STATE = G_C * P_C
D_IN = 3 * W_A + 2 * W_B + 2 * KV_W + 2 * W_C
EPS = 1e-5
NEG = -1e30

O_UA, O_VA, O_GA = 0, W_A, 2 * W_A
O_Q = 3 * W_A
O_K = O_Q + W_B
O_V = O_K + KV_W
O_GB = O_V + KV_W
O_UC = O_GB + W_B
O_GC = O_UC + W_C

LANES = 128
SUBLANES = 8
N_TILE = N_STATE // LANES
N_SLAB = W_C // LANES
SLAB_GROUPS = G_C // N_SLAB
SLAB_STATES = SLAB_GROUPS * P_C
SLAB_TILES = SLAB_STATES // LANES
VMEM_LIMIT = 56 * 1024 * 1024

PROJ_ROWS = 256
PROJ_COLS = 512
W_PREP_ROWS = 256
W_PREP_COLS = 512
SCAN_VREGS = 4
SUB_T = CHUNK // SUBLANES

_NT = (((1,), (1,)), ((), ()))


def _rms(x, g):
    return x * lax.rsqrt(jnp.mean(x * x, axis=-1, keepdims=True) + EPS) * g


def _layer_spec(shape, layer, **kw):
    zeros = (0,) * len(shape)
    return pl.BlockSpec((None,) + tuple(shape), lambda *_: (layer,) + zeros, **kw)


def _in_proj_kernel(x_ref, g_ref, w_ref, z_ref):
    xn = _rms(x_ref[...], g_ref[...]).astype(BF16)
    for n0 in range(0, D_IN, PROJ_COLS):
        z_ref[:, n0:n0 + PROJ_COLS] = jnp.dot(
            xn, w_ref[:, n0:n0 + PROJ_COLS], preferred_element_type=F32)


def _in_proj(x2d, g_all, w_all, layer):
    m = x2d.shape[0]
    tm = min(PROJ_ROWS, m)
    return pl.pallas_call(
        _in_proj_kernel,
        grid=(m // tm,),
        in_specs=[
            pl.BlockSpec((tm, D_MODEL), lambda i: (i, 0)),
            _layer_spec((1, D_MODEL), layer),
            _layer_spec((D_MODEL, D_IN), layer, pipeline_mode=pl.Buffered(1)),
        ],
        out_specs=pl.BlockSpec((tm, D_IN), lambda i: (i, 0)),
        out_shape=jax.ShapeDtypeStruct((m, D_IN), F32),
        compiler_params=pltpu.CompilerParams(
            dimension_semantics=("arbitrary",), vmem_limit_bytes=VMEM_LIMIT),
        name="in_proj",
    )(x2d, g_all, w_all)


def _out_proj_kernel(m_ref, x_ref, w_ref, fg_ref, y_ref, *, final):
    y = x_ref[...] + jnp.dot(m_ref[...].astype(BF16), w_ref[...], preferred_element_type=F32)
    if final:
        y = _rms(y, fg_ref[...])
    y_ref[...] = y


def _out_proj(mixed, x2d, w_all, fg, layer):
    m = x2d.shape[0]
    tm = min(PROJ_ROWS, m)
    final = layer == DEPTH - 1
    return pl.pallas_call(
        functools.partial(_out_proj_kernel, final=final),
        grid=(m // tm,),
        in_specs=[
            pl.BlockSpec((tm, D_MODEL), lambda i: (i, 0)),
            pl.BlockSpec((tm, D_MODEL), lambda i: (i, 0)),
            _layer_spec((D_MODEL, D_MODEL), layer, pipeline_mode=pl.Buffered(1)),
            pl.BlockSpec((1, D_MODEL), lambda i: (0, 0)),
        ],
        out_specs=pl.BlockSpec((tm, D_MODEL), lambda i: (i, 0)),
        out_shape=jax.ShapeDtypeStruct((m, D_MODEL), F32),
        compiler_params=pltpu.CompilerParams(
            dimension_semantics=("arbitrary",), vmem_limit_bytes=VMEM_LIMIT),
        name="out_proj_final" if final else "out_proj",
    )(mixed, x2d, w_all, fg)


def _rep_major_source(new_head):
    r, g = divmod(new_head, KV_B)
    return g * REP_B + r


def _w_in_prep_kernel(w_ref, o_ref):
    heads_per_tile = LANES // HD_B
    lane_lo = lax.broadcasted_iota(jnp.int32, (W_PREP_ROWS, LANES), 1) < HD_B

    def half(sec, head, want_hi):
        t = w_ref[:, sec + (head // heads_per_tile) * LANES:sec + (head // heads_per_tile + 1) * LANES]
        return t if (head % heads_per_tile == 1) == want_hi else pltpu.roll(t, HD_B, 1)

    for c0, c1, permute in ((0, O_Q, False), (O_Q, O_K, True), (O_K, O_GB, False),
                            (O_GB, O_UC, True), (O_UC, D_IN, False)):
        if not permute:
            o_ref[:, c0:c1] = w_ref[:, c0:c1].astype(BF16)
            continue
        for j in range(W_B // LANES):
            lo = half(c0, _rep_major_source(heads_per_tile * j), False)
            hi = half(c0, _rep_major_source(heads_per_tile * j + 1), True)
            o_ref[:, c0 + j * LANES:c0 + (j + 1) * LANES] = jnp.where(lane_lo, lo, hi).astype(BF16)


def _w_out_prep_kernel(w_ref, o_ref):
    o_ref[0:W_A, :] = w_ref[0:W_A, :].astype(BF16)
    for new_head in range(H_B):
        src = W_A + _rep_major_source(new_head) * HD_B
        dst = W_A + new_head * HD_B
        o_ref[dst:dst + HD_B, :] = w_ref[src:src + HD_B, :].astype(BF16)
    o_ref[W_A + W_B:, :] = w_ref[W_A + W_B:, :].astype(BF16)


def _prep_weights(w_in, w_out):
    w_in_b = pl.pallas_call(
        _w_in_prep_kernel,
        grid=(DEPTH, D_MODEL // W_PREP_ROWS),
        in_specs=[pl.BlockSpec((None, W_PREP_ROWS, D_IN), lambda l, i: (l, i, 0))],
        out_specs=pl.BlockSpec((None, W_PREP_ROWS, D_IN), lambda l, i: (l, i, 0)),
        out_shape=jax.ShapeDtypeStruct((DEPTH, D_MODEL, D_IN), BF16),
        compiler_params=pltpu.CompilerParams(
            dimension_semantics=("arbitrary", "arbitrary"), vmem_limit_bytes=VMEM_LIMIT),
        name="w_in_prep",
    )(w_in)
    w_out_b = pl.pallas_call(
        _w_out_prep_kernel,
        grid=(DEPTH, D_MODEL // W_PREP_COLS),
        in_specs=[pl.BlockSpec((None, D_MODEL, W_PREP_COLS), lambda l, i: (l, 0, i))],
        out_specs=pl.BlockSpec((None, D_MODEL, W_PREP_COLS), lambda l, i: (l, 0, i)),
        out_shape=jax.ShapeDtypeStruct((DEPTH, D_MODEL, D_MODEL), BF16),
        compiler_params=pltpu.CompilerParams(
            dimension_semantics=("arbitrary", "arbitrary"), vmem_limit_bytes=VMEM_LIMIT),
        name="w_out_prep",
    )(w_out)
    return w_in_b, w_out_b


def _store_block_diag(out_ref, lane0, x3):
    per_tile = LANES // P_C
    lane_slot = lax.broadcasted_iota(jnp.int32, (GC, LANES), 1) // P_C
    for g in range(G_C):
        xg = x3[g]
        pair = jnp.concatenate([xg] * per_tile, axis=1)
        tile = jnp.where(lane_slot == g % per_tile, pair, 0.0)
        l0 = lane0 + ((g % SLAB_GROUPS) // per_tile) * LANES
        out_ref[g * GC:(g + 1) * GC, l0:l0 + LANES] = tile.astype(BF16)


def _s5_prep_kernel(are_ref, aim_ref, ldt_ref, btr_ref, bti_ref, cre_ref, cim_ref,
                    abr_ref, abi_ref, pwr_ref, pwi_ref, bcat_ref, ccat_ref):
    a_re, a_im = are_ref[...], aim_ref[...]
    dt = jnp.exp(ldt_ref[...])
    mag = jnp.exp(a_re * dt)
    ab_re = mag * jnp.cos(a_im * dt)
    ab_im = mag * jnp.sin(a_im * dt)
    nr, ni = ab_re - 1.0, ab_im
    den = a_re * a_re + a_im * a_im
    f_re = (nr * a_re + ni * a_im) / den
    f_im = (ni * a_re - nr * a_im) / den
    br, bi = btr_ref[...], bti_ref[...]
    bcat_ref[...] = jnp.zeros_like(bcat_ref)
    ccat_ref[...] = jnp.zeros_like(ccat_ref)
    _store_block_diag(bcat_ref, 0, f_re * br - f_im * bi)
    _store_block_diag(bcat_ref, SLAB_STATES, f_re * bi + f_im * br)
    _store_block_diag(ccat_ref, 0, cre_ref[...])
    _store_block_diag(ccat_ref, SLAB_STATES, -cim_ref[...])
    abr_ref[...] = ab_re
    abi_ref[...] = ab_im
    p_re, p_im = ab_re, ab_im
    for j in range(SUB_T):
        pwr_ref[:, j:j + 1, :] = p_re
        pwi_ref[:, j:j + 1, :] = p_im
        p_re, p_im = p_re * ab_re - p_im * ab_im, p_re * ab_im + p_im * ab_re


def _s5_prep(a_re, a_im, log_dt, b_re, b_im, c_re, c_im):
    a4 = lambda a: a.reshape(DEPTH, G_C, 1, P_C)
    bt = lambda b: jnp.transpose(b, (0, 1, 3, 2))
    vec = pl.BlockSpec((None, G_C, 1, P_C), lambda l: (l, 0, 0, 0))
    dts = pl.BlockSpec((None, G_C, 1, 1), lambda l: (l, 0, 0, 0))
    mat = pl.BlockSpec((None, G_C, GC, P_C), lambda l: (l, 0, 0, 0))
    pws = pl.BlockSpec((None, G_C, SUB_T, P_C), lambda l: (l, 0, 0, 0))
    dense = pl.BlockSpec((None, W_C, 2 * SLAB_STATES), lambda l: (l, 0, 0))
    vshape = jax.ShapeDtypeStruct((DEPTH, G_C, 1, P_C), F32)
    pshape = jax.ShapeDtypeStruct((DEPTH, G_C, SUB_T, P_C), F32)
    dshape = jax.ShapeDtypeStruct((DEPTH, W_C, 2 * SLAB_STATES), BF16)
    return pl.pallas_call(
        _s5_prep_kernel,
        grid=(DEPTH,),
        in_specs=[vec, vec, dts, mat, mat, mat, mat],
        out_specs=[vec, vec, pws, pws, dense, dense],
        out_shape=[vshape, vshape, pshape, pshape, dshape, dshape],
        name="s5_prep",
    )(a4(a_re), a4(a_im), log_dt.reshape(DEPTH, G_C, 1, 1), bt(b_re), bt(b_im), c_re, c_im)


def _rope(x, tab_ref):
    c, s_lo, s_hi = tab_ref[0], tab_ref[1], tab_ref[2]
    half = ROPE_DIM // 2
    tiles = []
    for j in range(x.shape[1] // LANES):
        t = x[:, j * LANES:(j + 1) * LANES]
        tiles.append(t * c + pltpu.roll(t, half, 1) * s_lo + pltpu.roll(t, LANES - half, 1) * s_hi)
    return jnp.concatenate(tiles, axis=1) if len(tiles) > 1 else tiles[0]


def _chunk_mlp(mix_w, bias_ref, z_ref):
    outs = []
    for h in range(H_A):
        v = z_ref[:, O_VA + h * C_A:O_VA + (h + 1) * C_A].astype(BF16)
        zz = jnp.dot(mix_w(h), v, preferred_element_type=F32) + bias_ref[:, h:h + 1]
        u = z_ref[:, O_UA + h * C_A:O_UA + (h + 1) * C_A]
        g = z_ref[:, O_GA + h * C_A:O_GA + (h + 1) * C_A]
        outs.append(u * zz * jax.nn.silu(g))
    return jnp.concatenate(outs, axis=1)


def _attention(q, kcat, vcat, sink, prev_off):
    t_len = q.shape[0]
    lane_grp = lax.broadcasted_iota(jnp.int32, (t_len, KV_W), 1) // HD_B
    pieces = []
    for r in range(REP_B):
        chunk = q[:, r * KV_W:(r + 1) * KV_W]
        for g in range(KV_B):
            pieces.append(jnp.where(lane_grp == g, chunk, 0.0))
    qbd = jnp.concatenate(pieces, axis=0).astype(BF16)
    s_all = lax.dot_general(qbd, kcat, _NT, preferred_element_type=F32)
    row = lax.broadcasted_iota(jnp.int32, (t_len, 2 * WINDOW), 0)
    col = lax.broadcasted_iota(jnp.int32, (t_len, 2 * WINDOW), 1)
    visible = jnp.where(col < WINDOW, col - row - prev_off, row - col + WINDOW + 1) > 0
    heads = [(r, g) for r in range(REP_B) for g in range(KV_B)]
    masked = lambda hh: jnp.where(visible, s_all[hh * t_len:(hh + 1) * t_len], NEG)
    tops = [jnp.maximum(jnp.max(masked(hh), axis=-1, keepdims=True), sink(g, r))
            for hh, (r, g) in enumerate(heads)]
    probs, scales = [], []
    for hh, (r, g) in enumerate(heads):
        p = jnp.exp(masked(hh) - tops[hh])
        den = jnp.sum(p, axis=-1, keepdims=True) + jnp.exp(sink(g, r) - tops[hh])
        probs.append(p.astype(BF16))
        scales.append(1.0 / den)
    p_all = jnp.concatenate(probs, axis=0)
    o_all = jnp.dot(p_all, vcat, preferred_element_type=F32)
    outs = []
    for r in range(REP_B):
        acc = None
        for g in range(KV_B):
            hh = r * KV_B + g
            o = o_all[hh * t_len:(hh + 1) * t_len] * scales[hh]
            acc = o if acc is None else jnp.where(lane_grp == g, o, acc)
        outs.append(acc)
    return jnp.concatenate(outs, axis=1)


def _lane_tile(n):
    return slice(n * LANES, (n + 1) * LANES)


def _seq_major_perm(n_seq, n_step, transpose):
    m = n_seq * n_step
    row = lax.broadcasted_iota(jnp.int32, (m, m), 0)
    col = lax.broadcasted_iota(jnp.int32, (m, m), 1)
    if transpose:
        hit = row == (col % n_seq) * n_step + col // n_seq
    else:
        hit = col == (row % n_seq) * n_step + row // n_seq
    return jnp.where(hit, 1.0, 0.0).astype(BF16)


def _h_lanes(n):
    base = (n // SLAB_TILES) * 2 * SLAB_STATES + (n % SLAB_TILES) * LANES
    return slice(base, base + LANES), slice(base + SLAB_STATES, base + SLAB_STATES + LANES)


def _s5_drive(h_ref, z_ref, bcat_ref, n_seq, n_step):
    u = z_ref[:, O_UC:O_UC + W_C].astype(BF16)
    u = jnp.dot(_seq_major_perm(n_seq, n_step, False), u, preferred_element_type=F32).astype(BF16)
    for s in range(N_SLAB):
        h_ref[:, s * 2 * SLAB_STATES:(s + 1) * 2 * SLAB_STATES] = jnp.dot(
            u[:, _lane_tile(s)], bcat_ref[_lane_tile(s), :], preferred_element_type=F32)


def _s5_scan(h_ref, a_ref, n_seq, n_step, init):
    group = max(1, SCAN_VREGS * SUBLANES // n_seq)
    finals = [None] * N_TILE
    for n0 in range(0, N_TILE, group):
        tiles = range(n0, n0 + group)
        state = {n: init(n) for n in tiles}
        for j in range(n_step):
            rows = slice(j * n_seq, (j + 1) * n_seq)
            for n in tiles:
                re_l, im_l = _h_lanes(n)
                ar, ai = a_ref[0:1, _lane_tile(n)], a_ref[1:2, _lane_tile(n)]
                hr, hi = state[n]
                hr, hi = (ar * hr - ai * hi + h_ref[rows, re_l],
                          ar * hi + ai * hr + h_ref[rows, im_l])
                h_ref[rows, re_l] = hr
                h_ref[rows, im_l] = hi
                state[n] = (hr, hi)
        for n in tiles:
            finals[n] = state[n]
    return finals


def _s5_readout(h_ref, z_ref, ccat_ref, d_ref, wglu_ref, bglu_ref, n_seq, n_step):
    y = jnp.concatenate([
        lax.dot_general(h_ref[:, s * 2 * SLAB_STATES:(s + 1) * 2 * SLAB_STATES].astype(BF16),
                        ccat_ref[_lane_tile(s), :], _NT, preferred_element_type=F32)
        for s in range(N_SLAB)], axis=1)
    y_hi = y.astype(BF16)
    y_lo = (y - y_hi.astype(F32)).astype(BF16)
    back = _seq_major_perm(n_seq, n_step, True)
    y = (jnp.dot(back, y_hi, preferred_element_type=F32)
         + jnp.dot(back, y_lo, preferred_element_type=F32))
    y = jax.nn.gelu(y + d_ref[...] * z_ref[:, O_UC:O_UC + W_C])
    gate = jnp.dot(y.astype(BF16), wglu_ref[...], preferred_element_type=F32) + bglu_ref[...]
    y = y * jax.nn.sigmoid(gate)
    return y * jax.nn.silu(z_ref[:, O_GC:O_GC + W_C])


def _mix_prompt_kernel(sinks_ref, z_ref, rope_ref, ws_ref, bias_ref, a_ref, pwr_ref, pwi_ref,
                       bcat_ref, ccat_ref, d_ref, wglu_ref, bglu_ref,
                       mixed_ref, klast_ref, vlast_ref, hfin_ref,
                       kprev_ref, vprev_ref, h_ref, hin_ref, carry_ref, *, layer):
    i = pl.program_id(1)

    @pl.when(i == 0)
    def _():
        kprev_ref[...] = jnp.zeros_like(kprev_ref)
        vprev_ref[...] = jnp.zeros_like(vprev_ref)
        carry_ref[...] = jnp.zeros_like(carry_ref)

    r_i = lax.broadcasted_iota(jnp.int32, (CHUNK, CHUNK), 0)
    c_i = lax.broadcasted_iota(jnp.int32, (CHUNK, CHUNK), 1)
    causal_w = lambda h: jnp.where(c_i <= r_i, ws_ref[h], 0.0).astype(BF16)
    mixed_ref[:, 0:W_A] = _chunk_mlp(causal_w, bias_ref, z_ref).astype(BF16)

    q = _rope(z_ref[:, O_Q:O_Q + W_B], rope_ref) * (HD_B ** -0.5)
    k = _rope(z_ref[:, O_K:O_K + KV_W], rope_ref)
    v = z_ref[:, O_V:O_V + KV_W]
    klast_ref[...] = k
    vlast_ref[...] = v
    kb, vb = k.astype(BF16), v.astype(BF16)
    kcat = jnp.concatenate([kprev_ref[...], kb], axis=0)
    vcat = jnp.concatenate([vprev_ref[...], vb], axis=0)
    prev_off = jnp.where(i > 0, 0, 2 * WINDOW)
    sink = lambda g, r: sinks_ref[layer, g * REP_B + r]
    o_b = _attention(q, kcat, vcat, sink, prev_off)
    mixed_ref[:, W_A:W_A + W_B] = (o_b * jax.nn.silu(z_ref[:, O_GB:O_GB + W_B])).astype(BF16)
    kprev_ref[...] = kb
    vprev_ref[...] = vb

    _s5_drive(h_ref, z_ref, bcat_ref, SUBLANES, SUB_T)
    zero = jnp.zeros((SUBLANES, LANES), F32)
    finals = _s5_scan(h_ref, a_ref, SUBLANES, SUB_T, lambda n: (zero, zero))
    for n in range(N_TILE):
        st_l = _lane_tile(n)
        re_l, im_l = _h_lanes(n)
        end_r, end_i = finals[n]
        a_t_r, a_t_i = pwr_ref[SUB_T - 1:SUB_T, st_l], pwi_ref[SUB_T - 1:SUB_T, st_l]
        cr, ci = carry_ref[0:1, st_l], carry_ref[1:2, st_l]
        for r in range(SUBLANES):
            hin_ref[r:r + 1, re_l] = cr
            hin_ref[r:r + 1, im_l] = ci
            cr, ci = (end_r[r:r + 1] + a_t_r * cr - a_t_i * ci,
                      end_i[r:r + 1] + a_t_r * ci + a_t_i * cr)
        carry_ref[0:1, st_l] = cr
        carry_ref[1:2, st_l] = ci
        in_r, in_i = hin_ref[:, re_l], hin_ref[:, im_l]
        for j in range(SUB_T):
            rows = slice(j * SUBLANES, (j + 1) * SUBLANES)
            p_r, p_i = pwr_ref[j:j + 1, st_l], pwi_ref[j:j + 1, st_l]
            h_ref[rows, re_l] = h_ref[rows, re_l] + (p_r * in_r - p_i * in_i)
            h_ref[rows, im_l] = h_ref[rows, im_l] + (p_r * in_i + p_i * in_r)
    hfin_ref[...] = carry_ref[...]
    mixed_ref[:, W_A + W_B:] = _s5_readout(
        h_ref, z_ref, ccat_ref, d_ref, wglu_ref, bglu_ref, SUBLANES, SUB_T).astype(BF16)


def _mix_prompt(z3, layer, sinks, rope_tab, ws, bias_t, a_rows, pw_re, pw_im, bcat, ccat,
                d, wglu, bglu):
    bsz, seq, _ = z3.shape
    nb = seq // CHUNK
    per_layer = lambda *shape: _layer_spec(shape, layer)
    return pl.pallas_call(
        functools.partial(_mix_prompt_kernel, layer=layer),
        grid=(bsz, nb),
        in_specs=[
            pl.BlockSpec(memory_space=pltpu.SMEM),
            pl.BlockSpec((None, CHUNK, D_IN), lambda b, i: (b, i, 0)),
            pl.BlockSpec((3, CHUNK, LANES), lambda b, i: (0, i, 0)),
            per_layer(H_A, CHUNK, CHUNK),
            per_layer(CHUNK, H_A),
            per_layer(2, N_STATE),
            per_layer(SUB_T, N_STATE),
            per_layer(SUB_T, N_STATE),
            per_layer(W_C, 2 * SLAB_STATES),
            per_layer(W_C, 2 * SLAB_STATES),
            per_layer(1, W_C),
            per_layer(W_C, W_C),
            per_layer(1, W_C),
        ],
        out_specs=[
            pl.BlockSpec((None, CHUNK, D_MODEL), lambda b, i: (b, i, 0)),
            pl.BlockSpec((None, WINDOW, KV_W), lambda b, i: (b, 0, 0)),
            pl.BlockSpec((None, WINDOW, KV_W), lambda b, i: (b, 0, 0)),
            pl.BlockSpec((None, 2, N_STATE), lambda b, i: (b, 0, 0)),
        ],
        out_shape=[
            jax.ShapeDtypeStruct((bsz, seq, D_MODEL), BF16),
            jax.ShapeDtypeStruct((bsz, WINDOW, KV_W), F32),
            jax.ShapeDtypeStruct((bsz, WINDOW, KV_W), F32),
            jax.ShapeDtypeStruct((bsz, 2, N_STATE), F32),
        ],
        scratch_shapes=[
            pltpu.VMEM((WINDOW, KV_W), BF16),
            pltpu.VMEM((WINDOW, KV_W), BF16),
            pltpu.VMEM((CHUNK, 2 * N_STATE), F32),
            pltpu.VMEM((SUBLANES, 2 * N_STATE), F32),
            pltpu.VMEM((2, N_STATE), F32),
        ],
        compiler_params=pltpu.CompilerParams(
            dimension_semantics=("arbitrary", "arbitrary"), vmem_limit_bytes=VMEM_LIMIT),
        name="mix_prompt",
    )(sinks, z3, rope_tab, ws, bias_t, a_rows, pw_re, pw_im, bcat, ccat, d, wglu, bglu)


def _mix_sample_kernel(sinks_ref, z_ref, rope_ref, ck_ref, cv_ref, ws_ref, bias_ref, a_ref,
                       h0r_ref, h0i_ref, bcat_ref, ccat_ref, d_ref, wglu_ref, bglu_ref,
                       mixed_ref, knew_ref, hre_ref, him_ref,
                       q_ref, h_ref, *, layer, n_batch, n_tok):
    b = pl.program_id(0)
    m = n_batch * n_tok

    @pl.when(b == 0)
    def _():
        r_i = lax.broadcasted_iota(jnp.int32, (m, m), 0)
        c_i = lax.broadcasted_iota(jnp.int32, (m, m), 1)
        keep = (r_i // n_tok == c_i // n_tok) & (c_i <= r_i)
        spread = jnp.where(lax.broadcasted_iota(jnp.int32, (m, CHUNK), 0) % n_tok
                           == lax.broadcasted_iota(jnp.int32, (m, CHUNK), 1), 1.0, 0.0).astype(BF16)
        spread_t = jnp.where(lax.broadcasted_iota(jnp.int32, (CHUNK, m), 1) % n_tok
                             == lax.broadcasted_iota(jnp.int32, (CHUNK, m), 0), 1.0, 0.0).astype(BF16)

        def tiled_w(h):
            rows = jnp.dot(spread, ws_ref[h].astype(BF16), preferred_element_type=F32)
            full = jnp.dot(rows.astype(BF16), spread_t, preferred_element_type=F32)
            return jnp.where(keep, full, 0.0).astype(BF16)

        mixed_ref[:, 0:W_A] = _chunk_mlp(tiled_w, bias_ref, z_ref)
        q_ref[...] = _rope(z_ref[:, O_Q:O_Q + W_B], rope_ref) * (HD_B ** -0.5)
        knew_ref[...] = _rope(z_ref[:, O_K:O_K + KV_W], rope_ref)
        _s5_drive(h_ref, z_ref, bcat_ref, n_batch, n_tok)
        finals = _s5_scan(h_ref, a_ref, n_batch, n_tok,
                          lambda n: (h0r_ref[:, _lane_tile(n)], h0i_ref[:, _lane_tile(n)]))
        for n in range(N_TILE):
            hre_ref[:, _lane_tile(n)] = finals[n][0]
            him_ref[:, _lane_tile(n)] = finals[n][1]
        mixed_ref[:, W_A + W_B:] = _s5_readout(
            h_ref, z_ref, ccat_ref, d_ref, wglu_ref, bglu_ref, n_batch, n_tok)

    rows = pl.ds(pl.multiple_of(b * n_tok, n_tok), n_tok)
    pad = jnp.zeros((WINDOW - n_tok, KV_W), F32)
    kcat = jnp.concatenate([ck_ref[...], knew_ref[rows, :], pad], axis=0).astype(BF16)
    vcat = jnp.concatenate([cv_ref[...], z_ref[rows, O_V:O_V + KV_W], pad], axis=0).astype(BF16)
    sink = lambda g, r: sinks_ref[layer, g * REP_B + r]
    o_b = _attention(q_ref[rows, :], kcat, vcat, sink, 0)
    mixed_ref[rows, W_A:W_A + W_B] = o_b * jax.nn.silu(z_ref[rows, O_GB:O_GB + W_B])


def _mix_sample(z2, layer, sinks, rope_tab, ck, cv, ws, bias_t, a_rows, h0r, h0i,
                bcat, ccat, d, wglu, bglu, n_batch, n_tok):
    m = n_batch * n_tok
    const = lambda *shape: pl.BlockSpec(shape, lambda b: (0,) * len(shape))
    per_layer = lambda *shape: _layer_spec(shape, layer)
    return pl.pallas_call(
        functools.partial(_mix_sample_kernel, layer=layer, n_batch=n_batch, n_tok=n_tok),
        grid=(n_batch,),
        in_specs=[
            pl.BlockSpec(memory_space=pltpu.SMEM),
            const(m, D_IN),
            const(3, m, LANES),
            pl.BlockSpec((None, None, WINDOW, KV_W), lambda b: (layer, b, 0, 0)),
            pl.BlockSpec((None, None, WINDOW, KV_W), lambda b: (layer, b, 0, 0)),
            per_layer(H_A, CHUNK, CHUNK),
            per_layer(m, H_A),
            per_layer(2, N_STATE),
            per_layer(n_batch, N_STATE),
            per_layer(n_batch, N_STATE),
            per_layer(W_C, 2 * SLAB_STATES),
            per_layer(W_C, 2 * SLAB_STATES),
            per_layer(1, W_C),
            per_layer(W_C, W_C),
            per_layer(1, W_C),
        ],
        out_specs=[
            const(m, D_MODEL),
            const(m, KV_W),
            const(n_batch, N_STATE),
            const(n_batch, N_STATE),
        ],
        out_shape=[
            jax.ShapeDtypeStruct((m, D_MODEL), F32),
            jax.ShapeDtypeStruct((m, KV_W), F32),
            jax.ShapeDtypeStruct((n_batch, N_STATE), F32),
            jax.ShapeDtypeStruct((n_batch, N_STATE), F32),
        ],
        scratch_shapes=[
            pltpu.VMEM((m, W_B), F32),
            pltpu.VMEM((m, 2 * N_STATE), F32),
        ],
        compiler_params=pltpu.CompilerParams(
            dimension_semantics=("arbitrary",), vmem_limit_bytes=VMEM_LIMIT),
        name="mix_sample",
    )(sinks, z2, rope_tab, ck, cv, ws, bias_t, a_rows, h0r, h0i, bcat, ccat, d, wglu, bglu)


def _rope_table(pos):
    half = ROPE_DIM // 2
    inv = ROPE_THETA ** (-jnp.arange(half, dtype=F32) * 2.0 / ROPE_DIM)
    ang = pos[:, None] * inv[None, :]
    cos, sin = jnp.cos(ang), jnp.sin(ang)
    n = pos.shape[0]
    rest = HD_B - ROPE_DIM
    c = jnp.concatenate([cos, cos, jnp.ones((n, rest), F32)], axis=1)
    s_lo = jnp.concatenate([jnp.zeros((n, half), F32), sin, jnp.zeros((n, rest), F32)], axis=1)
    s_hi = jnp.concatenate([-sin, jnp.zeros((n, half + rest), F32)], axis=1)
    tab = jnp.stack([c, s_lo, s_hi])
    return jnp.concatenate([tab] * (LANES // HD_B), axis=2)


def kernel(x_prompt, x_sample, cache_swa_k, cache_swa_v, state_ssm_re, state_ssm_im, norm_g,
           final_norm_g, w_in, w_out, chunk_w_s, chunk_b_s, attn_sinks, ssm_a_re, ssm_a_im,
           ssm_log_dt, ssm_b_re, ssm_b_im, ssm_c_re, ssm_c_im, ssm_d, glu_w, glu_b):
    bsz, seq, _ = x_prompt.shape
    n_batch, n_tok, _ = x_sample.shape
    m_s = n_batch * n_tok

    w_in_b, w_out_b = _prep_weights(w_in, w_out)
    wglu_b = glu_w.astype(BF16)

    ab_re, ab_im, pw_re, pw_im, bcat, ccat = _s5_prep(
        ssm_a_re, ssm_a_im, ssm_log_dt, ssm_b_re, ssm_b_im, ssm_c_re, ssm_c_im)
    a_rows = jnp.concatenate([ab_re.reshape(DEPTH, 1, N_STATE),
                              ab_im.reshape(DEPTH, 1, N_STATE)], axis=1)
    pw_re = jnp.transpose(pw_re, (0, 2, 1, 3)).reshape(DEPTH, SUB_T, N_STATE)
    pw_im = jnp.transpose(pw_im, (0, 2, 1, 3)).reshape(DEPTH, SUB_T, N_STATE)

    rope_p = _rope_table(jnp.arange(seq, dtype=F32))
    rope_s = jnp.tile(_rope_table(jnp.arange(n_tok, dtype=F32) + PAST_LEN), (1, n_batch, 1))
    bias_p = jnp.transpose(chunk_b_s, (0, 2, 1))
    bias_s = jnp.tile(bias_p[:, :n_tok], (1, n_batch, 1))
    ck = cache_swa_k.reshape(DEPTH, n_batch, WINDOW, KV_W)
    cv = cache_swa_v.reshape(DEPTH, n_batch, WINDOW, KV_W)
    h0r = state_ssm_re.reshape(DEPTH, n_batch, N_STATE)
    h0i = state_ssm_im.reshape(DEPTH, n_batch, N_STATE)
    g_all = norm_g.reshape(DEPTH, 1, D_MODEL)
    d_all = ssm_d.reshape(DEPTH, 1, W_C)
    bglu_all = glu_b.reshape(DEPTH, 1, W_C)
    fg = final_norm_g.reshape(1, D_MODEL)

    xp = x_prompt.reshape(bsz * seq, D_MODEL)
    xs = x_sample.reshape(m_s, D_MODEL)
    outs = [[] for _ in range(9)]
    for l in range(DEPTH):
        zp = _in_proj(xp, g_all, w_in_b, l)
        mixed_p, k_last, v_last, h_fin = _mix_prompt(
            zp.reshape(bsz, seq, D_IN), l, attn_sinks, rope_p, chunk_w_s, bias_p,
            a_rows, pw_re, pw_im, bcat, ccat, d_all, wglu_b, bglu_all)
        xp = _out_proj(mixed_p.reshape(bsz * seq, D_MODEL), xp, w_out_b, fg, l)

        zs = _in_proj(xs, g_all, w_in_b, l)
        mixed_s, k_new, h_re_s, h_im_s = _mix_sample(
            zs, l, attn_sinks, rope_s, ck, cv, chunk_w_s, bias_s, a_rows, h0r, h0i,
            bcat, ccat, d_all, wglu_b, bglu_all, n_batch, n_tok)
        xs = _out_proj(mixed_s, xs, w_out_b, fg, l)

        outs[0].append(k_last.reshape(bsz, WINDOW, KV_B, HD_B))
        outs[1].append(v_last.reshape(bsz, WINDOW, KV_B, HD_B))
        outs[2].append(k_new.reshape(n_batch, n_tok, KV_B, HD_B))
        outs[3].append(zs[:, O_V:O_V + KV_W].reshape(n_batch, n_tok, KV_B, HD_B))
        outs[4].append(h_fin[:, 0].reshape(bsz, G_C, P_C))
        outs[5].append(h_fin[:, 1].reshape(bsz, G_C, P_C))
        outs[6].append(h_re_s.reshape(n_batch, G_C, P_C))
        outs[7].append(h_im_s.reshape(n_batch, G_C, P_C))
        outs[8].append(zs[:, O_VA:O_VA + W_A].reshape(n_batch, n_tok, W_A))

    return (xp.reshape(bsz, seq, D_MODEL), xs.reshape(n_batch, n_tok, D_MODEL),
            *[jnp.stack(o) for o in outs])
```

```python
import functools

import jax
import jax.numpy as jnp
from jax import lax
from jax.experimental import pallas as pl
from jax.experimental.pallas import tpu as pltpu

F32 = jnp.float32
BF16 = jnp.bfloat16

D_MODEL = 2048
DEPTH = 4
PAST_LEN = 16384
CHUNK = 128
W_A = 512
H_A = 4
C_A = W_A // H_A
W_B = 1024
HD_B = 64
H_B = W_B // HD_B
KV_B = 4
REP_B = H_B // KV_B
KV_W = KV_B * HD_B
WINDOW = 128
ROPE_DIM = HD_B // 4
ROPE_THETA = 500000.0
W_C = 512
GC = 16
G_C = W_C // GC
P_C = 64
N_STATE = G_C * P_C
D_IN = 3 * W_A + 2 * W_B + 2 * KV_W + 2 * W_C
EPS = 1e-5
NEG = -1e30

O_UA, O_VA, O_GA = 0, W_A, 2 * W_A
O_Q = 3 * W_A
O_K = O_Q + W_B
O_V = O_K + KV_W
O_GB = O_V + KV_W
O_UC = O_GB + W_B
O_GC = O_UC + W_C

LANES = 128
SUBLANES = 8
N_TILE = N_STATE // LANES
N_SLAB = W_C // LANES
SLAB_GROUPS = G_C // N_SLAB
SLAB_STATES = SLAB_GROUPS * P_C
SLAB_TILES = SLAB_STATES // LANES
VMEM_LIMIT = 56 * 1024 * 1024

PROJ_ROWS = 256
PROJ_COLS = 512
FUSE_ROWS = 256
FUSE_EVERY = 2
W_PREP_ROWS = 256
W_PREP_COLS = 512
SCAN_VREGS = 4
SUB_T = CHUNK // SUBLANES

_NT = (((1,), (1,)), ((), ()))


def _rms(x, g):
    return x * lax.rsqrt(jnp.mean(x * x, axis=-1, keepdims=True) + EPS) * g


def _layer_spec(shape, layer, **kw):
    zeros = (0,) * len(shape)
    return pl.BlockSpec((None,) + tuple(shape), lambda *_: (layer,) + zeros, **kw)


def _in_proj_kernel(x_ref, g_ref, w_ref, z_ref):
    xn = _rms(x_ref[...], g_ref[...]).astype(BF16)
    for n0 in range(0, D_IN, PROJ_COLS):
        z_ref[:, n0:n0 + PROJ_COLS] = jnp.dot(
            xn, w_ref[:, n0:n0 + PROJ_COLS], preferred_element_type=F32)


def _in_proj(x2d, g_all, w_all, layer):
    m = x2d.shape[0]
    tm = min(PROJ_ROWS, m)
    return pl.pallas_call(
        _in_proj_kernel,
        grid=(m // tm,),
        in_specs=[
            pl.BlockSpec((tm, D_MODEL), lambda i: (i, 0)),
            _layer_spec((1, D_MODEL), layer),
            _layer_spec((D_MODEL, D_IN), layer, pipeline_mode=pl.Buffered(1)),
        ],
        out_specs=pl.BlockSpec((tm, D_IN), lambda i: (i, 0)),
        out_shape=jax.ShapeDtypeStruct((m, D_IN), F32),
        compiler_params=pltpu.CompilerParams(
            dimension_semantics=("arbitrary",), vmem_limit_bytes=VMEM_LIMIT),
        name="in_proj",
    )(x2d, g_all, w_all)


def _out_proj_kernel(m_ref, x_ref, w_ref, fg_ref, y_ref, *, final):
    y = x_ref[...] + jnp.dot(m_ref[...].astype(BF16), w_ref[...], preferred_element_type=F32)
    if final:
        y = _rms(y, fg_ref[...])
    y_ref[...] = y


def _out_proj(mixed, x2d, w_all, fg, layer):
    m = x2d.shape[0]
    tm = min(PROJ_ROWS, m)
    final = layer == DEPTH - 1
    return pl.pallas_call(
        functools.partial(_out_proj_kernel, final=final),
        grid=(m // tm,),
        in_specs=[
            pl.BlockSpec((tm, D_MODEL), lambda i: (i, 0)),
            pl.BlockSpec((tm, D_MODEL), lambda i: (i, 0)),
            _layer_spec((D_MODEL, D_MODEL), layer, pipeline_mode=pl.Buffered(1)),
            pl.BlockSpec((1, D_MODEL), lambda i: (0, 0)),
        ],
        out_specs=pl.BlockSpec((tm, D_MODEL), lambda i: (i, 0)),
        out_shape=jax.ShapeDtypeStruct((m, D_MODEL), F32),
        compiler_params=pltpu.CompilerParams(
            dimension_semantics=("arbitrary",), vmem_limit_bytes=VMEM_LIMIT),
        name="out_proj_final" if final else "out_proj",
    )(mixed, x2d, w_all, fg)


def _rep_major_source(new_head):
    r, g = divmod(new_head, KV_B)
    return g * REP_B + r


def _w_in_prep_kernel(w_ref, o_ref):
    heads_per_tile = LANES // HD_B
    lane_lo = lax.broadcasted_iota(jnp.int32, (W_PREP_ROWS, LANES), 1) < HD_B

    def half(sec, head, want_hi):
        t = w_ref[:, sec + (head // heads_per_tile) * LANES:sec + (head // heads_per_tile + 1) * LANES]
        return t if (head % heads_per_tile == 1) == want_hi else pltpu.roll(t, HD_B, 1)

    for c0, c1, permute in ((0, O_Q, False), (O_Q, O_K, True), (O_K, O_GB, False),
                            (O_GB, O_UC, True), (O_UC, D_IN, False)):
        if not permute:
            o_ref[:, c0:c1] = w_ref[:, c0:c1].astype(BF16)
            continue
        for j in range(W_B // LANES):
            lo = half(c0, _rep_major_source(heads_per_tile * j), False)
            hi = half(c0, _rep_major_source(heads_per_tile * j + 1), True)
            o_ref[:, c0 + j * LANES:c0 + (j + 1) * LANES] = jnp.where(lane_lo, lo, hi).astype(BF16)


def _w_out_prep_kernel(w_ref, o_ref):
    o_ref[0:W_A, :] = w_ref[0:W_A, :].astype(BF16)
    for new_head in range(H_B):
        src = W_A + _rep_major_source(new_head) * HD_B
        dst = W_A + new_head * HD_B
        o_ref[dst:dst + HD_B, :] = w_ref[src:src + HD_B, :].astype(BF16)
    o_ref[W_A + W_B:, :] = w_ref[W_A + W_B:, :].astype(BF16)


def _prep_weights(w_in, w_out):
    w_in_b = pl.pallas_call(
        _w_in_prep_kernel,
        grid=(DEPTH, D_MODEL // W_PREP_ROWS),
        in_specs=[pl.BlockSpec((None, W_PREP_ROWS, D_IN), lambda l, i: (l, i, 0))],
        out_specs=pl.BlockSpec((None, W_PREP_ROWS, D_IN), lambda l, i: (l, i, 0)),
        out_shape=jax.ShapeDtypeStruct((DEPTH, D_MODEL, D_IN), BF16),
        compiler_params=pltpu.CompilerParams(
            dimension_semantics=("arbitrary", "arbitrary"), vmem_limit_bytes=VMEM_LIMIT),
        name="w_in_prep",
    )(w_in)
    w_out_b = pl.pallas_call(
        _w_out_prep_kernel,
        grid=(DEPTH, D_MODEL // W_PREP_COLS),
        in_specs=[pl.BlockSpec((None, D_MODEL, W_PREP_COLS), lambda l, i: (l, 0, i))],
        out_specs=pl.BlockSpec((None, D_MODEL, W_PREP_COLS), lambda l, i: (l, 0, i)),
        out_shape=jax.ShapeDtypeStruct((DEPTH, D_MODEL, D_MODEL), BF16),
        compiler_params=pltpu.CompilerParams(
            dimension_semantics=("arbitrary", "arbitrary"), vmem_limit_bytes=VMEM_LIMIT),
        name="w_out_prep",
    )(w_out)
    return w_in_b, w_out_b


def _store_block_diag(out_ref, lane0, x3):
    per_tile = LANES // P_C
    lane_slot = lax.broadcasted_iota(jnp.int32, (GC, LANES), 1) // P_C
    for g in range(G_C):
        xg = x3[g]
        pair = jnp.concatenate([xg] * per_tile, axis=1)
        tile = jnp.where(lane_slot == g % per_tile, pair, 0.0)
        l0 = lane0 + ((g % SLAB_GROUPS) // per_tile) * LANES
        out_ref[g * GC:(g + 1) * GC, l0:l0 + LANES] = tile.astype(BF16)


def _s5_prep_kernel(are_ref, aim_ref, ldt_ref, btr_ref, bti_ref, cre_ref, cim_ref,
                    abr_ref, abi_ref, pwr_ref, pwi_ref, bcat_ref, ccat_ref):
    a_re, a_im = are_ref[...], aim_ref[...]
    dt = jnp.exp(ldt_ref[...])
    mag = jnp.exp(a_re * dt)
    ab_re = mag * jnp.cos(a_im * dt)
    ab_im = mag * jnp.sin(a_im * dt)
    nr, ni = ab_re - 1.0, ab_im
    den = a_re * a_re + a_im * a_im
    f_re = (nr * a_re + ni * a_im) / den
    f_im = (ni * a_re - nr * a_im) / den
    br, bi = btr_ref[...], bti_ref[...]
    bcat_ref[...] = jnp.zeros_like(bcat_ref)
    ccat_ref[...] = jnp.zeros_like(ccat_ref)
    _store_block_diag(bcat_ref, 0, f_re * br - f_im * bi)
    _store_block_diag(bcat_ref, SLAB_STATES, f_re * bi + f_im * br)
    _store_block_diag(ccat_ref, 0, cre_ref[...])
    _store_block_diag(ccat_ref, SLAB_STATES, -cim_ref[...])
    abr_ref[...] = ab_re
    abi_ref[...] = ab_im
    p_re, p_im = ab_re, ab_im
    for j in range(SUB_T):
        pwr_ref[:, j:j + 1, :] = p_re
        pwi_ref[:, j:j + 1, :] = p_im
        p_re, p_im = p_re * ab_re - p_im * ab_im, p_re * ab_im + p_im * ab_re


def _s5_prep(a_re, a_im, log_dt, b_re, b_im, c_re, c_im):
    a4 = lambda a: a.reshape(DEPTH, G_C, 1, P_C)
    bt = lambda b: jnp.transpose(b, (0, 1, 3, 2))
    vec = pl.BlockSpec((None, G_C, 1, P_C), lambda l: (l, 0, 0, 0))
    dts = pl.BlockSpec((None, G_C, 1, 1), lambda l: (l, 0, 0, 0))
    mat = pl.BlockSpec((None, G_C, GC, P_C), lambda l: (l, 0, 0, 0))
    pws = pl.BlockSpec((None, G_C, SUB_T, P_C), lambda l: (l, 0, 0, 0))
    dense = pl.BlockSpec((None, W_C, 2 * SLAB_STATES), lambda l: (l, 0, 0))
    vshape = jax.ShapeDtypeStruct((DEPTH, G_C, 1, P_C), F32)
    pshape = jax.ShapeDtypeStruct((DEPTH, G_C, SUB_T, P_C), F32)
    dshape = jax.ShapeDtypeStruct((DEPTH, W_C, 2 * SLAB_STATES), BF16)
    return pl.pallas_call(
        _s5_prep_kernel,
        grid=(DEPTH,),
        in_specs=[vec, vec, dts, mat, mat, mat, mat],
        out_specs=[vec, vec, pws, pws, dense, dense],
        out_shape=[vshape, vshape, pshape, pshape, dshape, dshape],
        name="s5_prep",
    )(a4(a_re), a4(a_im), log_dt.reshape(DEPTH, G_C, 1, 1), bt(b_re), bt(b_im), c_re, c_im)


def _rope(x, tab_ref):
    c, s_lo, s_hi = tab_ref[0], tab_ref[1], tab_ref[2]
    half = ROPE_DIM // 2
    tiles = []
    for j in range(x.shape[1] // LANES):
        t = x[:, j * LANES:(j + 1) * LANES]
        tiles.append(t * c + pltpu.roll(t, half, 1) * s_lo + pltpu.roll(t, LANES - half, 1) * s_hi)
    return jnp.concatenate(tiles, axis=1) if len(tiles) > 1 else tiles[0]


def _chunk_mlp(mix_w, bias_ref, z_ref):
    outs = []
    for h in range(H_A):
        v = z_ref[:, O_VA + h * C_A:O_VA + (h + 1) * C_A].astype(BF16)
        zz = jnp.dot(mix_w(h), v, preferred_element_type=F32) + bias_ref[:, h:h + 1]
        u = z_ref[:, O_UA + h * C_A:O_UA + (h + 1) * C_A]
        g = z_ref[:, O_GA + h * C_A:O_GA + (h + 1) * C_A]
        outs.append(u * zz * jax.nn.silu(g))
    return jnp.concatenate(outs, axis=1)


def _drain(steps):
    try:
        while True:
            next(steps)
    except StopIteration as done:
        return done.value


def _attention(q, kcat, vcat, sink, prev_off):
    t_len = q.shape[0]
    lane_grp = lax.broadcasted_iota(jnp.int32, (t_len, KV_W), 1) // HD_B
    pieces = []
    for r in range(REP_B):
        chunk = q[:, r * KV_W:(r + 1) * KV_W]
        for g in range(KV_B):
            pieces.append(jnp.where(lane_grp == g, chunk, 0.0))
    qbd = jnp.concatenate(pieces, axis=0).astype(BF16)
    s_all = lax.dot_general(qbd, kcat, _NT, preferred_element_type=F32)
    row = lax.broadcasted_iota(jnp.int32, (t_len, 2 * WINDOW), 0)
    col = lax.broadcasted_iota(jnp.int32, (t_len, 2 * WINDOW), 1)
    visible = jnp.where(col < WINDOW, col - row - prev_off, row - col + WINDOW + 1) > 0
    heads = [(r, g) for r in range(REP_B) for g in range(KV_B)]
    masked = lambda hh: jnp.where(visible, s_all[hh * t_len:(hh + 1) * t_len], NEG)
    yield
    tops = [jnp.maximum(jnp.max(masked(hh), axis=-1, keepdims=True), sink(g, r))
            for hh, (r, g) in enumerate(heads)]
    yield
    probs, scales = [], []
    for hh, (r, g) in enumerate(heads):
        p = jnp.exp(masked(hh) - tops[hh])
        den = jnp.sum(p, axis=-1, keepdims=True) + jnp.exp(sink(g, r) - tops[hh])
        probs.append(p.astype(BF16))
        scales.append(1.0 / den)
        if hh == len(heads) // 2 - 1:
            yield
    yield
    p_all = jnp.concatenate(probs, axis=0)
    o_all = jnp.dot(p_all, vcat, preferred_element_type=F32)
    yield
    outs = []
    for r in range(REP_B):
        acc = None
        for g in range(KV_B):
            hh = r * KV_B + g
            o = o_all[hh * t_len:(hh + 1) * t_len] * scales[hh]
            acc = o if acc is None else jnp.where(lane_grp == g, o, acc)
        outs.append(acc)
    return jnp.concatenate(outs, axis=1)


def _lane_tile(n):
    return slice(n * LANES, (n + 1) * LANES)


def _seq_major_perm(n_seq, n_step, transpose):
    m = n_seq * n_step
    row = lax.broadcasted_iota(jnp.int32, (m, m), 0)
    col = lax.broadcasted_iota(jnp.int32, (m, m), 1)
    if transpose:
        hit = row == (col % n_seq) * n_step + col // n_seq
    else:
        hit = col == (row % n_seq) * n_step + row // n_seq
    return jnp.where(hit, 1.0, 0.0).astype(BF16)


def _h_lanes(n):
    base = (n // SLAB_TILES) * 2 * SLAB_STATES + (n % SLAB_TILES) * LANES
    return slice(base, base + LANES), slice(base + SLAB_STATES, base + SLAB_STATES + LANES)


def _s5_drive(h_ref, z_ref, bcat_ref, n_seq, n_step):
    u = z_ref[:, O_UC:O_UC + W_C].astype(BF16)
    u = jnp.dot(_seq_major_perm(n_seq, n_step, False), u, preferred_element_type=F32).astype(BF16)
    for s in range(N_SLAB):
        h_ref[:, s * 2 * SLAB_STATES:(s + 1) * 2 * SLAB_STATES] = jnp.dot(
            u[:, _lane_tile(s)], bcat_ref[_lane_tile(s), :], preferred_element_type=F32)


def _s5_scan(h_ref, a_ref, n_seq, n_step, init):
    group = max(1, SCAN_VREGS * SUBLANES // n_seq)
    finals = [None] * N_TILE
    for n0 in range(0, N_TILE, group):
        tiles = range(n0, n0 + group)
        state = {n: init(n) for n in tiles}
        for j in range(n_step):
            rows = slice(j * n_seq, (j + 1) * n_seq)
            for n in tiles:
                re_l, im_l = _h_lanes(n)
                ar, ai = a_ref[0:1, _lane_tile(n)], a_ref[1:2, _lane_tile(n)]
                hr, hi = state[n]
                hr, hi = (ar * hr - ai * hi + h_ref[rows, re_l],
                          ar * hi + ai * hr + h_ref[rows, im_l])
                h_ref[rows, re_l] = hr
                h_ref[rows, im_l] = hi
                state[n] = (hr, hi)
        for n in tiles:
            finals[n] = state[n]
    return finals


def _s5_readout(h_ref, z_ref, ccat_ref, d_ref, wglu_ref, bglu_ref, n_seq, n_step):
    y = jnp.concatenate([
        lax.dot_general(h_ref[:, s * 2 * SLAB_STATES:(s + 1) * 2 * SLAB_STATES].astype(BF16),
                        ccat_ref[_lane_tile(s), :], _NT, preferred_element_type=F32)
        for s in range(N_SLAB)], axis=1)
    y_hi = y.astype(BF16)
    y_lo = (y - y_hi.astype(F32)).astype(BF16)
    back = _seq_major_perm(n_seq, n_step, True)
    y = (jnp.dot(back, y_hi, preferred_element_type=F32)
         + jnp.dot(back, y_lo, preferred_element_type=F32))
    y = jax.nn.gelu(y + d_ref[...] * z_ref[:, O_UC:O_UC + W_C])
    gate = jnp.dot(y.astype(BF16), wglu_ref[...], preferred_element_type=F32) + bglu_ref[...]
    y = y * jax.nn.sigmoid(gate)
    return y * jax.nn.silu(z_ref[:, O_GC:O_GC + W_C])


def _mix_block_steps(i, sinks_ref, z_ref, rope_ref, ws_ref, bias_ref, a_ref, pwr_ref, pwi_ref,
                     bcat_ref, ccat_ref, d_ref, wglu_ref, bglu_ref,
                     mixed_ref, klast_ref, vlast_ref, hfin_ref,
                     prev_kv, h_ref, hin_ref, carry_ref, layer):
    r_i = lax.broadcasted_iota(jnp.int32, (CHUNK, CHUNK), 0)
    c_i = lax.broadcasted_iota(jnp.int32, (CHUNK, CHUNK), 1)
    causal_w = lambda h: jnp.where(c_i <= r_i, ws_ref[h], 0.0).astype(BF16)
    mixed_ref[:, 0:W_A] = _chunk_mlp(causal_w, bias_ref, z_ref).astype(BF16)
    yield

    q = _rope(z_ref[:, O_Q:O_Q + W_B], rope_ref) * (HD_B ** -0.5)
    k = _rope(z_ref[:, O_K:O_K + KV_W], rope_ref)
    v = z_ref[:, O_V:O_V + KV_W]
    klast_ref[...] = k
    vlast_ref[...] = v
    kb, vb = k.astype(BF16), v.astype(BF16)
    kcat = jnp.concatenate([prev_kv[0], kb], axis=0)
    vcat = jnp.concatenate([prev_kv[1], vb], axis=0)
    prev_off = jnp.where(i > 0, 0, 2 * WINDOW)
    sink = lambda g, r: sinks_ref[layer, g * REP_B + r]
    yield
    o_b = yield from _attention(q, kcat, vcat, sink, prev_off)
    mixed_ref[:, W_A:W_A + W_B] = (o_b * jax.nn.silu(z_ref[:, O_GB:O_GB + W_B])).astype(BF16)
    yield

    _s5_drive(h_ref, z_ref, bcat_ref, SUBLANES, SUB_T)
    yield
    zero = jnp.zeros((SUBLANES, LANES), F32)
    finals = _s5_scan(h_ref, a_ref, SUBLANES, SUB_T, lambda n: (zero, zero))
    yield
    for n in range(N_TILE):
        st_l = _lane_tile(n)
        re_l, im_l = _h_lanes(n)
        end_r, end_i = finals[n]
        a_t_r, a_t_i = pwr_ref[SUB_T - 1:SUB_T, st_l], pwi_ref[SUB_T - 1:SUB_T, st_l]
        cr, ci = carry_ref[0:1, st_l], carry_ref[1:2, st_l]
        for r in range(SUBLANES):
            hin_ref[r:r + 1, re_l] = cr
            hin_ref[r:r + 1, im_l] = ci
            cr, ci = (end_r[r:r + 1] + a_t_r * cr - a_t_i * ci,
                      end_i[r:r + 1] + a_t_r * ci + a_t_i * cr)
        carry_ref[0:1, st_l] = cr
        carry_ref[1:2, st_l] = ci
        in_r, in_i = hin_ref[:, re_l], hin_ref[:, im_l]
        for j in range(SUB_T):
            rows = slice(j * SUBLANES, (j + 1) * SUBLANES)
            p_r, p_i = pwr_ref[j:j + 1, st_l], pwi_ref[j:j + 1, st_l]
            h_ref[rows, re_l] = h_ref[rows, re_l] + (p_r * in_r - p_i * in_i)
            h_ref[rows, im_l] = h_ref[rows, im_l] + (p_r * in_i + p_i * in_r)
        if n % (N_TILE // 2) == N_TILE // 2 - 1:
            yield
    hfin_ref[...] = carry_ref[...]
    mixed_ref[:, W_A + W_B:] = _s5_readout(
        h_ref, z_ref, ccat_ref, d_ref, wglu_ref, bglu_ref, SUBLANES, SUB_T).astype(BF16)
    return kb, vb


def _prompt_layer_kernel(sinks_ref, x_ref, g_ref, w_ref, rope_ref, ws_ref, bias_ref, a_ref,
                         pwr_ref, pwi_ref, bcat_ref, ccat_ref, d_ref, wglu_ref, bglu_ref,
                         mixed_ref, klast_ref, vlast_ref, hfin_ref,
                         z_a, z_b, kprev_ref, vprev_ref, h_ref, hin_ref, carry_ref,
                         *, layer, halves_per_seq):
    s = pl.program_id(0)
    half = jnp.maximum(s - 1, 0) % halves_per_seq

    @pl.when(s == 0)
    def _():
        z_b[...] = jnp.zeros_like(z_b)

    @pl.when(half == 0)
    def _():
        kprev_ref[...] = jnp.zeros_like(kprev_ref)
        vprev_ref[...] = jnp.zeros_like(vprev_ref)
        carry_ref[...] = jnp.zeros_like(carry_ref)

    def step(z_new, z_cur):
        xn = _rms(x_ref[...], g_ref[...]).astype(BF16)
        starts = list(range(0, D_IN, PROJ_COLS))

        def project():
            if starts:
                n0 = starts.pop(0)
                z_new[:, n0:n0 + PROJ_COLS] = jnp.dot(
                    xn, w_ref[:, n0:n0 + PROJ_COLS], preferred_element_type=F32)

        project()
        phase = 0
        prev_kv = (kprev_ref[...], vprev_ref[...])
        for sb in range(FUSE_ROWS // CHUNK):
            rows = pl.ds(sb * CHUNK, CHUNK)
            steps = _mix_block_steps(
                half * (FUSE_ROWS // CHUNK) + sb, sinks_ref, z_cur.at[rows],
                rope_ref.at[:, rows], ws_ref, bias_ref, a_ref, pwr_ref, pwi_ref, bcat_ref,
                ccat_ref, d_ref, wglu_ref, bglu_ref, mixed_ref.at[rows], klast_ref,
                vlast_ref, hfin_ref, prev_kv, h_ref, hin_ref, carry_ref, layer)
            while True:
                try:
                    next(steps)
                except StopIteration as done:
                    prev_kv = done.value
                    break
                phase += 1
                if phase % FUSE_EVERY == 0:
                    project()
        kprev_ref[...], vprev_ref[...] = prev_kv
        while starts:
            project()

    pl.when(s % 2 == 0)(lambda: step(z_a, z_b))
    pl.when(s % 2 == 1)(lambda: step(z_b, z_a))


def _prompt_layer(x2d, seq, layer, g_all, w_all, sinks, rope_tab, ws, bias_t, a_rows, pw_re, pw_im,
                  bcat, ccat, d, wglu, bglu):
    m = x2d.shape[0]
    bsz = m // seq
    n_half = m // FUSE_ROWS
    halves_per_seq = seq // FUSE_ROWS
    per_layer = lambda *shape, **kw: _layer_spec(shape, layer, **kw)
    once = dict(pipeline_mode=pl.Buffered(1))
    done = lambda s: jnp.maximum(s - 1, 0)
    return pl.pallas_call(
        functools.partial(_prompt_layer_kernel, layer=layer, halves_per_seq=halves_per_seq),
        grid=(n_half + 1,),
        in_specs=[
            pl.BlockSpec(memory_space=pltpu.SMEM),
            pl.BlockSpec((FUSE_ROWS, D_MODEL), lambda s: (jnp.minimum(s, n_half - 1), 0)),
            per_layer(1, D_MODEL),
            per_layer(D_MODEL, D_IN, **once),
            pl.BlockSpec((3, FUSE_ROWS, LANES), lambda s: (0, done(s) % halves_per_seq, 0)),
            per_layer(H_A, CHUNK, CHUNK),
            per_layer(CHUNK, H_A),
            per_layer(2, N_STATE),
            per_layer(SUB_T, N_STATE),
            per_layer(SUB_T, N_STATE),
            per_layer(W_C, 2 * SLAB_STATES, **once),
            per_layer(W_C, 2 * SLAB_STATES, **once),
            per_layer(1, W_C),
            per_layer(W_C, W_C, **once),
            per_layer(1, W_C),
        ],
        out_specs=[
            pl.BlockSpec((FUSE_ROWS, D_MODEL), lambda s: (done(s), 0)),
            pl.BlockSpec((None, WINDOW, KV_W), lambda s: (done(s) // halves_per_seq, 0, 0)),
            pl.BlockSpec((None, WINDOW, KV_W), lambda s: (done(s) // halves_per_seq, 0, 0)),
            pl.BlockSpec((None, 2, N_STATE), lambda s: (done(s) // halves_per_seq, 0, 0)),
        ],
        out_shape=[
            jax.ShapeDtypeStruct((m, D_MODEL), BF16),
            jax.ShapeDtypeStruct((bsz, WINDOW, KV_W), F32),
            jax.ShapeDtypeStruct((bsz, WINDOW, KV_W), F32),
            jax.ShapeDtypeStruct((bsz, 2, N_STATE), F32),
        ],
        scratch_shapes=[
            pltpu.VMEM((FUSE_ROWS, D_IN), F32),
            pltpu.VMEM((FUSE_ROWS, D_IN), F32),
            pltpu.VMEM((WINDOW, KV_W), BF16),
            pltpu.VMEM((WINDOW, KV_W), BF16),
            pltpu.VMEM((CHUNK, 2 * N_STATE), F32),
            pltpu.VMEM((SUBLANES, 2 * N_STATE), F32),
            pltpu.VMEM((2, N_STATE), F32),
        ],
        compiler_params=pltpu.CompilerParams(
            dimension_semantics=("arbitrary",), vmem_limit_bytes=VMEM_LIMIT),
        name="prompt_layer",
    )(sinks, x2d, g_all, w_all, rope_tab, ws, bias_t, a_rows, pw_re, pw_im, bcat, ccat, d, wglu, bglu)


def _mix_sample_kernel(sinks_ref, z_ref, rope_ref, ck_ref, cv_ref, ws_ref, bias_ref, a_ref,
                       h0r_ref, h0i_ref, bcat_ref, ccat_ref, d_ref, wglu_ref, bglu_ref,
                       mixed_ref, knew_ref, hre_ref, him_ref,
                       q_ref, h_ref, *, layer, n_batch, n_tok):
    b = pl.program_id(0)
    m = n_batch * n_tok

    @pl.when(b == 0)
    def _():
        r_i = lax.broadcasted_iota(jnp.int32, (m, m), 0)
        c_i = lax.broadcasted_iota(jnp.int32, (m, m), 1)
        keep = (r_i // n_tok == c_i // n_tok) & (c_i <= r_i)
        spread = jnp.where(lax.broadcasted_iota(jnp.int32, (m, CHUNK), 0) % n_tok
                           == lax.broadcasted_iota(jnp.int32, (m, CHUNK), 1), 1.0, 0.0).astype(BF16)
        spread_t = jnp.where(lax.broadcasted_iota(jnp.int32, (CHUNK, m), 1) % n_tok
                             == lax.broadcasted_iota(jnp.int32, (CHUNK, m), 0), 1.0, 0.0).astype(BF16)

        def tiled_w(h):
            rows = jnp.dot(spread, ws_ref[h].astype(BF16), preferred_element_type=F32)
            full = jnp.dot(rows.astype(BF16), spread_t, preferred_element_type=F32)
            return jnp.where(keep, full, 0.0).astype(BF16)

        mixed_ref[:, 0:W_A] = _chunk_mlp(tiled_w, bias_ref, z_ref)
        q_ref[...] = _rope(z_ref[:, O_Q:O_Q + W_B], rope_ref) * (HD_B ** -0.5)
        knew_ref[...] = _rope(z_ref[:, O_K:O_K + KV_W], rope_ref)
        _s5_drive(h_ref, z_ref, bcat_ref, n_batch, n_tok)
        finals = _s5_scan(h_ref, a_ref, n_batch, n_tok,
                          lambda n: (h0r_ref[:, _lane_tile(n)], h0i_ref[:, _lane_tile(n)]))
        for n in range(N_TILE):
            hre_ref[:, _lane_tile(n)] = finals[n][0]
            him_ref[:, _lane_tile(n)] = finals[n][1]
        mixed_ref[:, W_A + W_B:] = _s5_readout(
            h_ref, z_ref, ccat_ref, d_ref, wglu_ref, bglu_ref, n_batch, n_tok)

    rows = pl.ds(pl.multiple_of(b * n_tok, n_tok), n_tok)
    pad = jnp.zeros((WINDOW - n_tok, KV_W), F32)
    kcat = jnp.concatenate([ck_ref[...], knew_ref[rows, :], pad], axis=0).astype(BF16)
    vcat = jnp.concatenate([cv_ref[...], z_ref[rows, O_V:O_V + KV_W], pad], axis=0).astype(BF16)
    sink = lambda g, r: sinks_ref[layer, g * REP_B + r]
    o_b = _drain(_attention(q_ref[rows, :], kcat, vcat, sink, 0))
    mixed_ref[rows, W_A:W_A + W_B] = o_b * jax.nn.silu(z_ref[rows, O_GB:O_GB + W_B])


def _mix_sample(z2, layer, sinks, rope_tab, ck, cv, ws, bias_t, a_rows, h0r, h0i,
                bcat, ccat, d, wglu, bglu, n_batch, n_tok):
    m = n_batch * n_tok
    const = lambda *shape: pl.BlockSpec(shape, lambda b: (0,) * len(shape))
    per_layer = lambda *shape: _layer_spec(shape, layer)
    return pl.pallas_call(
        functools.partial(_mix_sample_kernel, layer=layer, n_batch=n_batch, n_tok=n_tok),
        grid=(n_batch,),
        in_specs=[
            pl.BlockSpec(memory_space=pltpu.SMEM),
            const(m, D_IN),
            const(3, m, LANES),
            pl.BlockSpec((None, None, WINDOW, KV_W), lambda b: (layer, b, 0, 0)),
            pl.BlockSpec((None, None, WINDOW, KV_W), lambda b: (layer, b, 0, 0)),
            per_layer(H_A, CHUNK, CHUNK),
            per_layer(m, H_A),
            per_layer(2, N_STATE),
            per_layer(n_batch, N_STATE),
            per_layer(n_batch, N_STATE),
            per_layer(W_C, 2 * SLAB_STATES),
            per_layer(W_C, 2 * SLAB_STATES),
            per_layer(1, W_C),
            per_layer(W_C, W_C),
            per_layer(1, W_C),
        ],
        out_specs=[
            const(m, D_MODEL),
            const(m, KV_W),
            const(n_batch, N_STATE),
            const(n_batch, N_STATE),
        ],
        out_shape=[
            jax.ShapeDtypeStruct((m, D_MODEL), F32),
            jax.ShapeDtypeStruct((m, KV_W), F32),
            jax.ShapeDtypeStruct((n_batch, N_STATE), F32),
            jax.ShapeDtypeStruct((n_batch, N_STATE), F32),
        ],
        scratch_shapes=[
            pltpu.VMEM((m, W_B), F32),
            pltpu.VMEM((m, 2 * N_STATE), F32),
        ],
        compiler_params=pltpu.CompilerParams(
            dimension_semantics=("arbitrary",), vmem_limit_bytes=VMEM_LIMIT),
        name="mix_sample",
    )(sinks, z2, rope_tab, ck, cv, ws, bias_t, a_rows, h0r, h0i, bcat, ccat, d, wglu, bglu)


def _rope_table(pos):
    half = ROPE_DIM // 2
    inv = ROPE_THETA ** (-jnp.arange(half, dtype=F32) * 2.0 / ROPE_DIM)
    ang = pos[:, None] * inv[None, :]
    cos, sin = jnp.cos(ang), jnp.sin(ang)
    n = pos.shape[0]
    rest = HD_B - ROPE_DIM
    c = jnp.concatenate([cos, cos, jnp.ones((n, rest), F32)], axis=1)
    s_lo = jnp.concatenate([jnp.zeros((n, half), F32), sin, jnp.zeros((n, rest), F32)], axis=1)
    s_hi = jnp.concatenate([-sin, jnp.zeros((n, half + rest), F32)], axis=1)
    tab = jnp.stack([c, s_lo, s_hi])
    return jnp.concatenate([tab] * (LANES // HD_B), axis=2)


def kernel(x_prompt, x_sample, cache_swa_k, cache_swa_v, state_ssm_re, state_ssm_im, norm_g,
           final_norm_g, w_in, w_out, chunk_w_s, chunk_b_s, attn_sinks, ssm_a_re, ssm_a_im,
           ssm_log_dt, ssm_b_re, ssm_b_im, ssm_c_re, ssm_c_im, ssm_d, glu_w, glu_b):
    bsz, seq, _ = x_prompt.shape
    n_batch, n_tok, _ = x_sample.shape
    m_s = n_batch * n_tok

    w_in_b, w_out_b = _prep_weights(w_in, w_out)
    wglu_b = glu_w.astype(BF16)

    ab_re, ab_im, pw_re, pw_im, bcat, ccat = _s5_prep(
        ssm_a_re, ssm_a_im, ssm_log_dt, ssm_b_re, ssm_b_im, ssm_c_re, ssm_c_im)
    a_rows = jnp.concatenate([ab_re.reshape(DEPTH, 1, N_STATE),
                              ab_im.reshape(DEPTH, 1, N_STATE)], axis=1)
    pw_re = jnp.transpose(pw_re, (0, 2, 1, 3)).reshape(DEPTH, SUB_T, N_STATE)
    pw_im = jnp.transpose(pw_im, (0, 2, 1, 3)).reshape(DEPTH, SUB_T, N_STATE)

    rope_p = _rope_table(jnp.arange(seq, dtype=F32))
    rope_s = jnp.tile(_rope_table(jnp.arange(n_tok, dtype=F32) + PAST_LEN), (1, n_batch, 1))
    bias_p = jnp.transpose(chunk_b_s, (0, 2, 1))
    bias_s = jnp.tile(bias_p[:, :n_tok], (1, n_batch, 1))
    ck = cache_swa_k.reshape(DEPTH, n_batch, WINDOW, KV_W)
    cv = cache_swa_v.reshape(DEPTH, n_batch, WINDOW, KV_W)
    h0r = state_ssm_re.reshape(DEPTH, n_batch, N_STATE)
    h0i = state_ssm_im.reshape(DEPTH, n_batch, N_STATE)
    g_all = norm_g.reshape(DEPTH, 1, D_MODEL)
    d_all = ssm_d.reshape(DEPTH, 1, W_C)
    bglu_all = glu_b.reshape(DEPTH, 1, W_C)
    fg = final_norm_g.reshape(1, D_MODEL)

    xp = x_prompt.reshape(bsz * seq, D_MODEL)
    xs = x_sample.reshape(m_s, D_MODEL)
    outs = [[] for _ in range(9)]
    for l in range(DEPTH):
        mixed_p, k_last, v_last, h_fin = _prompt_layer(
            xp, seq, l, g_all, w_in_b, attn_sinks, rope_p, chunk_w_s, bias_p,
            a_rows, pw_re, pw_im, bcat, ccat, d_all, wglu_b, bglu_all)
        xp = _out_proj(mixed_p, xp, w_out_b, fg, l)

        zs = _in_proj(xs, g_all, w_in_b, l)
        mixed_s, k_new, h_re_s, h_im_s = _mix_sample(
            zs, l, attn_sinks, rope_s, ck, cv, chunk_w_s, bias_s, a_rows, h0r, h0i,
            bcat, ccat, d_all, wglu_b, bglu_all, n_batch, n_tok)
        xs = _out_proj(mixed_s, xs, w_out_b, fg, l)

        outs[0].append(k_last.reshape(bsz, WINDOW, KV_B, HD_B))
        outs[1].append(v_last.reshape(bsz, WINDOW, KV_B, HD_B))
        outs[2].append(k_new.reshape(n_batch, n_tok, KV_B, HD_B))
        outs[3].append(zs[:, O_V:O_V + KV_W].reshape(n_batch, n_tok, KV_B, HD_B))
        outs[4].append(h_fin[:, 0].reshape(bsz, G_C, P_C))
        outs[5].append(h_fin[:, 1].reshape(bsz, G_C, P_C))
        outs[6].append(h_re_s.reshape(n_batch, G_C, P_C))
        outs[7].append(h_im_s.reshape(n_batch, G_C, P_C))
        outs[8].append(zs[:, O_VA:O_VA + W_A].reshape(n_batch, n_tok, W_A))

    return (xp.reshape(bsz, seq, D_MODEL), xs.reshape(n_batch, n_tok, D_MODEL),
            *[jnp.stack(o) for o in outs])
```

```python
import functools

import jax
import jax.numpy as jnp
from jax import lax
from jax.experimental import pallas as pl
from jax.experimental.pallas import tpu as pltpu

F32 = jnp.float32
BF16 = jnp.bfloat16

D_MODEL = 2048
DEPTH = 4
PAST_LEN = 16384
CHUNK = 128
W_A = 512
H_A = 4
C_A = W_A // H_A
W_B = 1024
HD_B = 64
H_B = W_B // HD_B
KV_B = 4
REP_B = H_B // KV_B
KV_W = KV_B * HD_B
WINDOW = 128
ROPE_DIM = HD_B // 4
ROPE_THETA = 500000.0
W_C = 512
GC = 16
G_C = W_C // GC
P_C = 64
N_STATE = G_C * P_C
D_IN = 3 * W_A + 2 * W_B + 2 * KV_W + 2 * W_C
EPS = 1e-5
NEG = -1e30

O_UA, O_VA, O_GA = 0, W_A, 2 * W_A
O_Q = 3 * W_A
O_K = O_Q + W_B
O_V = O_K + KV_W
O_GB = O_V + KV_W
O_UC = O_GB + W_B
O_GC = O_UC + W_C

LANES = 128
SUBLANES = 8
N_TILE = N_STATE // LANES
N_SLAB = W_C // LANES
SLAB_GROUPS = G_C // N_SLAB
SLAB_STATES = SLAB_GROUPS * P_C
SLAB_TILES = SLAB_STATES // LANES
VMEM_LIMIT = 56 * 1024 * 1024

PROJ_ROWS = 256
PROJ_COLS = 512
FUSE_ROWS = 256
SAMPLE_GROUP = 4
VPU_RUN_NEXT = True
W_PREP_ROWS = 256
W_PREP_COLS = 512
SCAN_VREGS = 4
SUB_T = CHUNK // SUBLANES

_NT = (((1,), (1,)), ((), ()))


def _rms(x, g):
    return x * lax.rsqrt(jnp.mean(x * x, axis=-1, keepdims=True) + EPS) * g


def _layer_spec(shape, layer, **kw):
    zeros = (0,) * len(shape)
    return pl.BlockSpec((None,) + tuple(shape), lambda *_: (layer,) + zeros, **kw)


def _out_proj_kernel(m_ref, x_ref, w_ref, fg_ref, y_ref, *, final):
    y = x_ref[...] + jnp.dot(m_ref[...].astype(BF16), w_ref[...], preferred_element_type=F32)
    if final:
        y = _rms(y, fg_ref[...])
    y_ref[...] = y


def _out_proj(mixed, x2d, w_all, fg, layer):
    m = x2d.shape[0]
    tm = min(PROJ_ROWS, m)
    final = layer == DEPTH - 1
    return pl.pallas_call(
        functools.partial(_out_proj_kernel, final=final),
        grid=(m // tm,),
        in_specs=[
            pl.BlockSpec((tm, D_MODEL), lambda i: (i, 0)),
            pl.BlockSpec((tm, D_MODEL), lambda i: (i, 0)),
            _layer_spec((D_MODEL, D_MODEL), layer, pipeline_mode=pl.Buffered(1)),
            pl.BlockSpec((1, D_MODEL), lambda i: (0, 0)),
        ],
        out_specs=pl.BlockSpec((tm, D_MODEL), lambda i: (i, 0)),
        out_shape=jax.ShapeDtypeStruct((m, D_MODEL), F32),
        compiler_params=pltpu.CompilerParams(
            dimension_semantics=("arbitrary",), vmem_limit_bytes=VMEM_LIMIT),
        name="out_proj_final" if final else "out_proj",
    )(mixed, x2d, w_all, fg)


def _rep_major_source(new_head):
    r, g = divmod(new_head, KV_B)
    return g * REP_B + r


def _w_in_prep_kernel(w_ref, o_ref):
    heads_per_tile = LANES // HD_B
    lane_lo = lax.broadcasted_iota(jnp.int32, (W_PREP_ROWS, LANES), 1) < HD_B

    def half(sec, head, want_hi):
        t = w_ref[:, sec + (head // heads_per_tile) * LANES:sec + (head // heads_per_tile + 1) * LANES]
        return t if (head % heads_per_tile == 1) == want_hi else pltpu.roll(t, HD_B, 1)

    for c0, c1, permute in ((0, O_Q, False), (O_Q, O_K, True), (O_K, O_GB, False),
                            (O_GB, O_UC, True), (O_UC, D_IN, False)):
        if not permute:
            o_ref[:, c0:c1] = w_ref[:, c0:c1].astype(BF16)
            continue
        for j in range(W_B // LANES):
            lo = half(c0, _rep_major_source(heads_per_tile * j), False)
            hi = half(c0, _rep_major_source(heads_per_tile * j + 1), True)
            o_ref[:, c0 + j * LANES:c0 + (j + 1) * LANES] = jnp.where(lane_lo, lo, hi).astype(BF16)


def _w_out_prep_kernel(w_ref, o_ref):
    o_ref[0:W_A, :] = w_ref[0:W_A, :].astype(BF16)
    for new_head in range(H_B):
        src = W_A + _rep_major_source(new_head) * HD_B
        dst = W_A + new_head * HD_B
        o_ref[dst:dst + HD_B, :] = w_ref[src:src + HD_B, :].astype(BF16)
    o_ref[W_A + W_B:, :] = w_ref[W_A + W_B:, :].astype(BF16)


def _prep_weights(w_in, w_out):
    w_in_b = pl.pallas_call(
        _w_in_prep_kernel,
        grid=(DEPTH, D_MODEL // W_PREP_ROWS),
        in_specs=[pl.BlockSpec((None, W_PREP_ROWS, D_IN), lambda l, i: (l, i, 0))],
        out_specs=pl.BlockSpec((None, W_PREP_ROWS, D_IN), lambda l, i: (l, i, 0)),
        out_shape=jax.ShapeDtypeStruct((DEPTH, D_MODEL, D_IN), BF16),
        compiler_params=pltpu.CompilerParams(
            dimension_semantics=("arbitrary", "arbitrary"), vmem_limit_bytes=VMEM_LIMIT),
        name="w_in_prep",
    )(w_in)
    w_out_b = pl.pallas_call(
        _w_out_prep_kernel,
        grid=(DEPTH, D_MODEL // W_PREP_COLS),
        in_specs=[pl.BlockSpec((None, D_MODEL, W_PREP_COLS), lambda l, i: (l, 0, i))],
        out_specs=pl.BlockSpec((None, D_MODEL, W_PREP_COLS), lambda l, i: (l, 0, i)),
        out_shape=jax.ShapeDtypeStruct((DEPTH, D_MODEL, D_MODEL), BF16),
        compiler_params=pltpu.CompilerParams(
            dimension_semantics=("arbitrary", "arbitrary"), vmem_limit_bytes=VMEM_LIMIT),
        name="w_out_prep",
    )(w_out)
    return w_in_b, w_out_b


def _store_block_diag(out_ref, lane0, x3):
    per_tile = LANES // P_C
    lane_slot = lax.broadcasted_iota(jnp.int32, (GC, LANES), 1) // P_C
    for g in range(G_C):
        xg = x3[g]
        pair = jnp.concatenate([xg] * per_tile, axis=1)
        tile = jnp.where(lane_slot == g % per_tile, pair, 0.0)
        l0 = lane0 + ((g % SLAB_GROUPS) // per_tile) * LANES
        out_ref[g * GC:(g + 1) * GC, l0:l0 + LANES] = tile.astype(BF16)


def _s5_prep_kernel(are_ref, aim_ref, ldt_ref, btr_ref, bti_ref, cre_ref, cim_ref,
                    abr_ref, abi_ref, pwr_ref, pwi_ref, bcat_ref, ccat_ref):
    a_re, a_im = are_ref[...], aim_ref[...]
    dt = jnp.exp(ldt_ref[...])
    mag = jnp.exp(a_re * dt)
    ab_re = mag * jnp.cos(a_im * dt)
    ab_im = mag * jnp.sin(a_im * dt)
    nr, ni = ab_re - 1.0, ab_im
    den = a_re * a_re + a_im * a_im
    f_re = (nr * a_re + ni * a_im) / den
    f_im = (ni * a_re - nr * a_im) / den
    br, bi = btr_ref[...], bti_ref[...]
    bcat_ref[...] = jnp.zeros_like(bcat_ref)
    ccat_ref[...] = jnp.zeros_like(ccat_ref)
    _store_block_diag(bcat_ref, 0, f_re * br - f_im * bi)
    _store_block_diag(bcat_ref, SLAB_STATES, f_re * bi + f_im * br)
    _store_block_diag(ccat_ref, 0, cre_ref[...])
    _store_block_diag(ccat_ref, SLAB_STATES, -cim_ref[...])
    abr_ref[...] = ab_re
    abi_ref[...] = ab_im
    p_re, p_im = ab_re, ab_im
    for j in range(SUB_T):
        pwr_ref[:, j:j + 1, :] = p_re
        pwi_ref[:, j:j + 1, :] = p_im
        p_re, p_im = p_re * ab_re - p_im * ab_im, p_re * ab_im + p_im * ab_re


def _s5_prep(a_re, a_im, log_dt, b_re, b_im, c_re, c_im):
    a4 = lambda a: a.reshape(DEPTH, G_C, 1, P_C)
    bt = lambda b: jnp.transpose(b, (0, 1, 3, 2))
    vec = pl.BlockSpec((None, G_C, 1, P_C), lambda l: (l, 0, 0, 0))
    dts = pl.BlockSpec((None, G_C, 1, 1), lambda l: (l, 0, 0, 0))
    mat = pl.BlockSpec((None, G_C, GC, P_C), lambda l: (l, 0, 0, 0))
    pws = pl.BlockSpec((None, G_C, SUB_T, P_C), lambda l: (l, 0, 0, 0))
    dense = pl.BlockSpec((None, W_C, 2 * SLAB_STATES), lambda l: (l, 0, 0))
    vshape = jax.ShapeDtypeStruct((DEPTH, G_C, 1, P_C), F32)
    pshape = jax.ShapeDtypeStruct((DEPTH, G_C, SUB_T, P_C), F32)
    dshape = jax.ShapeDtypeStruct((DEPTH, W_C, 2 * SLAB_STATES), BF16)
    return pl.pallas_call(
        _s5_prep_kernel,
        grid=(DEPTH,),
        in_specs=[vec, vec, dts, mat, mat, mat, mat],
        out_specs=[vec, vec, pws, pws, dense, dense],
        out_shape=[vshape, vshape, pshape, pshape, dshape, dshape],
        name="s5_prep",
    )(a4(a_re), a4(a_im), log_dt.reshape(DEPTH, G_C, 1, 1), bt(b_re), bt(b_im), c_re, c_im)


def _rope(x, tab_ref):
    c, s_lo, s_hi = tab_ref[0], tab_ref[1], tab_ref[2]
    half = ROPE_DIM // 2
    tiles = []
    for j in range(x.shape[1] // LANES):
        t = x[:, j * LANES:(j + 1) * LANES]
        tiles.append(t * c + pltpu.roll(t, half, 1) * s_lo + pltpu.roll(t, LANES - half, 1) * s_hi)
    return jnp.concatenate(tiles, axis=1) if len(tiles) > 1 else tiles[0]


def _chunk_mlp(mix_w, bias_ref, z_ref):
    outs = []
    for h in range(H_A):
        v = z_ref[:, O_VA + h * C_A:O_VA + (h + 1) * C_A].astype(BF16)
        zz = jnp.dot(mix_w(h), v, preferred_element_type=F32) + bias_ref[:, h:h + 1]
        u = z_ref[:, O_UA + h * C_A:O_UA + (h + 1) * C_A]
        g = z_ref[:, O_GA + h * C_A:O_GA + (h + 1) * C_A]
        outs.append(u * zz * jax.nn.silu(g))
    return jnp.concatenate(outs, axis=1)


def _drain(steps):
    try:
        while True:
            next(steps)
    except StopIteration as done:
        return done.value


def _attention(q, kcat, vcat, sink, prev_off):
    t_len = q.shape[0]
    lane_grp = lax.broadcasted_iota(jnp.int32, (t_len, KV_W), 1) // HD_B
    pieces = []
    for r in range(REP_B):
        chunk = q[:, r * KV_W:(r + 1) * KV_W]
        for g in range(KV_B):
            pieces.append(jnp.where(lane_grp == g, chunk, 0.0))
    qbd = jnp.concatenate(pieces, axis=0).astype(BF16)
    s_all = lax.dot_general(qbd, kcat, _NT, preferred_element_type=F32)
    row = lax.broadcasted_iota(jnp.int32, (t_len, 2 * WINDOW), 0)
    col = lax.broadcasted_iota(jnp.int32, (t_len, 2 * WINDOW), 1)
    visible = jnp.where(col < WINDOW, col - row - prev_off, row - col + WINDOW + 1) > 0
    heads = [(r, g) for r in range(REP_B) for g in range(KV_B)]
    masked = lambda hh: jnp.where(visible, s_all[hh * t_len:(hh + 1) * t_len], NEG)
    yield VPU_RUN_NEXT
    tops = [jnp.maximum(jnp.max(masked(hh), axis=-1, keepdims=True), sink(g, r))
            for hh, (r, g) in enumerate(heads)]
    yield
    probs, scales = [], []
    for hh, (r, g) in enumerate(heads):
        p = jnp.exp(masked(hh) - tops[hh])
        den = jnp.sum(p, axis=-1, keepdims=True) + jnp.exp(sink(g, r) - tops[hh])
        probs.append(p.astype(BF16))
        scales.append(1.0 / den)
        if hh == len(heads) // 2 - 1:
            yield VPU_RUN_NEXT
    yield
    p_all = jnp.concatenate(probs, axis=0)
    o_all = jnp.dot(p_all, vcat, preferred_element_type=F32)
    yield
    outs = []
    for r in range(REP_B):
        acc = None
        for g in range(KV_B):
            hh = r * KV_B + g
            o = o_all[hh * t_len:(hh + 1) * t_len] * scales[hh]
            acc = o if acc is None else jnp.where(lane_grp == g, o, acc)
        outs.append(acc)
    return jnp.concatenate(outs, axis=1)


def _lane_tile(n):
    return slice(n * LANES, (n + 1) * LANES)


def _seq_major_perm(n_seq, n_step, transpose):
    m = n_seq * n_step
    row = lax.broadcasted_iota(jnp.int32, (m, m), 0)
    col = lax.broadcasted_iota(jnp.int32, (m, m), 1)
    if transpose:
        hit = row == (col % n_seq) * n_step + col // n_seq
    else:
        hit = col == (row % n_seq) * n_step + row // n_seq
    return jnp.where(hit, 1.0, 0.0).astype(BF16)


def _h_lanes(n):
    base = (n // SLAB_TILES) * 2 * SLAB_STATES + (n % SLAB_TILES) * LANES
    return slice(base, base + LANES), slice(base + SLAB_STATES, base + SLAB_STATES + LANES)


def _s5_drive(h_ref, z_ref, bcat_ref, n_seq, n_step):
    u = z_ref[:, O_UC:O_UC + W_C].astype(BF16)
    u = jnp.dot(_seq_major_perm(n_seq, n_step, False), u, preferred_element_type=F32).astype(BF16)
    for s in range(N_SLAB):
        h_ref[:, s * 2 * SLAB_STATES:(s + 1) * 2 * SLAB_STATES] = jnp.dot(
            u[:, _lane_tile(s)], bcat_ref[_lane_tile(s), :], preferred_element_type=F32)


def _s5_scan(h_ref, a_ref, n_seq, n_step, init):
    group = max(1, SCAN_VREGS * SUBLANES // n_seq)
    finals = [None] * N_TILE
    for n0 in range(0, N_TILE, group):
        tiles = range(n0, n0 + group)
        state = {n: init(n) for n in tiles}
        for j in range(n_step):
            rows = slice(j * n_seq, (j + 1) * n_seq)
            for n in tiles:
                re_l, im_l = _h_lanes(n)
                ar, ai = a_ref[0:1, _lane_tile(n)], a_ref[1:2, _lane_tile(n)]
                hr, hi = state[n]
                hr, hi = (ar * hr - ai * hi + h_ref[rows, re_l],
                          ar * hi + ai * hr + h_ref[rows, im_l])
                h_ref[rows, re_l] = hr
                h_ref[rows, im_l] = hi
                state[n] = (hr, hi)
        for n in tiles:
            finals[n] = state[n]
    return finals


def _s5_readout(h_ref, z_ref, ccat_ref, d_ref, wglu_ref, bglu_ref, n_seq, n_step):
    y = jnp.concatenate([
        lax.dot_general(h_ref[:, s * 2 * SLAB_STATES:(s + 1) * 2 * SLAB_STATES].astype(BF16),
                        ccat_ref[_lane_tile(s), :], _NT, preferred_element_type=F32)
        for s in range(N_SLAB)], axis=1)
    y_hi = y.astype(BF16)
    y_lo = (y - y_hi.astype(F32)).astype(BF16)
    back = _seq_major_perm(n_seq, n_step, True)
    y = (jnp.dot(back, y_hi, preferred_element_type=F32)
         + jnp.dot(back, y_lo, preferred_element_type=F32))
    y = jax.nn.gelu(y + d_ref[...] * z_ref[:, O_UC:O_UC + W_C])
    gate = jnp.dot(y.astype(BF16), wglu_ref[...], preferred_element_type=F32) + bglu_ref[...]
    y = y * jax.nn.sigmoid(gate)
    return y * jax.nn.silu(z_ref[:, O_GC:O_GC + W_C])


def _mix_block_steps(i, sinks_ref, z_ref, rope_ref, ws_ref, bias_ref, a_ref, pwr_ref, pwi_ref,
                     bcat_ref, ccat_ref, d_ref, wglu_ref, bglu_ref,
                     mixed_ref, klast_ref, vlast_ref, hfin_ref,
                     prev_kv, h_ref, hin_ref, carry_ref, layer):
    r_i = lax.broadcasted_iota(jnp.int32, (CHUNK, CHUNK), 0)
    c_i = lax.broadcasted_iota(jnp.int32, (CHUNK, CHUNK), 1)
    causal_w = lambda h: jnp.where(c_i <= r_i, ws_ref[h], 0.0).astype(BF16)
    mixed_ref[:, 0:W_A] = _chunk_mlp(causal_w, bias_ref, z_ref).astype(BF16)
    yield VPU_RUN_NEXT

    q = _rope(z_ref[:, O_Q:O_Q + W_B], rope_ref) * (HD_B ** -0.5)
    k = _rope(z_ref[:, O_K:O_K + KV_W], rope_ref)
    v = z_ref[:, O_V:O_V + KV_W]
    klast_ref[...] = k
    vlast_ref[...] = v
    kb, vb = k.astype(BF16), v.astype(BF16)
    kcat = jnp.concatenate([prev_kv[0], kb], axis=0)
    vcat = jnp.concatenate([prev_kv[1], vb], axis=0)
    prev_off = jnp.where(i > 0, 0, 2 * WINDOW)
    sink = lambda g, r: sinks_ref[layer, g * REP_B + r]
    yield
    o_b = yield from _attention(q, kcat, vcat, sink, prev_off)
    mixed_ref[:, W_A:W_A + W_B] = (o_b * jax.nn.silu(z_ref[:, O_GB:O_GB + W_B])).astype(BF16)
    yield

    _s5_drive(h_ref, z_ref, bcat_ref, SUBLANES, SUB_T)
    yield VPU_RUN_NEXT
    zero = jnp.zeros((SUBLANES, LANES), F32)
    finals = _s5_scan(h_ref, a_ref, SUBLANES, SUB_T, lambda n: (zero, zero))
    yield
    for n in range(N_TILE):
        st_l = _lane_tile(n)
        re_l, im_l = _h_lanes(n)
        end_r, end_i = finals[n]
        a_t_r, a_t_i = pwr_ref[SUB_T - 1:SUB_T, st_l], pwi_ref[SUB_T - 1:SUB_T, st_l]
        cr, ci = carry_ref[0:1, st_l], carry_ref[1:2, st_l]
        for r in range(SUBLANES):
            hin_ref[r:r + 1, re_l] = cr
            hin_ref[r:r + 1, im_l] = ci
            cr, ci = (end_r[r:r + 1] + a_t_r * cr - a_t_i * ci,
                      end_i[r:r + 1] + a_t_r * ci + a_t_i * cr)
        carry_ref[0:1, st_l] = cr
        carry_ref[1:2, st_l] = ci
        in_r, in_i = hin_ref[:, re_l], hin_ref[:, im_l]
        for j in range(SUB_T):
            rows = slice(j * SUBLANES, (j + 1) * SUBLANES)
            p_r, p_i = pwr_ref[j:j + 1, st_l], pwi_ref[j:j + 1, st_l]
            h_ref[rows, re_l] = h_ref[rows, re_l] + (p_r * in_r - p_i * in_i)
            h_ref[rows, im_l] = h_ref[rows, im_l] + (p_r * in_i + p_i * in_r)
        if n == N_TILE // 2 - 1:
            yield VPU_RUN_NEXT
    yield
    hfin_ref[...] = carry_ref[...]
    mixed_ref[:, W_A + W_B:] = _s5_readout(
        h_ref, z_ref, ccat_ref, d_ref, wglu_ref, bglu_ref, SUBLANES, SUB_T).astype(BF16)
    return kb, vb


def _prompt_layer_kernel(sinks_ref, x_ref, xs_ref, g_ref, w_ref, rope_ref, ws_ref, bias_ref, a_ref,
                         pwr_ref, pwi_ref, bcat_ref, ccat_ref, d_ref, wglu_ref, bglu_ref,
                         mixed_ref, klast_ref, vlast_ref, hfin_ref, zs_ref,
                         z_a, z_b, kprev_ref, vprev_ref, h_ref, hin_ref, carry_ref, zs_sem,
                         *, layer, halves_per_seq, n_half):
    s = pl.program_id(0)
    half = jnp.maximum(s - 1, 0) % halves_per_seq

    @pl.when(s == 0)
    def _():
        z_b[...] = jnp.zeros_like(z_b)

    @pl.when(half == 0)
    def _():
        kprev_ref[...] = jnp.zeros_like(kprev_ref)
        vprev_ref[...] = jnp.zeros_like(vprev_ref)
        carry_ref[...] = jnp.zeros_like(carry_ref)

    def step(z_new, z_cur):
        x = jnp.where(s == n_half, xs_ref[...], x_ref[...])
        xn = _rms(x, g_ref[...]).astype(BF16)
        starts = list(range(0, D_IN, PROJ_COLS))

        def project():
            if starts:
                n0 = starts.pop(0)
                z_new[:, n0:n0 + PROJ_COLS] = jnp.dot(
                    xn, w_ref[:, n0:n0 + PROJ_COLS], preferred_element_type=F32)

        prev_kv = (kprev_ref[...], vprev_ref[...])
        for sb in range(FUSE_ROWS // CHUNK):
            rows = pl.ds(sb * CHUNK, CHUNK)
            steps = _mix_block_steps(
                half * (FUSE_ROWS // CHUNK) + sb, sinks_ref, z_cur.at[rows],
                rope_ref.at[:, rows], ws_ref, bias_ref, a_ref, pwr_ref, pwi_ref, bcat_ref,
                ccat_ref, d_ref, wglu_ref, bglu_ref, mixed_ref.at[rows], klast_ref,
                vlast_ref, hfin_ref, prev_kv, h_ref, hin_ref, carry_ref, layer)
            while True:
                try:
                    vpu_run_next = next(steps)
                except StopIteration as done:
                    prev_kv = done.value
                    break
                if vpu_run_next:
                    project()
        kprev_ref[...], vprev_ref[...] = prev_kv
        while starts:
            project()

    pl.when(s % 2 == 0)(lambda: step(z_a, z_b))
    pl.when(s % 2 == 1)(lambda: step(z_b, z_a))

    @pl.when(s == n_half)
    def _():
        copy = pltpu.make_async_copy((z_a, z_b)[n_half % 2], zs_ref, zs_sem)
        copy.start()
        copy.wait()


def _prompt_layer(x2d, xs2d, seq, layer, g_all, w_all, sinks, rope_tab, ws, bias_t, a_rows,
                  pw_re, pw_im, bcat, ccat, d, wglu, bglu):
    m = x2d.shape[0]
    assert xs2d.shape == (FUSE_ROWS, D_MODEL), xs2d.shape
    bsz = m // seq
    n_half = m // FUSE_ROWS
    halves_per_seq = seq // FUSE_ROWS
    per_layer = lambda *shape, **kw: _layer_spec(shape, layer, **kw)
    once = dict(pipeline_mode=pl.Buffered(1))
    done = lambda s: jnp.maximum(s - 1, 0)
    return pl.pallas_call(
        functools.partial(_prompt_layer_kernel, layer=layer, halves_per_seq=halves_per_seq,
                          n_half=n_half),
        grid=(n_half + 1,),
        in_specs=[
            pl.BlockSpec(memory_space=pltpu.SMEM),
            pl.BlockSpec((FUSE_ROWS, D_MODEL), lambda s: (jnp.minimum(s, n_half - 1), 0)),
            pl.BlockSpec((FUSE_ROWS, D_MODEL), lambda s: (0, 0), **once),
            per_layer(1, D_MODEL),
            per_layer(D_MODEL, D_IN, **once),
            pl.BlockSpec((3, FUSE_ROWS, LANES), lambda s: (0, done(s) % halves_per_seq, 0)),
            per_layer(H_A, CHUNK, CHUNK),
            per_layer(CHUNK, H_A),
            per_layer(2, N_STATE),
            per_layer(SUB_T, N_STATE),
            per_layer(SUB_T, N_STATE),
            per_layer(W_C, 2 * SLAB_STATES, **once),
            per_layer(W_C, 2 * SLAB_STATES, **once),
            per_layer(1, W_C),
            per_layer(W_C, W_C, **once),
            per_layer(1, W_C),
        ],
        out_specs=[
            pl.BlockSpec((FUSE_ROWS, D_MODEL), lambda s: (done(s), 0)),
            pl.BlockSpec((None, WINDOW, KV_W), lambda s: (done(s) // halves_per_seq, 0, 0)),
            pl.BlockSpec((None, WINDOW, KV_W), lambda s: (done(s) // halves_per_seq, 0, 0)),
            pl.BlockSpec((None, 2, N_STATE), lambda s: (done(s) // halves_per_seq, 0, 0)),
            pl.BlockSpec(memory_space=pl.ANY),
        ],
        out_shape=[
            jax.ShapeDtypeStruct((m, D_MODEL), BF16),
            jax.ShapeDtypeStruct((bsz, WINDOW, KV_W), F32),
            jax.ShapeDtypeStruct((bsz, WINDOW, KV_W), F32),
            jax.ShapeDtypeStruct((bsz, 2, N_STATE), F32),
            jax.ShapeDtypeStruct((FUSE_ROWS, D_IN), F32),
        ],
        scratch_shapes=[
            pltpu.VMEM((FUSE_ROWS, D_IN), F32),
            pltpu.VMEM((FUSE_ROWS, D_IN), F32),
            pltpu.VMEM((WINDOW, KV_W), BF16),
            pltpu.VMEM((WINDOW, KV_W), BF16),
            pltpu.VMEM((CHUNK, 2 * N_STATE), F32),
            pltpu.VMEM((SUBLANES, 2 * N_STATE), F32),
            pltpu.VMEM((2, N_STATE), F32),
            pltpu.SemaphoreType.DMA(()),
        ],
        compiler_params=pltpu.CompilerParams(
            dimension_semantics=("arbitrary",), vmem_limit_bytes=VMEM_LIMIT),
        name="prompt_layer",
    )(sinks, x2d, xs2d, g_all, w_all, rope_tab, ws, bias_t, a_rows, pw_re, pw_im, bcat, ccat, d,
      wglu, bglu)


def _mix_sample_kernel(sinks_ref, z_ref, rope_ref, ck_ref, cv_ref, ws_ref, bias_ref, a_ref,
                       h0r_ref, h0i_ref, bcat_ref, ccat_ref, d_ref, wglu_ref, bglu_ref,
                       mixed_ref, knew_ref, hre_ref, him_ref,
                       q_ref, h_ref, *, layer, n_batch, n_tok):
    b = pl.program_id(0)
    m = n_batch * n_tok

    @pl.when(b == 0)
    def _():
        r_i = lax.broadcasted_iota(jnp.int32, (m, m), 0)
        c_i = lax.broadcasted_iota(jnp.int32, (m, m), 1)
        keep = (r_i // n_tok == c_i // n_tok) & (c_i <= r_i)
        spread = jnp.where(lax.broadcasted_iota(jnp.int32, (m, CHUNK), 0) % n_tok
                           == lax.broadcasted_iota(jnp.int32, (m, CHUNK), 1), 1.0, 0.0).astype(BF16)
        spread_t = jnp.where(lax.broadcasted_iota(jnp.int32, (CHUNK, m), 1) % n_tok
                             == lax.broadcasted_iota(jnp.int32, (CHUNK, m), 0), 1.0, 0.0).astype(BF16)

        def tiled_w(h):
            rows = jnp.dot(spread, ws_ref[h].astype(BF16), preferred_element_type=F32)
            full = jnp.dot(rows.astype(BF16), spread_t, preferred_element_type=F32)
            return jnp.where(keep, full, 0.0).astype(BF16)

        mixed_ref[:, 0:W_A] = _chunk_mlp(tiled_w, bias_ref, z_ref)
        q_ref[...] = _rope(z_ref[:, O_Q:O_Q + W_B], rope_ref) * (HD_B ** -0.5)
        knew_ref[...] = _rope(z_ref[:, O_K:O_K + KV_W], rope_ref)
        _s5_drive(h_ref, z_ref, bcat_ref, n_batch, n_tok)
        finals = _s5_scan(h_ref, a_ref, n_batch, n_tok,
                          lambda n: (h0r_ref[:, _lane_tile(n)], h0i_ref[:, _lane_tile(n)]))
        for n in range(N_TILE):
            hre_ref[:, _lane_tile(n)] = finals[n][0]
            him_ref[:, _lane_tile(n)] = finals[n][1]
        mixed_ref[:, W_A + W_B:] = _s5_readout(
            h_ref, z_ref, ccat_ref, d_ref, wglu_ref, bglu_ref, n_batch, n_tok)

    pad = jnp.zeros((WINDOW - n_tok, KV_W), F32)
    sink = lambda g, r: sinks_ref[layer, g * REP_B + r]
    row_sets, running = [], []
    for j in range(SAMPLE_GROUP):
        rows = pl.ds(pl.multiple_of((b * SAMPLE_GROUP + j) * n_tok, n_tok), n_tok)
        kcat = jnp.concatenate([ck_ref[j], knew_ref[rows, :], pad], axis=0).astype(BF16)
        vcat = jnp.concatenate([cv_ref[j], z_ref[rows, O_V:O_V + KV_W], pad], axis=0).astype(BF16)
        row_sets.append(rows)
        running.append(_attention(q_ref[rows, :], kcat, vcat, sink, 0))
    results = [None] * SAMPLE_GROUP
    while any(r is None for r in results):
        for j, steps in enumerate(running):
            if results[j] is None:
                try:
                    next(steps)
                except StopIteration as done:
                    results[j] = done.value
    for rows, o_b in zip(row_sets, results):
        mixed_ref[rows, W_A:W_A + W_B] = o_b * jax.nn.silu(z_ref[rows, O_GB:O_GB + W_B])


def _mix_sample(z2, layer, sinks, rope_tab, ck, cv, ws, bias_t, a_rows, h0r, h0i,
                bcat, ccat, d, wglu, bglu, n_batch, n_tok):
    m = n_batch * n_tok
    const = lambda *shape: pl.BlockSpec(shape, lambda b: (0,) * len(shape))
    per_layer = lambda *shape: _layer_spec(shape, layer)
    return pl.pallas_call(
        functools.partial(_mix_sample_kernel, layer=layer, n_batch=n_batch, n_tok=n_tok),
        grid=(n_batch // SAMPLE_GROUP,),
        in_specs=[
            pl.BlockSpec(memory_space=pltpu.SMEM),
            const(m, D_IN),
            const(3, m, LANES),
            pl.BlockSpec((None, SAMPLE_GROUP, WINDOW, KV_W), lambda b: (layer, b, 0, 0)),
            pl.BlockSpec((None, SAMPLE_GROUP, WINDOW, KV_W), lambda b: (layer, b, 0, 0)),
            per_layer(H_A, CHUNK, CHUNK),
            per_layer(m, H_A),
            per_layer(2, N_STATE),
            per_layer(n_batch, N_STATE),
            per_layer(n_batch, N_STATE),
            per_layer(W_C, 2 * SLAB_STATES),
            per_layer(W_C, 2 * SLAB_STATES),
            per_layer(1, W_C),
            per_layer(W_C, W_C),
            per_layer(1, W_C),
        ],
        out_specs=[
            const(m, D_MODEL),
            const(m, KV_W),
            const(n_batch, N_STATE),
            const(n_batch, N_STATE),
        ],
        out_shape=[
            jax.ShapeDtypeStruct((m, D_MODEL), F32),
            jax.ShapeDtypeStruct((m, KV_W), F32),
            jax.ShapeDtypeStruct((n_batch, N_STATE), F32),
            jax.ShapeDtypeStruct((n_batch, N_STATE), F32),
        ],
        scratch_shapes=[
            pltpu.VMEM((m, W_B), F32),
            pltpu.VMEM((m, 2 * N_STATE), F32),
        ],
        compiler_params=pltpu.CompilerParams(
            dimension_semantics=("arbitrary",), vmem_limit_bytes=VMEM_LIMIT),
        name="mix_sample",
    )(sinks, z2, rope_tab, ck, cv, ws, bias_t, a_rows, h0r, h0i, bcat, ccat, d, wglu, bglu)


def _rope_table(pos):
    half = ROPE_DIM // 2
    inv = ROPE_THETA ** (-jnp.arange(half, dtype=F32) * 2.0 / ROPE_DIM)
    ang = pos[:, None] * inv[None, :]
    cos, sin = jnp.cos(ang), jnp.sin(ang)
    n = pos.shape[0]
    rest = HD_B - ROPE_DIM
    c = jnp.concatenate([cos, cos, jnp.ones((n, rest), F32)], axis=1)
    s_lo = jnp.concatenate([jnp.zeros((n, half), F32), sin, jnp.zeros((n, rest), F32)], axis=1)
    s_hi = jnp.concatenate([-sin, jnp.zeros((n, half + rest), F32)], axis=1)
    tab = jnp.stack([c, s_lo, s_hi])
    return jnp.concatenate([tab] * (LANES // HD_B), axis=2)


def kernel(x_prompt, x_sample, cache_swa_k, cache_swa_v, state_ssm_re, state_ssm_im, norm_g,
           final_norm_g, w_in, w_out, chunk_w_s, chunk_b_s, attn_sinks, ssm_a_re, ssm_a_im,
           ssm_log_dt, ssm_b_re, ssm_b_im, ssm_c_re, ssm_c_im, ssm_d, glu_w, glu_b):
    bsz, seq, _ = x_prompt.shape
    n_batch, n_tok, _ = x_sample.shape
    m_s = n_batch * n_tok

    w_in_b, w_out_b = _prep_weights(w_in, w_out)
    wglu_b = glu_w.astype(BF16)

    ab_re, ab_im, pw_re, pw_im, bcat, ccat = _s5_prep(
        ssm_a_re, ssm_a_im, ssm_log_dt, ssm_b_re, ssm_b_im, ssm_c_re, ssm_c_im)
    a_rows = jnp.concatenate([ab_re.reshape(DEPTH, 1, N_STATE),
                              ab_im.reshape(DEPTH, 1, N_STATE)], axis=1)
    pw_re = jnp.transpose(pw_re, (0, 2, 1, 3)).reshape(DEPTH, SUB_T, N_STATE)
    pw_im = jnp.transpose(pw_im, (0, 2, 1, 3)).reshape(DEPTH, SUB_T, N_STATE)

    rope_p = _rope_table(jnp.arange(seq, dtype=F32))
    rope_s = jnp.tile(_rope_table(jnp.arange(n_tok, dtype=F32) + PAST_LEN), (1, n_batch, 1))
    bias_p = jnp.transpose(chunk_b_s, (0, 2, 1))
    bias_s = jnp.tile(bias_p[:, :n_tok], (1, n_batch, 1))
    ck = cache_swa_k.reshape(DEPTH, n_batch, WINDOW, KV_W)
    cv = cache_swa_v.reshape(DEPTH, n_batch, WINDOW, KV_W)
    h0r = state_ssm_re.reshape(DEPTH, n_batch, N_STATE)
    h0i = state_ssm_im.reshape(DEPTH, n_batch, N_STATE)
    g_all = norm_g.reshape(DEPTH, 1, D_MODEL)
    d_all = ssm_d.reshape(DEPTH, 1, W_C)
    bglu_all = glu_b.reshape(DEPTH, 1, W_C)
    fg = final_norm_g.reshape(1, D_MODEL)

    xp = x_prompt.reshape(bsz * seq, D_MODEL)
    xs = x_sample.reshape(m_s, D_MODEL)
    outs = [[] for _ in range(9)]
    for l in range(DEPTH):
        mixed_p, k_last, v_last, h_fin, zs = _prompt_layer(
            xp, xs, seq, l, g_all, w_in_b, attn_sinks, rope_p, chunk_w_s, bias_p,
            a_rows, pw_re, pw_im, bcat, ccat, d_all, wglu_b, bglu_all)
        xp = _out_proj(mixed_p, xp, w_out_b, fg, l)

        mixed_s, k_new, h_re_s, h_im_s = _mix_sample(
            zs, l, attn_sinks, rope_s, ck, cv, chunk_w_s, bias_s, a_rows, h0r, h0i,
            bcat, ccat, d_all, wglu_b, bglu_all, n_batch, n_tok)
        xs = _out_proj(mixed_s, xs, w_out_b, fg, l)

        outs[0].append(k_last.reshape(bsz, WINDOW, KV_B, HD_B))
        outs[1].append(v_last.reshape(bsz, WINDOW, KV_B, HD_B))
        outs[2].append(k_new.reshape(n_batch, n_tok, KV_B, HD_B))
        outs[3].append(zs[:, O_V:O_V + KV_W].reshape(n_batch, n_tok, KV_B, HD_B))
        outs[4].append(h_fin[:, 0].reshape(bsz, G_C, P_C))
        outs[5].append(h_fin[:, 1].reshape(bsz, G_C, P_C))
        outs[6].append(h_re_s.reshape(n_batch, G_C, P_C))
        outs[7].append(h_im_s.reshape(n_batch, G_C, P_C))
        outs[8].append(zs[:, O_VA:O_VA + W_A].reshape(n_batch, n_tok, W_A))

    return (xp.reshape(bsz, seq, D_MODEL), xs.reshape(n_batch, n_tok, D_MODEL),
            *[jnp.stack(o) for o in outs])
```

```python
import functools

import jax
import jax.numpy as jnp
from jax import lax
from jax.experimental import pallas as pl
from jax.experimental.pallas import tpu as pltpu

F32 = jnp.float32
BF16 = jnp.bfloat16

D_MODEL = 2048
DEPTH = 4
PAST_LEN = 16384
CHUNK = 128
W_A = 512
H_A = 4
C_A = W_A // H_A
W_B = 1024
HD_B = 64
H_B = W_B // HD_B
KV_B = 4
REP_B = H_B // KV_B
KV_W = KV_B * HD_B
WINDOW = 128
ROPE_DIM = HD_B // 4
ROPE_THETA = 500000.0
W_C = 512
GC = 16
G_C = W_C // GC
P_C = 64
N_STATE = G_C * P_C
D_IN = 3 * W_A + 2 * W_B + 2 * KV_W + 2 * W_C
EPS = 1e-5
NEG = -1e30

O_UA, O_VA, O_GA = 0, W_A, 2 * W_A
O_Q = 3 * W_A
O_K = O_Q + W_B
O_V = O_K + KV_W
O_GB = O_V + KV_W
O_UC = O_GB + W_B
O_GC = O_UC + W_C

LANES = 128
SUBLANES = 8
N_TILE = N_STATE // LANES
N_SLAB = W_C // LANES
SLAB_GROUPS = G_C // N_SLAB
SLAB_STATES = SLAB_GROUPS * P_C
SLAB_TILES = SLAB_STATES // LANES
VMEM_LIMIT = 60 * 1024 * 1024

PROJ_ROWS = 256
PROJ_COLS = 512
FUSE_ROWS = 256
SAMPLE_GROUP = 4
VPU_RUN_NEXT = True
W_PREP_ROWS = 256
W_PREP_COLS = 512
SCAN_VREGS = 4
SUB_T = CHUNK // SUBLANES

_NT = (((1,), (1,)), ((), ()))


def _rms(x, g):
    return x * lax.rsqrt(jnp.mean(x * x, axis=-1, keepdims=True) + EPS) * g


def _layer_spec(shape, layer, **kw):
    zeros = (0,) * len(shape)
    return pl.BlockSpec((None,) + tuple(shape), lambda *_: (layer,) + zeros, **kw)


def _out_proj_kernel(m_ref, x_ref, w_ref, fg_ref, *rest, final, prep_next):
    if prep_next:
        wn_ref, y_ref, wnb_ref = rest
        wnb_ref[...] = wn_ref[...].astype(BF16)
    else:
        y_ref, = rest
    y = x_ref[...] + jnp.dot(m_ref[...].astype(BF16), w_ref[...], preferred_element_type=F32)
    if final:
        y = _rms(y, fg_ref[...])
    y_ref[...] = y


def _w_out_source_block(i):
    first, last = W_A // HD_B, (W_A + W_B) // HD_B
    head = i - first
    src = first + (head % KV_B) * REP_B + head // KV_B
    return jnp.where((i >= first) & (i < last), src, i)


def _out_proj(mixed, x2d, w_b, fg, layer, w_out_f32=None):
    m = x2d.shape[0]
    tm = min(PROJ_ROWS, m)
    final = layer == DEPTH - 1
    prep_next = w_out_f32 is not None
    in_specs = [
        pl.BlockSpec((tm, D_MODEL), lambda i: (i, 0)),
        pl.BlockSpec((tm, D_MODEL), lambda i: (i, 0)),
        pl.BlockSpec((D_MODEL, D_MODEL), lambda i: (0, 0), pipeline_mode=pl.Buffered(1)),
        pl.BlockSpec((1, D_MODEL), lambda i: (0, 0)),
    ]
    out_specs = [pl.BlockSpec((tm, D_MODEL), lambda i: (i, 0))]
    out_shape = [jax.ShapeDtypeStruct((m, D_MODEL), F32)]
    args = [mixed, x2d, w_b, fg]
    if prep_next:
        assert m // tm == D_MODEL // HD_B, (m, tm)
        in_specs.append(pl.BlockSpec((None, HD_B, D_MODEL),
                                     lambda i: (layer + 1, _w_out_source_block(i), 0)))
        out_specs.append(pl.BlockSpec((HD_B, D_MODEL), lambda i: (i, 0)))
        out_shape.append(jax.ShapeDtypeStruct((D_MODEL, D_MODEL), BF16))
        args.append(w_out_f32)
    out = pl.pallas_call(
        functools.partial(_out_proj_kernel, final=final, prep_next=prep_next),
        grid=(m // tm,),
        in_specs=in_specs,
        out_specs=out_specs,
        out_shape=out_shape,
        compiler_params=pltpu.CompilerParams(
            dimension_semantics=("arbitrary",), vmem_limit_bytes=VMEM_LIMIT),
        name="out_proj_final" if final else "out_proj",
    )(*args)
    return out if prep_next else out[0]


def _rep_major_source(new_head):
    r, g = divmod(new_head, KV_B)
    return g * REP_B + r


def _w_in_prep_kernel(w_ref, o_ref):
    heads_per_tile = LANES // HD_B
    lane_lo = lax.broadcasted_iota(jnp.int32, (w_ref.shape[0], LANES), 1) < HD_B

    def half(sec, head, want_hi):
        t = w_ref[:, sec + (head // heads_per_tile) * LANES:sec + (head // heads_per_tile + 1) * LANES]
        return t if (head % heads_per_tile == 1) == want_hi else pltpu.roll(t, HD_B, 1)

    for c0, c1, permute in ((0, O_Q, False), (O_Q, O_K, True), (O_K, O_GB, False),
                            (O_GB, O_UC, True), (O_UC, D_IN, False)):
        if not permute:
            o_ref[:, c0:c1] = w_ref[:, c0:c1].astype(BF16)
            continue
        for j in range(W_B // LANES):
            lo = half(c0, _rep_major_source(heads_per_tile * j), False)
            hi = half(c0, _rep_major_source(heads_per_tile * j + 1), True)
            o_ref[:, c0 + j * LANES:c0 + (j + 1) * LANES] = jnp.where(lane_lo, lo, hi).astype(BF16)


def _w_out_prep_kernel(w_ref, o_ref):
    o_ref[0:W_A, :] = w_ref[0:W_A, :].astype(BF16)
    for new_head in range(H_B):
        src = W_A + _rep_major_source(new_head) * HD_B
        dst = W_A + new_head * HD_B
        o_ref[dst:dst + HD_B, :] = w_ref[src:src + HD_B, :].astype(BF16)
    o_ref[W_A + W_B:, :] = w_ref[W_A + W_B:, :].astype(BF16)


def _prep_first_weights(w_in, w_out):
    w_in_b = pl.pallas_call(
        _w_in_prep_kernel,
        grid=(D_MODEL // W_PREP_ROWS,),
        in_specs=[pl.BlockSpec((None, W_PREP_ROWS, D_IN), lambda i: (0, i, 0))],
        out_specs=pl.BlockSpec((W_PREP_ROWS, D_IN), lambda i: (i, 0)),
        out_shape=jax.ShapeDtypeStruct((D_MODEL, D_IN), BF16),
        compiler_params=pltpu.CompilerParams(
            dimension_semantics=("arbitrary",), vmem_limit_bytes=VMEM_LIMIT),
        name="w_in_prep",
    )(w_in)
    w_out_b = pl.pallas_call(
        _w_out_prep_kernel,
        grid=(D_MODEL // W_PREP_COLS,),
        in_specs=[pl.BlockSpec((None, D_MODEL, W_PREP_COLS), lambda i: (0, 0, i))],
        out_specs=pl.BlockSpec((D_MODEL, W_PREP_COLS), lambda i: (0, i)),
        out_shape=jax.ShapeDtypeStruct((D_MODEL, D_MODEL), BF16),
        compiler_params=pltpu.CompilerParams(
            dimension_semantics=("arbitrary",), vmem_limit_bytes=VMEM_LIMIT),
        name="w_out_prep",
    )(w_out)
    return w_in_b, w_out_b


def _store_block_diag(out_ref, lane0, x3):
    per_tile = LANES // P_C
    lane_slot = lax.broadcasted_iota(jnp.int32, (GC, LANES), 1) // P_C
    for g in range(G_C):
        xg = x3[g]
        pair = jnp.concatenate([xg] * per_tile, axis=1)
        tile = jnp.where(lane_slot == g % per_tile, pair, 0.0)
        l0 = lane0 + ((g % SLAB_GROUPS) // per_tile) * LANES
        out_ref[g * GC:(g + 1) * GC, l0:l0 + LANES] = tile.astype(BF16)


def _s5_prep_kernel(are_ref, aim_ref, ldt_ref, btr_ref, bti_ref, cre_ref, cim_ref,
                    abr_ref, abi_ref, pwr_ref, pwi_ref, bcat_ref, ccat_ref):
    a_re, a_im = are_ref[...], aim_ref[...]
    dt = jnp.exp(ldt_ref[...])
    mag = jnp.exp(a_re * dt)
    ab_re = mag * jnp.cos(a_im * dt)
    ab_im = mag * jnp.sin(a_im * dt)
    nr, ni = ab_re - 1.0, ab_im
    den = a_re * a_re + a_im * a_im
    f_re = (nr * a_re + ni * a_im) / den
    f_im = (ni * a_re - nr * a_im) / den
    br, bi = btr_ref[...], bti_ref[...]
    bcat_ref[...] = jnp.zeros_like(bcat_ref)
    ccat_ref[...] = jnp.zeros_like(ccat_ref)
    _store_block_diag(bcat_ref, 0, f_re * br - f_im * bi)
    _store_block_diag(bcat_ref, SLAB_STATES, f_re * bi + f_im * br)
    _store_block_diag(ccat_ref, 0, cre_ref[...])
    _store_block_diag(ccat_ref, SLAB_STATES, -cim_ref[...])
    abr_ref[...] = ab_re
    abi_ref[...] = ab_im
    p_re, p_im = ab_re, ab_im
    for j in range(SUB_T):
        pwr_ref[:, j:j + 1, :] = p_re
        pwi_ref[:, j:j + 1, :] = p_im
        p_re, p_im = p_re * ab_re - p_im * ab_im, p_re * ab_im + p_im * ab_re


def _s5_prep(a_re, a_im, log_dt, b_re, b_im, c_re, c_im):
    a4 = lambda a: a.reshape(DEPTH, G_C, 1, P_C)
    bt = lambda b: jnp.transpose(b, (0, 1, 3, 2))
    vec = pl.BlockSpec((None, G_C, 1, P_C), lambda l: (l, 0, 0, 0))
    dts = pl.BlockSpec((None, G_C, 1, 1), lambda l: (l, 0, 0, 0))
    mat = pl.BlockSpec((None, G_C, GC, P_C), lambda l: (l, 0, 0, 0))
    pws = pl.BlockSpec((None, G_C, SUB_T, P_C), lambda l: (l, 0, 0, 0))
    dense = pl.BlockSpec((None, W_C, 2 * SLAB_STATES), lambda l: (l, 0, 0))
    vshape = jax.ShapeDtypeStruct((DEPTH, G_C, 1, P_C), F32)
    pshape = jax.ShapeDtypeStruct((DEPTH, G_C, SUB_T, P_C), F32)
    dshape = jax.ShapeDtypeStruct((DEPTH, W_C, 2 * SLAB_STATES), BF16)
    return pl.pallas_call(
        _s5_prep_kernel,
        grid=(DEPTH,),
        in_specs=[vec, vec, dts, mat, mat, mat, mat],
        out_specs=[vec, vec, pws, pws, dense, dense],
        out_shape=[vshape, vshape, pshape, pshape, dshape, dshape],
        name="s5_prep",
    )(a4(a_re), a4(a_im), log_dt.reshape(DEPTH, G_C, 1, 1), bt(b_re), bt(b_im), c_re, c_im)


def _rope(x, tab_ref):
    c, s_lo, s_hi = tab_ref[0], tab_ref[1], tab_ref[2]
    half = ROPE_DIM // 2
    tiles = []
    for j in range(x.shape[1] // LANES):
        t = x[:, j * LANES:(j + 1) * LANES]
        tiles.append(t * c + pltpu.roll(t, half, 1) * s_lo + pltpu.roll(t, LANES - half, 1) * s_hi)
    return jnp.concatenate(tiles, axis=1) if len(tiles) > 1 else tiles[0]


def _chunk_mlp(mix_w, bias_ref, z_ref):
    outs = []
    for h in range(H_A):
        v = z_ref[:, O_VA + h * C_A:O_VA + (h + 1) * C_A].astype(BF16)
        zz = jnp.dot(mix_w(h), v, preferred_element_type=F32) + bias_ref[:, h:h + 1]
        u = z_ref[:, O_UA + h * C_A:O_UA + (h + 1) * C_A]
        g = z_ref[:, O_GA + h * C_A:O_GA + (h + 1) * C_A]
        outs.append(u * zz * jax.nn.silu(g))
    return jnp.concatenate(outs, axis=1)


def _drain(steps):
    try:
        while True:
            next(steps)
    except StopIteration as done:
        return done.value


def _attention(q, kcat, vcat, sink, prev_off):
    t_len = q.shape[0]
    lane_grp = lax.broadcasted_iota(jnp.int32, (t_len, KV_W), 1) // HD_B
    pieces = []
    for r in range(REP_B):
        chunk = q[:, r * KV_W:(r + 1) * KV_W]
        for g in range(KV_B):
            pieces.append(jnp.where(lane_grp == g, chunk, 0.0))
    qbd = jnp.concatenate(pieces, axis=0).astype(BF16)
    s_all = lax.dot_general(qbd, kcat, _NT, preferred_element_type=F32)
    row = lax.broadcasted_iota(jnp.int32, (t_len, 2 * WINDOW), 0)
    col = lax.broadcasted_iota(jnp.int32, (t_len, 2 * WINDOW), 1)
    visible = jnp.where(col < WINDOW, col - row - prev_off, row - col + WINDOW + 1) > 0
    heads = [(r, g) for r in range(REP_B) for g in range(KV_B)]
    masked = lambda hh: jnp.where(visible, s_all[hh * t_len:(hh + 1) * t_len], NEG)
    yield VPU_RUN_NEXT
    tops = [jnp.maximum(jnp.max(masked(hh), axis=-1, keepdims=True), sink(g, r))
            for hh, (r, g) in enumerate(heads)]
    yield
    probs, scales = [], []
    for hh, (r, g) in enumerate(heads):
        p = jnp.exp(masked(hh) - tops[hh])
        den = jnp.sum(p, axis=-1, keepdims=True) + jnp.exp(sink(g, r) - tops[hh])
        probs.append(p.astype(BF16))
        scales.append(1.0 / den)
        if hh == len(heads) // 2 - 1:
            yield VPU_RUN_NEXT
    yield
    p_all = jnp.concatenate(probs, axis=0)
    o_all = jnp.dot(p_all, vcat, preferred_element_type=F32)
    yield
    outs = []
    for r in range(REP_B):
        acc = None
        for g in range(KV_B):
            hh = r * KV_B + g
            o = o_all[hh * t_len:(hh + 1) * t_len] * scales[hh]
            acc = o if acc is None else jnp.where(lane_grp == g, o, acc)
        outs.append(acc)
    return jnp.concatenate(outs, axis=1)


def _lane_tile(n):
    return slice(n * LANES, (n + 1) * LANES)


def _seq_major_perm(n_seq, n_step, transpose):
    m = n_seq * n_step
    row = lax.broadcasted_iota(jnp.int32, (m, m), 0)
    col = lax.broadcasted_iota(jnp.int32, (m, m), 1)
    if transpose:
        hit = row == (col % n_seq) * n_step + col // n_seq
    else:
        hit = col == (row % n_seq) * n_step + row // n_seq
    return jnp.where(hit, 1.0, 0.0).astype(BF16)


def _h_lanes(n):
    base = (n // SLAB_TILES) * 2 * SLAB_STATES + (n % SLAB_TILES) * LANES
    return slice(base, base + LANES), slice(base + SLAB_STATES, base + SLAB_STATES + LANES)


def _s5_drive(h_ref, z_ref, bcat_ref, n_seq, n_step):
    u = z_ref[:, O_UC:O_UC + W_C].astype(BF16)
    u = jnp.dot(_seq_major_perm(n_seq, n_step, False), u, preferred_element_type=F32).astype(BF16)
    for s in range(N_SLAB):
        h_ref[:, s * 2 * SLAB_STATES:(s + 1) * 2 * SLAB_STATES] = jnp.dot(
            u[:, _lane_tile(s)], bcat_ref[_lane_tile(s), :], preferred_element_type=F32)


def _s5_scan(h_ref, a_ref, n_seq, n_step, init):
    group = max(1, SCAN_VREGS * SUBLANES // n_seq)
    finals = [None] * N_TILE
    for n0 in range(0, N_TILE, group):
        tiles = range(n0, n0 + group)
        state = {n: init(n) for n in tiles}
        for j in range(n_step):
            rows = slice(j * n_seq, (j + 1) * n_seq)
            for n in tiles:
                re_l, im_l = _h_lanes(n)
                ar, ai = a_ref[0:1, _lane_tile(n)], a_ref[1:2, _lane_tile(n)]
                hr, hi = state[n]
                hr, hi = (ar * hr - ai * hi + h_ref[rows, re_l],
                          ar * hi + ai * hr + h_ref[rows, im_l])
                h_ref[rows, re_l] = hr
                h_ref[rows, im_l] = hi
                state[n] = (hr, hi)
        for n in tiles:
            finals[n] = state[n]
    return finals


def _s5_readout(h_ref, z_ref, ccat_ref, d_ref, wglu_ref, bglu_ref, n_seq, n_step):
    y = jnp.concatenate([
        lax.dot_general(h_ref[:, s * 2 * SLAB_STATES:(s + 1) * 2 * SLAB_STATES].astype(BF16),
                        ccat_ref[_lane_tile(s), :], _NT, preferred_element_type=F32)
        for s in range(N_SLAB)], axis=1)
    y_hi = y.astype(BF16)
    y_lo = (y - y_hi.astype(F32)).astype(BF16)
    back = _seq_major_perm(n_seq, n_step, True)
    y = (jnp.dot(back, y_hi, preferred_element_type=F32)
         + jnp.dot(back, y_lo, preferred_element_type=F32))
    y = jax.nn.gelu(y + d_ref[...] * z_ref[:, O_UC:O_UC + W_C])
    gate = jnp.dot(y.astype(BF16), wglu_ref[...], preferred_element_type=F32) + bglu_ref[...]
    y = y * jax.nn.sigmoid(gate)
    return y * jax.nn.silu(z_ref[:, O_GC:O_GC + W_C])


def _mix_block_steps(i, sinks_ref, z_ref, rope_ref, ws_ref, bias_ref, a_ref, pwr_ref, pwi_ref,
                     bcat_ref, ccat_ref, d_ref, wglu_ref, bglu_ref,
                     mixed_ref, klast_ref, vlast_ref, hfin_ref,
                     prev_kv, h_ref, hin_ref, carry_ref, layer):
    r_i = lax.broadcasted_iota(jnp.int32, (CHUNK, CHUNK), 0)
    c_i = lax.broadcasted_iota(jnp.int32, (CHUNK, CHUNK), 1)
    causal_w = lambda h: jnp.where(c_i <= r_i, ws_ref[h], 0.0).astype(BF16)
    mixed_ref[:, 0:W_A] = _chunk_mlp(causal_w, bias_ref, z_ref).astype(BF16)
    yield VPU_RUN_NEXT

    q = _rope(z_ref[:, O_Q:O_Q + W_B], rope_ref) * (HD_B ** -0.5)
    k = _rope(z_ref[:, O_K:O_K + KV_W], rope_ref)
    v = z_ref[:, O_V:O_V + KV_W]
    klast_ref[...] = k
    vlast_ref[...] = v
    kb, vb = k.astype(BF16), v.astype(BF16)
    kcat = jnp.concatenate([prev_kv[0], kb], axis=0)
    vcat = jnp.concatenate([prev_kv[1], vb], axis=0)
    prev_off = jnp.where(i > 0, 0, 2 * WINDOW)
    sink = lambda g, r: sinks_ref[layer, g * REP_B + r]
    yield
    o_b = yield from _attention(q, kcat, vcat, sink, prev_off)
    mixed_ref[:, W_A:W_A + W_B] = (o_b * jax.nn.silu(z_ref[:, O_GB:O_GB + W_B])).astype(BF16)
    yield

    _s5_drive(h_ref, z_ref, bcat_ref, SUBLANES, SUB_T)
    yield VPU_RUN_NEXT
    zero = jnp.zeros((SUBLANES, LANES), F32)
    finals = _s5_scan(h_ref, a_ref, SUBLANES, SUB_T, lambda n: (zero, zero))
    yield
    for n in range(N_TILE):
        st_l = _lane_tile(n)
        re_l, im_l = _h_lanes(n)
        end_r, end_i = finals[n]
        a_t_r, a_t_i = pwr_ref[SUB_T - 1:SUB_T, st_l], pwi_ref[SUB_T - 1:SUB_T, st_l]
        cr, ci = carry_ref[0:1, st_l], carry_ref[1:2, st_l]
        for r in range(SUBLANES):
            hin_ref[r:r + 1, re_l] = cr
            hin_ref[r:r + 1, im_l] = ci
            cr, ci = (end_r[r:r + 1] + a_t_r * cr - a_t_i * ci,
                      end_i[r:r + 1] + a_t_r * ci + a_t_i * cr)
        carry_ref[0:1, st_l] = cr
        carry_ref[1:2, st_l] = ci
        in_r, in_i = hin_ref[:, re_l], hin_ref[:, im_l]
        for j in range(SUB_T):
            rows = slice(j * SUBLANES, (j + 1) * SUBLANES)
            p_r, p_i = pwr_ref[j:j + 1, st_l], pwi_ref[j:j + 1, st_l]
            h_ref[rows, re_l] = h_ref[rows, re_l] + (p_r * in_r - p_i * in_i)
            h_ref[rows, im_l] = h_ref[rows, im_l] + (p_r * in_i + p_i * in_r)
        if n == N_TILE // 2 - 1:
            yield VPU_RUN_NEXT
    yield
    hfin_ref[...] = carry_ref[...]
    mixed_ref[:, W_A + W_B:] = _s5_readout(
        h_ref, z_ref, ccat_ref, d_ref, wglu_ref, bglu_ref, SUBLANES, SUB_T).astype(BF16)
    return kb, vb


def _prompt_layer_kernel(*refs, layer, halves_per_seq, n_half, prep_next):
    refs = list(refs)
    (sinks_ref, x_ref, xs_ref, g_ref, w_ref, rope_ref, ws_ref, bias_ref, a_ref, pwr_ref, pwi_ref,
     bcat_ref, ccat_ref, d_ref, wglu_ref, bglu_ref) = refs[:16]
    del refs[:16]
    wn_ref = refs.pop(0) if prep_next else None
    mixed_ref, klast_ref, vlast_ref, hfin_ref, zs_ref = refs[:5]
    del refs[:5]
    wnb_ref = refs.pop(0) if prep_next else None
    z_a, z_b, kprev_ref, vprev_ref, h_ref, hin_ref, carry_ref, zs_sem = refs
    s = pl.program_id(0)
    half = jnp.maximum(s - 1, 0) % halves_per_seq

    @pl.when(s == 0)
    def _():
        z_b[...] = jnp.zeros_like(z_b)

    @pl.when(half == 0)
    def _():
        kprev_ref[...] = jnp.zeros_like(kprev_ref)
        vprev_ref[...] = jnp.zeros_like(vprev_ref)
        carry_ref[...] = jnp.zeros_like(carry_ref)

    def step(z_new, z_cur):
        if prep_next:
            _w_in_prep_kernel(wn_ref, wnb_ref)
        x = jnp.where(s == n_half, xs_ref[...], x_ref[...])
        xn = _rms(x, g_ref[...]).astype(BF16)
        starts = list(range(0, D_IN, PROJ_COLS))

        def project():
            if starts:
                n0 = starts.pop(0)
                z_new[:, n0:n0 + PROJ_COLS] = jnp.dot(
                    xn, w_ref[:, n0:n0 + PROJ_COLS], preferred_element_type=F32)

        prev_kv = (kprev_ref[...], vprev_ref[...])
        for sb in range(FUSE_ROWS // CHUNK):
            rows = pl.ds(sb * CHUNK, CHUNK)
            steps = _mix_block_steps(
                half * (FUSE_ROWS // CHUNK) + sb, sinks_ref, z_cur.at[rows],
                rope_ref.at[:, rows], ws_ref, bias_ref, a_ref, pwr_ref, pwi_ref, bcat_ref,
                ccat_ref, d_ref, wglu_ref, bglu_ref, mixed_ref.at[rows], klast_ref,
                vlast_ref, hfin_ref, prev_kv, h_ref, hin_ref, carry_ref, layer)
            while True:
                try:
                    vpu_run_next = next(steps)
                except StopIteration as done:
                    prev_kv = done.value
                    break
                if vpu_run_next:
                    project()
        kprev_ref[...], vprev_ref[...] = prev_kv
        while starts:
            project()

    pl.when(s % 2 == 0)(lambda: step(z_a, z_b))
    pl.when(s % 2 == 1)(lambda: step(z_b, z_a))

    @pl.when(s == n_half)
    def _():
        copy = pltpu.make_async_copy((z_a, z_b)[n_half % 2], zs_ref, zs_sem)
        copy.start()
        copy.wait()


def _prompt_layer(x2d, xs2d, seq, layer, g_all, w_b, w_in_f32, sinks, rope_tab, ws, bias_t, a_rows,
                  pw_re, pw_im, bcat, ccat, d, wglu, bglu):
    m = x2d.shape[0]
    assert xs2d.shape == (FUSE_ROWS, D_MODEL), xs2d.shape
    bsz = m // seq
    n_half = m // FUSE_ROWS
    halves_per_seq = seq // FUSE_ROWS
    prep_next = w_in_f32 is not None
    slab = D_MODEL // n_half
    assert slab * n_half == D_MODEL and slab % (2 * SUBLANES) == 0, (n_half, slab)
    per_layer = lambda *shape, **kw: _layer_spec(shape, layer, **kw)
    once = dict(pipeline_mode=pl.Buffered(1))
    done = lambda s: jnp.maximum(s - 1, 0)
    last = lambda s: jnp.minimum(s, n_half - 1)
    next_in = [pl.BlockSpec((None, slab, D_IN), lambda s: (layer + 1, last(s), 0))] * prep_next
    next_out = [pl.BlockSpec((slab, D_IN), lambda s: (last(s), 0))] * prep_next
    next_shape = [jax.ShapeDtypeStruct((D_MODEL, D_IN), BF16)] * prep_next
    return pl.pallas_call(
        functools.partial(_prompt_layer_kernel, layer=layer, halves_per_seq=halves_per_seq,
                          n_half=n_half, prep_next=prep_next),
        grid=(n_half + 1,),
        in_specs=[
            pl.BlockSpec(memory_space=pltpu.SMEM),
            pl.BlockSpec((FUSE_ROWS, D_MODEL), lambda s: (last(s), 0)),
            pl.BlockSpec((FUSE_ROWS, D_MODEL), lambda s: (0, 0), **once),
            per_layer(1, D_MODEL),
            pl.BlockSpec((D_MODEL, D_IN), lambda s: (0, 0), **once),
            pl.BlockSpec((3, FUSE_ROWS, LANES), lambda s: (0, done(s) % halves_per_seq, 0)),
            per_layer(H_A, CHUNK, CHUNK),
            per_layer(CHUNK, H_A),
            per_layer(2, N_STATE),
            per_layer(SUB_T, N_STATE),
            per_layer(SUB_T, N_STATE),
            per_layer(W_C, 2 * SLAB_STATES, **once),
            per_layer(W_C, 2 * SLAB_STATES, **once),
            per_layer(1, W_C),
            per_layer(W_C, W_C, **once),
            per_layer(1, W_C),
        ] + next_in,
        out_specs=[
            pl.BlockSpec((FUSE_ROWS, D_MODEL), lambda s: (done(s), 0)),
            pl.BlockSpec((None, WINDOW, KV_W), lambda s: (done(s) // halves_per_seq, 0, 0)),
            pl.BlockSpec((None, WINDOW, KV_W), lambda s: (done(s) // halves_per_seq, 0, 0)),
            pl.BlockSpec((None, 2, N_STATE), lambda s: (done(s) // halves_per_seq, 0, 0)),
            pl.BlockSpec(memory_space=pl.ANY),
        ] + next_out,
        out_shape=[
            jax.ShapeDtypeStruct((m, D_MODEL), BF16),
            jax.ShapeDtypeStruct((bsz, WINDOW, KV_W), F32),
            jax.ShapeDtypeStruct((bsz, WINDOW, KV_W), F32),
            jax.ShapeDtypeStruct((bsz, 2, N_STATE), F32),
            jax.ShapeDtypeStruct((FUSE_ROWS, D_IN), F32),
        ] + next_shape,
        scratch_shapes=[
            pltpu.VMEM((FUSE_ROWS, D_IN), F32),
            pltpu.VMEM((FUSE_ROWS, D_IN), F32),
            pltpu.VMEM((WINDOW, KV_W), BF16),
            pltpu.VMEM((WINDOW, KV_W), BF16),
            pltpu.VMEM((CHUNK, 2 * N_STATE), F32),
            pltpu.VMEM((SUBLANES, 2 * N_STATE), F32),
            pltpu.VMEM((2, N_STATE), F32),
            pltpu.SemaphoreType.DMA(()),
        ],
        compiler_params=pltpu.CompilerParams(
            dimension_semantics=("arbitrary",), vmem_limit_bytes=VMEM_LIMIT),
        name="prompt_layer",
    )(sinks, x2d, xs2d, g_all, w_b, rope_tab, ws, bias_t, a_rows, pw_re, pw_im, bcat, ccat, d,
      wglu, bglu, *([w_in_f32] * prep_next))


def _mix_sample_kernel(sinks_ref, z_ref, rope_ref, ck_ref, cv_ref, ws_ref, bias_ref, a_ref,
                       h0r_ref, h0i_ref, bcat_ref, ccat_ref, d_ref, wglu_ref, bglu_ref,
                       mixed_ref, knew_ref, hre_ref, him_ref,
                       q_ref, h_ref, *, layer, n_batch, n_tok):
    b = pl.program_id(0)
    m = n_batch * n_tok

    @pl.when(b == 0)
    def _():
        r_i = lax.broadcasted_iota(jnp.int32, (m, m), 0)
        c_i = lax.broadcasted_iota(jnp.int32, (m, m), 1)
        keep = (r_i // n_tok == c_i // n_tok) & (c_i <= r_i)
        spread = jnp.where(lax.broadcasted_iota(jnp.int32, (m, CHUNK), 0) % n_tok
                           == lax.broadcasted_iota(jnp.int32, (m, CHUNK), 1), 1.0, 0.0).astype(BF16)
        spread_t = jnp.where(lax.broadcasted_iota(jnp.int32, (CHUNK, m), 1) % n_tok
                             == lax.broadcasted_iota(jnp.int32, (CHUNK, m), 0), 1.0, 0.0).astype(BF16)

        def tiled_w(h):
            rows = jnp.dot(spread, ws_ref[h].astype(BF16), preferred_element_type=F32)
            full = jnp.dot(rows.astype(BF16), spread_t, preferred_element_type=F32)
            return jnp.where(keep, full, 0.0).astype(BF16)

        mixed_ref[:, 0:W_A] = _chunk_mlp(tiled_w, bias_ref, z_ref)
        q_ref[...] = _rope(z_ref[:, O_Q:O_Q + W_B], rope_ref) * (HD_B ** -0.5)
        knew_ref[...] = _rope(z_ref[:, O_K:O_K + KV_W], rope_ref)
        _s5_drive(h_ref, z_ref, bcat_ref, n_batch, n_tok)
        finals = _s5_scan(h_ref, a_ref, n_batch, n_tok,
                          lambda n: (h0r_ref[:, _lane_tile(n)], h0i_ref[:, _lane_tile(n)]))
        for n in range(N_TILE):
            hre_ref[:, _lane_tile(n)] = finals[n][0]
            him_ref[:, _lane_tile(n)] = finals[n][1]
        mixed_ref[:, W_A + W_B:] = _s5_readout(
            h_ref, z_ref, ccat_ref, d_ref, wglu_ref, bglu_ref, n_batch, n_tok)

    pad = jnp.zeros((WINDOW - n_tok, KV_W), F32)
    sink = lambda g, r: sinks_ref[layer, g * REP_B + r]
    row_sets, running = [], []
    for j in range(SAMPLE_GROUP):
        rows = pl.ds(pl.multiple_of((b * SAMPLE_GROUP + j) * n_tok, n_tok), n_tok)
        kcat = jnp.concatenate([ck_ref[j], knew_ref[rows, :], pad], axis=0).astype(BF16)
        vcat = jnp.concatenate([cv_ref[j], z_ref[rows, O_V:O_V + KV_W], pad], axis=0).astype(BF16)
        row_sets.append(rows)
        running.append(_attention(q_ref[rows, :], kcat, vcat, sink, 0))
    results = [None] * SAMPLE_GROUP
    while any(r is None for r in results):
        for j, steps in enumerate(running):
            if results[j] is None:
                try:
                    next(steps)
                except StopIteration as done:
                    results[j] = done.value
    for rows, o_b in zip(row_sets, results):
        mixed_ref[rows, W_A:W_A + W_B] = o_b * jax.nn.silu(z_ref[rows, O_GB:O_GB + W_B])


def _mix_sample(z2, layer, sinks, rope_tab, ck, cv, ws, bias_t, a_rows, h0r, h0i,
                bcat, ccat, d, wglu, bglu, n_batch, n_tok):
    m = n_batch * n_tok
    const = lambda *shape: pl.BlockSpec(shape, lambda b: (0,) * len(shape))
    per_layer = lambda *shape: _layer_spec(shape, layer)
    return pl.pallas_call(
        functools.partial(_mix_sample_kernel, layer=layer, n_batch=n_batch, n_tok=n_tok),
        grid=(n_batch // SAMPLE_GROUP,),
        in_specs=[
            pl.BlockSpec(memory_space=pltpu.SMEM),
            const(m, D_IN),
            const(3, m, LANES),
            pl.BlockSpec((None, SAMPLE_GROUP, WINDOW, KV_W), lambda b: (layer, b, 0, 0)),
            pl.BlockSpec((None, SAMPLE_GROUP, WINDOW, KV_W), lambda b: (layer, b, 0, 0)),
            per_layer(H_A, CHUNK, CHUNK),
            per_layer(m, H_A),
            per_layer(2, N_STATE),
            per_layer(n_batch, N_STATE),
            per_layer(n_batch, N_STATE),
            per_layer(W_C, 2 * SLAB_STATES),
            per_layer(W_C, 2 * SLAB_STATES),
            per_layer(1, W_C),
            per_layer(W_C, W_C),
            per_layer(1, W_C),
        ],
        out_specs=[
            const(m, D_MODEL),
            const(m, KV_W),
            const(n_batch, N_STATE),
            const(n_batch, N_STATE),
        ],
        out_shape=[
            jax.ShapeDtypeStruct((m, D_MODEL), F32),
            jax.ShapeDtypeStruct((m, KV_W), F32),
            jax.ShapeDtypeStruct((n_batch, N_STATE), F32),
            jax.ShapeDtypeStruct((n_batch, N_STATE), F32),
        ],
        scratch_shapes=[
            pltpu.VMEM((m, W_B), F32),
            pltpu.VMEM((m, 2 * N_STATE), F32),
        ],
        compiler_params=pltpu.CompilerParams(
            dimension_semantics=("arbitrary",), vmem_limit_bytes=VMEM_LIMIT),
        name="mix_sample",
    )(sinks, z2, rope_tab, ck, cv, ws, bias_t, a_rows, h0r, h0i, bcat, ccat, d, wglu, bglu)


def _rope_table(pos):
    half = ROPE_DIM // 2
    inv = ROPE_THETA ** (-jnp.arange(half, dtype=F32) * 2.0 / ROPE_DIM)
    ang = pos[:, None] * inv[None, :]
    cos, sin = jnp.cos(ang), jnp.sin(ang)
    n = pos.shape[0]
    rest = HD_B - ROPE_DIM
    c = jnp.concatenate([cos, cos, jnp.ones((n, rest), F32)], axis=1)
    s_lo = jnp.concatenate([jnp.zeros((n, half), F32), sin, jnp.zeros((n, rest), F32)], axis=1)
    s_hi = jnp.concatenate([-sin, jnp.zeros((n, half + rest), F32)], axis=1)
    tab = jnp.stack([c, s_lo, s_hi])
    return jnp.concatenate([tab] * (LANES // HD_B), axis=2)


def kernel(x_prompt, x_sample, cache_swa_k, cache_swa_v, state_ssm_re, state_ssm_im, norm_g,
           final_norm_g, w_in, w_out, chunk_w_s, chunk_b_s, attn_sinks, ssm_a_re, ssm_a_im,
           ssm_log_dt, ssm_b_re, ssm_b_im, ssm_c_re, ssm_c_im, ssm_d, glu_w, glu_b):
    bsz, seq, _ = x_prompt.shape
    n_batch, n_tok, _ = x_sample.shape
    m_s = n_batch * n_tok

    w_in_b, w_out_b = _prep_first_weights(w_in, w_out)
    wglu_b = glu_w.astype(BF16)

    ab_re, ab_im, pw_re, pw_im, bcat, ccat = _s5_prep(
        ssm_a_re, ssm_a_im, ssm_log_dt, ssm_b_re, ssm_b_im, ssm_c_re, ssm_c_im)
    a_rows = jnp.concatenate([ab_re.reshape(DEPTH, 1, N_STATE),
                              ab_im.reshape(DEPTH, 1, N_STATE)], axis=1)
    pw_re = jnp.transpose(pw_re, (0, 2, 1, 3)).reshape(DEPTH, SUB_T, N_STATE)
    pw_im = jnp.transpose(pw_im, (0, 2, 1, 3)).reshape(DEPTH, SUB_T, N_STATE)

    rope_p = _rope_table(jnp.arange(seq, dtype=F32))
    rope_s = jnp.tile(_rope_table(jnp.arange(n_tok, dtype=F32) + PAST_LEN), (1, n_batch, 1))
    bias_p = jnp.transpose(chunk_b_s, (0, 2, 1))
    bias_s = jnp.tile(bias_p[:, :n_tok], (1, n_batch, 1))
    ck = cache_swa_k.reshape(DEPTH, n_batch, WINDOW, KV_W)
    cv = cache_swa_v.reshape(DEPTH, n_batch, WINDOW, KV_W)
    h0r = state_ssm_re.reshape(DEPTH, n_batch, N_STATE)
    h0i = state_ssm_im.reshape(DEPTH, n_batch, N_STATE)
    g_all = norm_g.reshape(DEPTH, 1, D_MODEL)
    d_all = ssm_d.reshape(DEPTH, 1, W_C)
    bglu_all = glu_b.reshape(DEPTH, 1, W_C)
    fg = final_norm_g.reshape(1, D_MODEL)

    xp = x_prompt.reshape(bsz * seq, D_MODEL)
    xs = x_sample.reshape(m_s, D_MODEL)
    outs = [[] for _ in range(9)]
    for l in range(DEPTH):
        more = l + 1 < DEPTH
        mixed_p, k_last, v_last, h_fin, zs, *w_in_next = _prompt_layer(
            xp, xs, seq, l, g_all, w_in_b, w_in if more else None, attn_sinks, rope_p, chunk_w_s,
            bias_p, a_rows, pw_re, pw_im, bcat, ccat, d_all, wglu_b, bglu_all)
        if more:
            xp, w_out_next = _out_proj(mixed_p, xp, w_out_b, fg, l, w_out)
        else:
            xp = _out_proj(mixed_p, xp, w_out_b, fg, l)

        mixed_s, k_new, h_re_s, h_im_s = _mix_sample(
            zs, l, attn_sinks, rope_s, ck, cv, chunk_w_s, bias_s, a_rows, h0r, h0i,
            bcat, ccat, d_all, wglu_b, bglu_all, n_batch, n_tok)
        xs = _out_proj(mixed_s, xs, w_out_b, fg, l)
        if more:
            w_in_b, w_out_b = w_in_next[0], w_out_next

        outs[0].append(k_last.reshape(bsz, WINDOW, KV_B, HD_B))
        outs[1].append(v_last.reshape(bsz, WINDOW, KV_B, HD_B))
        outs[2].append(k_new.reshape(n_batch, n_tok, KV_B, HD_B))
        outs[3].append(zs[:, O_V:O_V + KV_W].reshape(n_batch, n_tok, KV_B, HD_B))
        outs[4].append(h_fin[:, 0].reshape(bsz, G_C, P_C))
        outs[5].append(h_fin[:, 1].reshape(bsz, G_C, P_C))
        outs[6].append(h_re_s.reshape(n_batch, G_C, P_C))
        outs[7].append(h_im_s.reshape(n_batch, G_C, P_C))
        outs[8].append(zs[:, O_VA:O_VA + W_A].reshape(n_batch, n_tok, W_A))

    return (xp.reshape(bsz, seq, D_MODEL), xs.reshape(n_batch, n_tok, D_MODEL),
            *[jnp.stack(o) for o in outs])
```

```python
import functools

import jax
import jax.numpy as jnp
from jax import lax
from jax.experimental import pallas as pl
from jax.experimental.pallas import tpu as pltpu

F32 = jnp.float32
BF16 = jnp.bfloat16

D_MODEL = 2048
DEPTH = 4
PAST_LEN = 16384
CHUNK = 128
W_A = 512
H_A = 4
C_A = W_A // H_A
W_B = 1024
HD_B = 64
H_B = W_B // HD_B
KV_B = 4
REP_B = H_B // KV_B
KV_W = KV_B * HD_B
WINDOW = 128
ROPE_DIM = HD_B // 4
ROPE_THETA = 500000.0
W_C = 512
GC = 16
G_C = W_C // GC
P_C = 64
N_STATE = G_C * P_C
D_IN = 3 * W_A + 2 * W_B + 2 * KV_W + 2 * W_C
EPS = 1e-5
NEG = -1e30

O_UA, O_VA, O_GA = 0, W_A, 2 * W_A
O_Q = 3 * W_A
O_K = O_Q + W_B
O_V = O_K + KV_W
O_GB = O_V + KV_W
O_UC = O_GB + W_B
O_GC = O_UC + W_C

LANES = 128
SUBLANES = 8
N_TILE = N_STATE // LANES
N_SLAB = W_C // LANES
SLAB_GROUPS = G_C // N_SLAB
SLAB_STATES = SLAB_GROUPS * P_C
SLAB_TILES = SLAB_STATES // LANES
VMEM_LIMIT = 60 * 1024 * 1024

OUT_ROWS = 512
PROJ_COLS = 512
FUSE_ROWS = 256
SAMPLE_GROUP = 4
VPU_RUN_NEXT = True
W_PREP_ROWS = 256
W_PREP_COLS = 512
SCAN_VREGS = 4
SUB_T = CHUNK // SUBLANES

_NT = (((1,), (1,)), ((), ()))


def _rms(x, g):
    return x * lax.rsqrt(jnp.mean(x * x, axis=-1, keepdims=True) + EPS) * g


def _layer_spec(shape, layer, **kw):
    zeros = (0,) * len(shape)
    return pl.BlockSpec((None,) + tuple(shape), lambda *_: (layer,) + zeros, **kw)


def _out_proj_kernel(m_ref, x_ref, w_ref, fg_ref, *rest, final, prep_next):
    if prep_next:
        *wn_refs, y_ref, wnb_ref = rest
        for j, wn_ref in enumerate(wn_refs):
            wnb_ref[j * HD_B:(j + 1) * HD_B, :] = wn_ref[...].astype(BF16)
    else:
        y_ref, = rest
    y = x_ref[...] + jnp.dot(m_ref[...].astype(BF16), w_ref[...], preferred_element_type=F32)
    if final:
        y = _rms(y, fg_ref[...])
    y_ref[...] = y


def _w_out_source_block(i):
    first, last = W_A // HD_B, (W_A + W_B) // HD_B
    head = i - first
    src = first + (head % KV_B) * REP_B + head // KV_B
    return jnp.where((i >= first) & (i < last), src, i)


def _out_proj(mixed, x2d, w_b, fg, layer, w_out_f32=None):
    m = x2d.shape[0]
    tm = min(OUT_ROWS, m)
    final = layer == DEPTH - 1
    prep_next = w_out_f32 is not None
    in_specs = [
        pl.BlockSpec((tm, D_MODEL), lambda i: (i, 0)),
        pl.BlockSpec((tm, D_MODEL), lambda i: (i, 0)),
        pl.BlockSpec((D_MODEL, D_MODEL), lambda i: (0, 0), pipeline_mode=pl.Buffered(1)),
        pl.BlockSpec((1, D_MODEL), lambda i: (0, 0)),
    ]
    out_specs = [pl.BlockSpec((tm, D_MODEL), lambda i: (i, 0))]
    out_shape = [jax.ShapeDtypeStruct((m, D_MODEL), F32)]
    args = [mixed, x2d, w_b, fg]
    if prep_next:
        per_step = (D_MODEL // HD_B) // (m // tm)
        assert per_step * (m // tm) * HD_B == D_MODEL, (m, tm)
        for j in range(per_step):
            in_specs.append(pl.BlockSpec(
                (None, HD_B, D_MODEL),
                lambda i, j=j: (layer + 1, _w_out_source_block(i * per_step + j), 0)))
            args.append(w_out_f32)
        out_specs.append(pl.BlockSpec((per_step * HD_B, D_MODEL), lambda i: (i, 0)))
        out_shape.append(jax.ShapeDtypeStruct((D_MODEL, D_MODEL), BF16))
    out = pl.pallas_call(
        functools.partial(_out_proj_kernel, final=final, prep_next=prep_next),
        grid=(m // tm,),
        in_specs=in_specs,
        out_specs=out_specs,
        out_shape=out_shape,
        compiler_params=pltpu.CompilerParams(
            dimension_semantics=("arbitrary",), vmem_limit_bytes=VMEM_LIMIT),
        name="out_proj_final" if final else "out_proj",
    )(*args)
    return out if prep_next else out[0]


def _rep_major_source(new_head):
    r, g = divmod(new_head, KV_B)
    return g * REP_B + r


def _w_in_prep_kernel(w_ref, o_ref):
    heads_per_tile = LANES // HD_B
    lane_lo = lax.broadcasted_iota(jnp.int32, (w_ref.shape[0], LANES), 1) < HD_B

    def half(sec, head, want_hi):
        t = w_ref[:, sec + (head // heads_per_tile) * LANES:sec + (head // heads_per_tile + 1) * LANES]
        return t if (head % heads_per_tile == 1) == want_hi else pltpu.roll(t, HD_B, 1)

    for c0, c1, permute in ((0, O_Q, False), (O_Q, O_K, True), (O_K, O_GB, False),
                            (O_GB, O_UC, True), (O_UC, D_IN, False)):
        if not permute:
            o_ref[:, c0:c1] = w_ref[:, c0:c1].astype(BF16)
            continue
        for j in range(W_B // LANES):
            lo = half(c0, _rep_major_source(heads_per_tile * j), False)
            hi = half(c0, _rep_major_source(heads_per_tile * j + 1), True)
            o_ref[:, c0 + j * LANES:c0 + (j + 1) * LANES] = jnp.where(lane_lo, lo, hi).astype(BF16)


def _w_out_prep_kernel(w_ref, o_ref):
    o_ref[0:W_A, :] = w_ref[0:W_A, :].astype(BF16)
    for new_head in range(H_B):
        src = W_A + _rep_major_source(new_head) * HD_B
        dst = W_A + new_head * HD_B
        o_ref[dst:dst + HD_B, :] = w_ref[src:src + HD_B, :].astype(BF16)
    o_ref[W_A + W_B:, :] = w_ref[W_A + W_B:, :].astype(BF16)


def _prep_first_weights(w_in, w_out):
    w_in_b = pl.pallas_call(
        _w_in_prep_kernel,
        grid=(D_MODEL // W_PREP_ROWS,),
        in_specs=[pl.BlockSpec((None, W_PREP_ROWS, D_IN), lambda i: (0, i, 0))],
        out_specs=pl.BlockSpec((W_PREP_ROWS, D_IN), lambda i: (i, 0)),
        out_shape=jax.ShapeDtypeStruct((D_MODEL, D_IN), BF16),
        compiler_params=pltpu.CompilerParams(
            dimension_semantics=("arbitrary",), vmem_limit_bytes=VMEM_LIMIT),
        name="w_in_prep",
    )(w_in)
    w_out_b = pl.pallas_call(
        _w_out_prep_kernel,
        grid=(D_MODEL // W_PREP_COLS,),
        in_specs=[pl.BlockSpec((None, D_MODEL, W_PREP_COLS), lambda i: (0, 0, i))],
        out_specs=pl.BlockSpec((D_MODEL, W_PREP_COLS), lambda i: (0, i)),
        out_shape=jax.ShapeDtypeStruct((D_MODEL, D_MODEL), BF16),
        compiler_params=pltpu.CompilerParams(
            dimension_semantics=("arbitrary",), vmem_limit_bytes=VMEM_LIMIT),
        name="w_out_prep",
    )(w_out)
    return w_in_b, w_out_b


def _store_block_diag(out_ref, lane0, x3):
    per_tile = LANES // P_C
    lane_slot = lax.broadcasted_iota(jnp.int32, (GC, LANES), 1) // P_C
    for g in range(G_C):
        xg = x3[g]
        pair = jnp.concatenate([xg] * per_tile, axis=1)
        tile = jnp.where(lane_slot == g % per_tile, pair, 0.0)
        l0 = lane0 + ((g % SLAB_GROUPS) // per_tile) * LANES
        out_ref[g * GC:(g + 1) * GC, l0:l0 + LANES] = tile.astype(BF16)


def _s5_prep_kernel(are_ref, aim_ref, ldt_ref, btr_ref, bti_ref, cre_ref, cim_ref,
                    abr_ref, abi_ref, pwr_ref, pwi_ref, bcat_ref, ccat_ref):
    a_re, a_im = are_ref[...], aim_ref[...]
    dt = jnp.exp(ldt_ref[...])
    mag = jnp.exp(a_re * dt)
    ab_re = mag * jnp.cos(a_im * dt)
    ab_im = mag * jnp.sin(a_im * dt)
    nr, ni = ab_re - 1.0, ab_im
    den = a_re * a_re + a_im * a_im
    f_re = (nr * a_re + ni * a_im) / den
    f_im = (ni * a_re - nr * a_im) / den
    br, bi = btr_ref[...], bti_ref[...]
    bcat_ref[...] = jnp.zeros_like(bcat_ref)
    ccat_ref[...] = jnp.zeros_like(ccat_ref)
    _store_block_diag(bcat_ref, 0, f_re * br - f_im * bi)
    _store_block_diag(bcat_ref, SLAB_STATES, f_re * bi + f_im * br)
    _store_block_diag(ccat_ref, 0, cre_ref[...])
    _store_block_diag(ccat_ref, SLAB_STATES, -cim_ref[...])
    abr_ref[...] = ab_re
    abi_ref[...] = ab_im
    p_re, p_im = ab_re, ab_im
    for j in range(SUB_T):
        pwr_ref[:, j:j + 1, :] = p_re
        pwi_ref[:, j:j + 1, :] = p_im
        p_re, p_im = p_re * ab_re - p_im * ab_im, p_re * ab_im + p_im * ab_re


def _s5_prep(a_re, a_im, log_dt, b_re, b_im, c_re, c_im):
    a4 = lambda a: a.reshape(DEPTH, G_C, 1, P_C)
    bt = lambda b: jnp.transpose(b, (0, 1, 3, 2))
    vec = pl.BlockSpec((None, G_C, 1, P_C), lambda l: (l, 0, 0, 0))
    dts = pl.BlockSpec((None, G_C, 1, 1), lambda l: (l, 0, 0, 0))
    mat = pl.BlockSpec((None, G_C, GC, P_C), lambda l: (l, 0, 0, 0))
    pws = pl.BlockSpec((None, G_C, SUB_T, P_C), lambda l: (l, 0, 0, 0))
    dense = pl.BlockSpec((None, W_C, 2 * SLAB_STATES), lambda l: (l, 0, 0))
    vshape = jax.ShapeDtypeStruct((DEPTH, G_C, 1, P_C), F32)
    pshape = jax.ShapeDtypeStruct((DEPTH, G_C, SUB_T, P_C), F32)
    dshape = jax.ShapeDtypeStruct((DEPTH, W_C, 2 * SLAB_STATES), BF16)
    return pl.pallas_call(
        _s5_prep_kernel,
        grid=(DEPTH,),
        in_specs=[vec, vec, dts, mat, mat, mat, mat],
        out_specs=[vec, vec, pws, pws, dense, dense],
        out_shape=[vshape, vshape, pshape, pshape, dshape, dshape],
        name="s5_prep",
    )(a4(a_re), a4(a_im), log_dt.reshape(DEPTH, G_C, 1, 1), bt(b_re), bt(b_im), c_re, c_im)


def _rope(x, tab_ref):
    c, s_lo, s_hi = tab_ref[0], tab_ref[1], tab_ref[2]
    half = ROPE_DIM // 2
    tiles = []
    for j in range(x.shape[1] // LANES):
        t = x[:, j * LANES:(j + 1) * LANES]
        tiles.append(t * c + pltpu.roll(t, half, 1) * s_lo + pltpu.roll(t, LANES - half, 1) * s_hi)
    return jnp.concatenate(tiles, axis=1) if len(tiles) > 1 else tiles[0]


def _chunk_mlp(mix_w, bias_ref, z_ref):
    outs = []
    for h in range(H_A):
        v = z_ref[:, O_VA + h * C_A:O_VA + (h + 1) * C_A].astype(BF16)
        zz = jnp.dot(mix_w(h), v, preferred_element_type=F32) + bias_ref[:, h:h + 1]
        u = z_ref[:, O_UA + h * C_A:O_UA + (h + 1) * C_A]
        g = z_ref[:, O_GA + h * C_A:O_GA + (h + 1) * C_A]
        outs.append(u * zz * jax.nn.silu(g))
    return jnp.concatenate(outs, axis=1)


def _drain(steps):
    try:
        while True:
            next(steps)
    except StopIteration as done:
        return done.value


def _attention(q, kcat, vcat, sink, prev_off):
    t_len = q.shape[0]
    lane_grp = lax.broadcasted_iota(jnp.int32, (t_len, KV_W), 1) // HD_B
    pieces = []
    for r in range(REP_B):
        chunk = q[:, r * KV_W:(r + 1) * KV_W]
        for g in range(KV_B):
            pieces.append(jnp.where(lane_grp == g, chunk, 0.0))
    qbd = jnp.concatenate(pieces, axis=0).astype(BF16)
    s_all = lax.dot_general(qbd, kcat, _NT, preferred_element_type=F32)
    row = lax.broadcasted_iota(jnp.int32, (t_len, 2 * WINDOW), 0)
    col = lax.broadcasted_iota(jnp.int32, (t_len, 2 * WINDOW), 1)
    visible = jnp.where(col < WINDOW, col - row - prev_off, row - col + WINDOW + 1) > 0
    heads = [(r, g) for r in range(REP_B) for g in range(KV_B)]
    masked = lambda hh: jnp.where(visible, s_all[hh * t_len:(hh + 1) * t_len], NEG)
    yield VPU_RUN_NEXT
    tops = [jnp.maximum(jnp.max(masked(hh), axis=-1, keepdims=True), sink(g, r))
            for hh, (r, g) in enumerate(heads)]
    yield
    probs, scales = [], []
    for hh, (r, g) in enumerate(heads):
        p = jnp.exp(masked(hh) - tops[hh])
        den = jnp.sum(p, axis=-1, keepdims=True) + jnp.exp(sink(g, r) - tops[hh])
        probs.append(p.astype(BF16))
        scales.append(1.0 / den)
        if hh == len(heads) // 2 - 1:
            yield VPU_RUN_NEXT
    yield
    p_all = jnp.concatenate(probs, axis=0)
    o_all = jnp.dot(p_all, vcat, preferred_element_type=F32)
    yield
    outs = []
    for r in range(REP_B):
        acc = None
        for g in range(KV_B):
            hh = r * KV_B + g
            o = o_all[hh * t_len:(hh + 1) * t_len] * scales[hh]
            acc = o if acc is None else jnp.where(lane_grp == g, o, acc)
        outs.append(acc)
    return jnp.concatenate(outs, axis=1)


def _lane_tile(n):
    return slice(n * LANES, (n + 1) * LANES)


def _seq_major_perm(n_seq, n_step, transpose):
    m = n_seq * n_step
    row = lax.broadcasted_iota(jnp.int32, (m, m), 0)
    col = lax.broadcasted_iota(jnp.int32, (m, m), 1)
    if transpose:
        hit = row == (col % n_seq) * n_step + col // n_seq
    else:
        hit = col == (row % n_seq) * n_step + row // n_seq
    return jnp.where(hit, 1.0, 0.0).astype(BF16)


def _h_lanes(n):
    base = (n // SLAB_TILES) * 2 * SLAB_STATES + (n % SLAB_TILES) * LANES
    return slice(base, base + LANES), slice(base + SLAB_STATES, base + SLAB_STATES + LANES)


def _s5_drive(h_ref, z_ref, bcat_ref, n_seq, n_step):
    u = z_ref[:, O_UC:O_UC + W_C].astype(BF16)
    u = jnp.dot(_seq_major_perm(n_seq, n_step, False), u, preferred_element_type=F32).astype(BF16)
    for s in range(N_SLAB):
        h_ref[:, s * 2 * SLAB_STATES:(s + 1) * 2 * SLAB_STATES] = jnp.dot(
            u[:, _lane_tile(s)], bcat_ref[_lane_tile(s), :], preferred_element_type=F32)


def _s5_scan(h_ref, a_ref, n_seq, n_step, init):
    group = max(1, SCAN_VREGS * SUBLANES // n_seq)
    finals = [None] * N_TILE
    for n0 in range(0, N_TILE, group):
        tiles = range(n0, n0 + group)
        state = {n: init(n) for n in tiles}
        for j in range(n_step):
            rows = slice(j * n_seq, (j + 1) * n_seq)
            for n in tiles:
                re_l, im_l = _h_lanes(n)
                ar, ai = a_ref[0:1, _lane_tile(n)], a_ref[1:2, _lane_tile(n)]
                hr, hi = state[n]
                hr, hi = (ar * hr - ai * hi + h_ref[rows, re_l],
                          ar * hi + ai * hr + h_ref[rows, im_l])
                h_ref[rows, re_l] = hr
                h_ref[rows, im_l] = hi
                state[n] = (hr, hi)
        for n in tiles:
            finals[n] = state[n]
    return finals


def _s5_readout(h_ref, z_ref, ccat_ref, d_ref, wglu_ref, bglu_ref, n_seq, n_step):
    y = jnp.concatenate([
        lax.dot_general(h_ref[:, s * 2 * SLAB_STATES:(s + 1) * 2 * SLAB_STATES].astype(BF16),
                        ccat_ref[_lane_tile(s), :], _NT, preferred_element_type=F32)
        for s in range(N_SLAB)], axis=1)
    y_hi = y.astype(BF16)
    y_lo = (y - y_hi.astype(F32)).astype(BF16)
    back = _seq_major_perm(n_seq, n_step, True)
    y = (jnp.dot(back, y_hi, preferred_element_type=F32)
         + jnp.dot(back, y_lo, preferred_element_type=F32))
    y = jax.nn.gelu(y + d_ref[...] * z_ref[:, O_UC:O_UC + W_C])
    gate = jnp.dot(y.astype(BF16), wglu_ref[...], preferred_element_type=F32) + bglu_ref[...]
    y = y * jax.nn.sigmoid(gate)
    return y * jax.nn.silu(z_ref[:, O_GC:O_GC + W_C])


def _mix_block_steps(i, sinks_ref, z_ref, rope_ref, ws_ref, bias_ref, a_ref, pwr_ref, pwi_ref,
                     bcat_ref, ccat_ref, d_ref, wglu_ref, bglu_ref,
                     mixed_ref, klast_ref, vlast_ref, hfin_ref,
                     prev_kv, h_ref, hin_ref, carry_ref, layer):
    r_i = lax.broadcasted_iota(jnp.int32, (CHUNK, CHUNK), 0)
    c_i = lax.broadcasted_iota(jnp.int32, (CHUNK, CHUNK), 1)
    causal_w = lambda h: jnp.where(c_i <= r_i, ws_ref[h], 0.0).astype(BF16)
    mixed_ref[:, 0:W_A] = _chunk_mlp(causal_w, bias_ref, z_ref).astype(BF16)
    yield VPU_RUN_NEXT

    q = _rope(z_ref[:, O_Q:O_Q + W_B], rope_ref) * (HD_B ** -0.5)
    k = _rope(z_ref[:, O_K:O_K + KV_W], rope_ref)
    v = z_ref[:, O_V:O_V + KV_W]
    klast_ref[...] = k
    vlast_ref[...] = v
    kb, vb = k.astype(BF16), v.astype(BF16)
    kcat = jnp.concatenate([prev_kv[0], kb], axis=0)
    vcat = jnp.concatenate([prev_kv[1], vb], axis=0)
    prev_off = jnp.where(i > 0, 0, 2 * WINDOW)
    sink = lambda g, r: sinks_ref[layer, g * REP_B + r]
    yield
    o_b = yield from _attention(q, kcat, vcat, sink, prev_off)
    mixed_ref[:, W_A:W_A + W_B] = (o_b * jax.nn.silu(z_ref[:, O_GB:O_GB + W_B])).astype(BF16)
    yield

    _s5_drive(h_ref, z_ref, bcat_ref, SUBLANES, SUB_T)
    yield VPU_RUN_NEXT
    zero = jnp.zeros((SUBLANES, LANES), F32)
    finals = _s5_scan(h_ref, a_ref, SUBLANES, SUB_T, lambda n: (zero, zero))
    yield
    for n in range(N_TILE):
        st_l = _lane_tile(n)
        re_l, im_l = _h_lanes(n)
        end_r, end_i = finals[n]
        a_t_r, a_t_i = pwr_ref[SUB_T - 1:SUB_T, st_l], pwi_ref[SUB_T - 1:SUB_T, st_l]
        cr, ci = carry_ref[0:1, st_l], carry_ref[1:2, st_l]
        for r in range(SUBLANES):
            hin_ref[r:r + 1, re_l] = cr
            hin_ref[r:r + 1, im_l] = ci
            cr, ci = (end_r[r:r + 1] + a_t_r * cr - a_t_i * ci,
                      end_i[r:r + 1] + a_t_r * ci + a_t_i * cr)
        carry_ref[0:1, st_l] = cr
        carry_ref[1:2, st_l] = ci
        in_r, in_i = hin_ref[:, re_l], hin_ref[:, im_l]
        for j in range(SUB_T):
            rows = slice(j * SUBLANES, (j + 1) * SUBLANES)
            p_r, p_i = pwr_ref[j:j + 1, st_l], pwi_ref[j:j + 1, st_l]
            h_ref[rows, re_l] = h_ref[rows, re_l] + (p_r * in_r - p_i * in_i)
            h_ref[rows, im_l] = h_ref[rows, im_l] + (p_r * in_i + p_i * in_r)
        if n == N_TILE // 2 - 1:
            yield VPU_RUN_NEXT
    yield
    hfin_ref[...] = carry_ref[...]
    mixed_ref[:, W_A + W_B:] = _s5_readout(
        h_ref, z_ref, ccat_ref, d_ref, wglu_ref, bglu_ref, SUBLANES, SUB_T).astype(BF16)
    return kb, vb


def _prompt_layer_kernel(*refs, layer, halves_per_seq, n_half, prep_next):
    refs = list(refs)
    (sinks_ref, x_ref, xs_ref, g_ref, w_ref, rope_ref, ws_ref, bias_ref, a_ref, pwr_ref, pwi_ref,
     bcat_ref, ccat_ref, d_ref, wglu_ref, bglu_ref) = refs[:16]
    del refs[:16]
    wn_ref = refs.pop(0) if prep_next else None
    mixed_ref, klast_ref, vlast_ref, hfin_ref, zs_ref = refs[:5]
    del refs[:5]
    wnb_ref = refs.pop(0) if prep_next else None
    z_a, z_b, kprev_ref, vprev_ref, h_ref, hin_ref, carry_ref, zs_sem = refs
    s = pl.program_id(0)
    half = jnp.maximum(s - 1, 0) % halves_per_seq

    @pl.when(s == 0)
    def _():
        z_b[...] = jnp.zeros_like(z_b)

    @pl.when(half == 0)
    def _():
        kprev_ref[...] = jnp.zeros_like(kprev_ref)
        vprev_ref[...] = jnp.zeros_like(vprev_ref)
        carry_ref[...] = jnp.zeros_like(carry_ref)

    def step(z_new, z_cur):
        if prep_next:
            _w_in_prep_kernel(wn_ref, wnb_ref)
        x = jnp.where(s == n_half, xs_ref[...], x_ref[...])
        xn = _rms(x, g_ref[...]).astype(BF16)
        starts = list(range(0, D_IN, PROJ_COLS))

        def project():
            if starts:
                n0 = starts.pop(0)
                z_new[:, n0:n0 + PROJ_COLS] = jnp.dot(
                    xn, w_ref[:, n0:n0 + PROJ_COLS], preferred_element_type=F32)

        prev_kv = (kprev_ref[...], vprev_ref[...])
        for sb in range(FUSE_ROWS // CHUNK):
            rows = pl.ds(sb * CHUNK, CHUNK)
            steps = _mix_block_steps(
                half * (FUSE_ROWS // CHUNK) + sb, sinks_ref, z_cur.at[rows],
                rope_ref.at[:, rows], ws_ref, bias_ref, a_ref, pwr_ref, pwi_ref, bcat_ref,
                ccat_ref, d_ref, wglu_ref, bglu_ref, mixed_ref.at[rows], klast_ref,
                vlast_ref, hfin_ref, prev_kv, h_ref, hin_ref, carry_ref, layer)
            while True:
                try:
                    vpu_run_next = next(steps)
                except StopIteration as done:
                    prev_kv = done.value
                    break
                if vpu_run_next:
                    project()
        kprev_ref[...], vprev_ref[...] = prev_kv
        while starts:
            project()

    pl.when(s % 2 == 0)(lambda: step(z_a, z_b))
    pl.when(s % 2 == 1)(lambda: step(z_b, z_a))

    @pl.when(s == n_half)
    def _():
        copy = pltpu.make_async_copy((z_a, z_b)[n_half % 2], zs_ref, zs_sem)
        copy.start()
        copy.wait()


def _prompt_layer(x2d, xs2d, seq, layer, g_all, w_b, w_in_f32, sinks, rope_tab, ws, bias_t, a_rows,
                  pw_re, pw_im, bcat, ccat, d, wglu, bglu):
    m = x2d.shape[0]
    assert xs2d.shape == (FUSE_ROWS, D_MODEL), xs2d.shape
    bsz = m // seq
    n_half = m // FUSE_ROWS
    halves_per_seq = seq // FUSE_ROWS
    prep_next = w_in_f32 is not None
    slab = D_MODEL // n_half
    assert slab * n_half == D_MODEL and slab % (2 * SUBLANES) == 0, (n_half, slab)
    per_layer = lambda *shape, **kw: _layer_spec(shape, layer, **kw)
    once = dict(pipeline_mode=pl.Buffered(1))
    done = lambda s: jnp.maximum(s - 1, 0)
    last = lambda s: jnp.minimum(s, n_half - 1)
    next_in = [pl.BlockSpec((None, slab, D_IN), lambda s: (layer + 1, last(s), 0))] * prep_next
    next_out = [pl.BlockSpec((slab, D_IN), lambda s: (last(s), 0))] * prep_next
    next_shape = [jax.ShapeDtypeStruct((D_MODEL, D_IN), BF16)] * prep_next
    return pl.pallas_call(
        functools.partial(_prompt_layer_kernel, layer=layer, halves_per_seq=halves_per_seq,
                          n_half=n_half, prep_next=prep_next),
        grid=(n_half + 1,),
        in_specs=[
            pl.BlockSpec(memory_space=pltpu.SMEM),
            pl.BlockSpec((FUSE_ROWS, D_MODEL), lambda s: (last(s), 0)),
            pl.BlockSpec((FUSE_ROWS, D_MODEL), lambda s: (0, 0), **once),
            per_layer(1, D_MODEL),
            pl.BlockSpec((D_MODEL, D_IN), lambda s: (0, 0), **once),
            pl.BlockSpec((3, FUSE_ROWS, LANES), lambda s: (0, done(s) % halves_per_seq, 0)),
            per_layer(H_A, CHUNK, CHUNK),
            per_layer(CHUNK, H_A),
            per_layer(2, N_STATE),
            per_layer(SUB_T, N_STATE),
            per_layer(SUB_T, N_STATE),
            per_layer(W_C, 2 * SLAB_STATES, **once),
            per_layer(W_C, 2 * SLAB_STATES, **once),
            per_layer(1, W_C),
            per_layer(W_C, W_C, **once),
            per_layer(1, W_C),
        ] + next_in,
        out_specs=[
            pl.BlockSpec((FUSE_ROWS, D_MODEL), lambda s: (done(s), 0)),
            pl.BlockSpec((None, WINDOW, KV_W), lambda s: (done(s) // halves_per_seq, 0, 0)),
            pl.BlockSpec((None, WINDOW, KV_W), lambda s: (done(s) // halves_per_seq, 0, 0)),
            pl.BlockSpec((None, 2, N_STATE), lambda s: (done(s) // halves_per_seq, 0, 0)),
            pl.BlockSpec(memory_space=pl.ANY),
        ] + next_out,
        out_shape=[
            jax.ShapeDtypeStruct((m, D_MODEL), BF16),
            jax.ShapeDtypeStruct((bsz, WINDOW, KV_W), F32),
            jax.ShapeDtypeStruct((bsz, WINDOW, KV_W), F32),
            jax.ShapeDtypeStruct((bsz, 2, N_STATE), F32),
            jax.ShapeDtypeStruct((FUSE_ROWS, D_IN), F32),
        ] + next_shape,
        scratch_shapes=[
            pltpu.VMEM((FUSE_ROWS, D_IN), F32),
            pltpu.VMEM((FUSE_ROWS, D_IN), F32),
            pltpu.VMEM((WINDOW, KV_W), BF16),
            pltpu.VMEM((WINDOW, KV_W), BF16),
            pltpu.VMEM((CHUNK, 2 * N_STATE), F32),
            pltpu.VMEM((SUBLANES, 2 * N_STATE), F32),
            pltpu.VMEM((2, N_STATE), F32),
            pltpu.SemaphoreType.DMA(()),
        ],
        compiler_params=pltpu.CompilerParams(
            dimension_semantics=("arbitrary",), vmem_limit_bytes=VMEM_LIMIT),
        name="prompt_layer",
    )(sinks, x2d, xs2d, g_all, w_b, rope_tab, ws, bias_t, a_rows, pw_re, pw_im, bcat, ccat, d,
      wglu, bglu, *([w_in_f32] * prep_next))


def _mix_sample_kernel(sinks_ref, z_ref, rope_ref, ck_ref, cv_ref, ws_ref, bias_ref, a_ref,
                       h0r_ref, h0i_ref, bcat_ref, ccat_ref, d_ref, wglu_ref, bglu_ref,
                       mixed_ref, knew_ref, hre_ref, him_ref,
                       q_ref, h_ref, *, layer, n_batch, n_tok):
    b = pl.program_id(0)
    m = n_batch * n_tok

    @pl.when(b == 0)
    def _():
        r_i = lax.broadcasted_iota(jnp.int32, (m, m), 0)
        c_i = lax.broadcasted_iota(jnp.int32, (m, m), 1)
        keep = (r_i // n_tok == c_i // n_tok) & (c_i <= r_i)
        spread = jnp.where(lax.broadcasted_iota(jnp.int32, (m, CHUNK), 0) % n_tok
                           == lax.broadcasted_iota(jnp.int32, (m, CHUNK), 1), 1.0, 0.0).astype(BF16)
        spread_t = jnp.where(lax.broadcasted_iota(jnp.int32, (CHUNK, m), 1) % n_tok
                             == lax.broadcasted_iota(jnp.int32, (CHUNK, m), 0), 1.0, 0.0).astype(BF16)

        def tiled_w(h):
            rows = jnp.dot(spread, ws_ref[h].astype(BF16), preferred_element_type=F32)
            full = jnp.dot(rows.astype(BF16), spread_t, preferred_element_type=F32)
            return jnp.where(keep, full, 0.0).astype(BF16)

        mixed_ref[:, 0:W_A] = _chunk_mlp(tiled_w, bias_ref, z_ref)
        q_ref[...] = _rope(z_ref[:, O_Q:O_Q + W_B], rope_ref) * (HD_B ** -0.5)
        knew_ref[...] = _rope(z_ref[:, O_K:O_K + KV_W], rope_ref)
        _s5_drive(h_ref, z_ref, bcat_ref, n_batch, n_tok)
        finals = _s5_scan(h_ref, a_ref, n_batch, n_tok,
                          lambda n: (h0r_ref[:, _lane_tile(n)], h0i_ref[:, _lane_tile(n)]))
        for n in range(N_TILE):
            hre_ref[:, _lane_tile(n)] = finals[n][0]
            him_ref[:, _lane_tile(n)] = finals[n][1]
        mixed_ref[:, W_A + W_B:] = _s5_readout(
            h_ref, z_ref, ccat_ref, d_ref, wglu_ref, bglu_ref, n_batch, n_tok)

    pad = jnp.zeros((WINDOW - n_tok, KV_W), F32)
    sink = lambda g, r: sinks_ref[layer, g * REP_B + r]
    row_sets, running = [], []
    for j in range(SAMPLE_GROUP):
        rows = pl.ds(pl.multiple_of((b * SAMPLE_GROUP + j) * n_tok, n_tok), n_tok)
        kcat = jnp.concatenate([ck_ref[j], knew_ref[rows, :], pad], axis=0).astype(BF16)
        vcat = jnp.concatenate([cv_ref[j], z_ref[rows, O_V:O_V + KV_W], pad], axis=0).astype(BF16)
        row_sets.append(rows)
        running.append(_attention(q_ref[rows, :], kcat, vcat, sink, 0))
    results = [None] * SAMPLE_GROUP
    while any(r is None for r in results):
        for j, steps in enumerate(running):
            if results[j] is None:
                try:
                    next(steps)
                except StopIteration as done:
                    results[j] = done.value
    for rows, o_b in zip(row_sets, results):
        mixed_ref[rows, W_A:W_A + W_B] = o_b * jax.nn.silu(z_ref[rows, O_GB:O_GB + W_B])


def _mix_sample(z2, layer, sinks, rope_tab, ck, cv, ws, bias_t, a_rows, h0r, h0i,
                bcat, ccat, d, wglu, bglu, n_batch, n_tok):
    m = n_batch * n_tok
    const = lambda *shape: pl.BlockSpec(shape, lambda b: (0,) * len(shape))
    per_layer = lambda *shape: _layer_spec(shape, layer)
    return pl.pallas_call(
        functools.partial(_mix_sample_kernel, layer=layer, n_batch=n_batch, n_tok=n_tok),
        grid=(n_batch // SAMPLE_GROUP,),
        in_specs=[
            pl.BlockSpec(memory_space=pltpu.SMEM),
            const(m, D_IN),
            const(3, m, LANES),
            pl.BlockSpec((None, SAMPLE_GROUP, WINDOW, KV_W), lambda b: (layer, b, 0, 0)),
            pl.BlockSpec((None, SAMPLE_GROUP, WINDOW, KV_W), lambda b: (layer, b, 0, 0)),
            per_layer(H_A, CHUNK, CHUNK),
            per_layer(m, H_A),
            per_layer(2, N_STATE),
            per_layer(n_batch, N_STATE),
            per_layer(n_batch, N_STATE),
            per_layer(W_C, 2 * SLAB_STATES),
            per_layer(W_C, 2 * SLAB_STATES),
            per_layer(1, W_C),
            per_layer(W_C, W_C),
            per_layer(1, W_C),
        ],
        out_specs=[
            const(m, D_MODEL),
            const(m, KV_W),
            const(n_batch, N_STATE),
            const(n_batch, N_STATE),
        ],
        out_shape=[
            jax.ShapeDtypeStruct((m, D_MODEL), F32),
            jax.ShapeDtypeStruct((m, KV_W), F32),
            jax.ShapeDtypeStruct((n_batch, N_STATE), F32),
            jax.ShapeDtypeStruct((n_batch, N_STATE), F32),
        ],
        scratch_shapes=[
            pltpu.VMEM((m, W_B), F32),
            pltpu.VMEM((m, 2 * N_STATE), F32),
        ],
        compiler_params=pltpu.CompilerParams(
            dimension_semantics=("arbitrary",), vmem_limit_bytes=VMEM_LIMIT),
        name="mix_sample",
    )(sinks, z2, rope_tab, ck, cv, ws, bias_t, a_rows, h0r, h0i, bcat, ccat, d, wglu, bglu)


def _rope_table(pos):
    half = ROPE_DIM // 2
    inv = ROPE_THETA ** (-jnp.arange(half, dtype=F32) * 2.0 / ROPE_DIM)
    ang = pos[:, None] * inv[None, :]
    cos, sin = jnp.cos(ang), jnp.sin(ang)
    n = pos.shape[0]
    rest = HD_B - ROPE_DIM
    c = jnp.concatenate([cos, cos, jnp.ones((n, rest), F32)], axis=1)
    s_lo = jnp.concatenate([jnp.zeros((n, half), F32), sin, jnp.zeros((n, rest), F32)], axis=1)
    s_hi = jnp.concatenate([-sin, jnp.zeros((n, half + rest), F32)], axis=1)
    tab = jnp.stack([c, s_lo, s_hi])
    return jnp.concatenate([tab] * (LANES // HD_B), axis=2)


def kernel(x_prompt, x_sample, cache_swa_k, cache_swa_v, state_ssm_re, state_ssm_im, norm_g,
           final_norm_g, w_in, w_out, chunk_w_s, chunk_b_s, attn_sinks, ssm_a_re, ssm_a_im,
           ssm_log_dt, ssm_b_re, ssm_b_im, ssm_c_re, ssm_c_im, ssm_d, glu_w, glu_b):
    bsz, seq, _ = x_prompt.shape
    n_batch, n_tok, _ = x_sample.shape
    m_s = n_batch * n_tok

    w_in_b, w_out_b = _prep_first_weights(w_in, w_out)
    wglu_b = glu_w.astype(BF16)

    ab_re, ab_im, pw_re, pw_im, bcat, ccat = _s5_prep(
        ssm_a_re, ssm_a_im, ssm_log_dt, ssm_b_re, ssm_b_im, ssm_c_re, ssm_c_im)
    a_rows = jnp.concatenate([ab_re.reshape(DEPTH, 1, N_STATE),
                              ab_im.reshape(DEPTH, 1, N_STATE)], axis=1)
    pw_re = jnp.transpose(pw_re, (0, 2, 1, 3)).reshape(DEPTH, SUB_T, N_STATE)
    pw_im = jnp.transpose(pw_im, (0, 2, 1, 3)).reshape(DEPTH, SUB_T, N_STATE)

    rope_p = _rope_table(jnp.arange(seq, dtype=F32))
    rope_s = jnp.tile(_rope_table(jnp.arange(n_tok, dtype=F32) + PAST_LEN), (1, n_batch, 1))
    bias_p = jnp.transpose(chunk_b_s, (0, 2, 1))
    bias_s = jnp.tile(bias_p[:, :n_tok], (1, n_batch, 1))
    ck = cache_swa_k.reshape(DEPTH, n_batch, WINDOW, KV_W)
    cv = cache_swa_v.reshape(DEPTH, n_batch, WINDOW, KV_W)
    h0r = state_ssm_re.reshape(DEPTH, n_batch, N_STATE)
    h0i = state_ssm_im.reshape(DEPTH, n_batch, N_STATE)
    g_all = norm_g.reshape(DEPTH, 1, D_MODEL)
    d_all = ssm_d.reshape(DEPTH, 1, W_C)
    bglu_all = glu_b.reshape(DEPTH, 1, W_C)
    fg = final_norm_g.reshape(1, D_MODEL)

    xp = x_prompt.reshape(bsz * seq, D_MODEL)
    xs = x_sample.reshape(m_s, D_MODEL)
    outs = [[] for _ in range(9)]
    for l in range(DEPTH):
        more = l + 1 < DEPTH
        mixed_p, k_last, v_last, h_fin, zs, *w_in_next = _prompt_layer(
            xp, xs, seq, l, g_all, w_in_b, w_in if more else None, attn_sinks, rope_p, chunk_w_s,
            bias_p, a_rows, pw_re, pw_im, bcat, ccat, d_all, wglu_b, bglu_all)
        if more:
            xp, w_out_next = _out_proj(mixed_p, xp, w_out_b, fg, l, w_out)
        else:
            xp = _out_proj(mixed_p, xp, w_out_b, fg, l)

        mixed_s, k_new, h_re_s, h_im_s = _mix_sample(
            zs, l, attn_sinks, rope_s, ck, cv, chunk_w_s, bias_s, a_rows, h0r, h0i,
            bcat, ccat, d_all, wglu_b, bglu_all, n_batch, n_tok)
        xs = _out_proj(mixed_s, xs, w_out_b, fg, l)
        if more:
            w_in_b, w_out_b = w_in_next[0], w_out_next

        outs[0].append(k_last.reshape(bsz, WINDOW, KV_B, HD_B))
        outs[1].append(v_last.reshape(bsz, WINDOW, KV_B, HD_B))
        outs[2].append(k_new.reshape(n_batch, n_tok, KV_B, HD_B))
        outs[3].append(zs[:, O_V:O_V + KV_W].reshape(n_batch, n_tok, KV_B, HD_B))
        outs[4].append(h_fin[:, 0].reshape(bsz, G_C, P_C))
        outs[5].append(h_fin[:, 1].reshape(bsz, G_C, P_C))
        outs[6].append(h_re_s.reshape(n_batch, G_C, P_C))
        outs[7].append(h_im_s.reshape(n_batch, G_C, P_C))
        outs[8].append(zs[:, O_VA:O_VA + W_A].reshape(n_batch, n_tok, W_A))

    return (xp.reshape(bsz, seq, D_MODEL), xs.reshape(n_batch, n_tok, D_MODEL),
            *[jnp.stack(o) for o in outs])
```

```python
import functools

import jax
import jax.numpy as jnp
from jax import lax
from jax.experimental import pallas as pl
from jax.experimental.pallas import tpu as pltpu

F32 = jnp.float32
BF16 = jnp.bfloat16

D_MODEL = 2048
DEPTH = 4
PAST_LEN = 16384
CHUNK = 128
W_A = 512
H_A = 4
C_A = W_A // H_A
W_B = 1024
HD_B = 64
H_B = W_B // HD_B
KV_B = 4
REP_B = H_B // KV_B
KV_W = KV_B * HD_B
WINDOW = 128
ROPE_DIM = HD_B // 4
ROPE_THETA = 500000.0
W_C = 512
GC = 16
G_C = W_C // GC
P_C = 64
N_STATE = G_C * P_C
D_IN = 3 * W_A + 2 * W_B + 2 * KV_W + 2 * W_C
EPS = 1e-5
NEG = -1e30

O_UA, O_VA, O_GA = 0, W_A, 2 * W_A
O_Q = 3 * W_A
O_K = O_Q + W_B
O_V = O_K + KV_W
O_GB = O_V + KV_W
O_UC = O_GB + W_B
O_GC = O_UC + W_C

LANES = 128
SUBLANES = 8
N_TILE = N_STATE // LANES
N_SLAB = W_C // LANES
SLAB_GROUPS = G_C // N_SLAB
SLAB_STATES = SLAB_GROUPS * P_C
SLAB_TILES = SLAB_STATES // LANES
VMEM_LIMIT = 60 * 1024 * 1024

OUT_ROWS = 1024
PROJ_COLS = 512
FUSE_ROWS = 256
SAMPLE_GROUP = 8
VPU_RUN_NEXT = True
W_PREP_ROWS = 256
W_PREP_COLS = 512
SCAN_VREGS = 4
SUB_T = CHUNK // SUBLANES

_NT = (((1,), (1,)), ((), ()))


def _rms(x, g):
    return x * lax.rsqrt(jnp.mean(x * x, axis=-1, keepdims=True) + EPS) * g


def _layer_spec(shape, layer, **kw):
    zeros = (0,) * len(shape)
    return pl.BlockSpec((None,) + tuple(shape), lambda *_: (layer,) + zeros, **kw)


def _out_proj_kernel(m_ref, x_ref, w_ref, fg_ref, *rest, final, prep_next):
    if prep_next:
        *wn_refs, y_ref, wnb_ref = rest
        for j, wn_ref in enumerate(wn_refs):
            wnb_ref[j * HD_B:(j + 1) * HD_B, :] = wn_ref[...].astype(BF16)
    else:
        y_ref, = rest
    y = x_ref[...] + jnp.dot(m_ref[...].astype(BF16), w_ref[...], preferred_element_type=F32)
    if final:
        y = _rms(y, fg_ref[...])
    y_ref[...] = y


def _w_out_source_block(i):
    first, last = W_A // HD_B, (W_A + W_B) // HD_B
    head = i - first
    src = first + (head % KV_B) * REP_B + head // KV_B
    return jnp.where((i >= first) & (i < last), src, i)


def _out_proj(mixed, x2d, w_b, fg, layer, w_out_f32=None):
    m = x2d.shape[0]
    tm = min(OUT_ROWS, m)
    final = layer == DEPTH - 1
    prep_next = w_out_f32 is not None
    in_specs = [
        pl.BlockSpec((tm, D_MODEL), lambda i: (i, 0)),
        pl.BlockSpec((tm, D_MODEL), lambda i: (i, 0)),
        pl.BlockSpec((D_MODEL, D_MODEL), lambda i: (0, 0), pipeline_mode=pl.Buffered(1)),
        pl.BlockSpec((1, D_MODEL), lambda i: (0, 0)),
    ]
    out_specs = [pl.BlockSpec((tm, D_MODEL), lambda i: (i, 0))]
    out_shape = [jax.ShapeDtypeStruct((m, D_MODEL), F32)]
    args = [mixed, x2d, w_b, fg]
    if prep_next:
        per_step = (D_MODEL // HD_B) // (m // tm)
        assert per_step * (m // tm) * HD_B == D_MODEL, (m, tm)
        for j in range(per_step):
            in_specs.append(pl.BlockSpec(
                (None, HD_B, D_MODEL),
                lambda i, j=j: (layer + 1, _w_out_source_block(i * per_step + j), 0)))
            args.append(w_out_f32)
        out_specs.append(pl.BlockSpec((per_step * HD_B, D_MODEL), lambda i: (i, 0)))
        out_shape.append(jax.ShapeDtypeStruct((D_MODEL, D_MODEL), BF16))
    out = pl.pallas_call(
        functools.partial(_out_proj_kernel, final=final, prep_next=prep_next),
        grid=(m // tm,),
        in_specs=in_specs,
        out_specs=out_specs,
        out_shape=out_shape,
        compiler_params=pltpu.CompilerParams(
            dimension_semantics=("arbitrary",), vmem_limit_bytes=VMEM_LIMIT),
        name="out_proj_final" if final else "out_proj",
    )(*args)
    return out if prep_next else out[0]


def _rep_major_source(new_head):
    r, g = divmod(new_head, KV_B)
    return g * REP_B + r


def _w_in_prep_kernel(w_ref, o_ref):
    heads_per_tile = LANES // HD_B
    lane_lo = lax.broadcasted_iota(jnp.int32, (w_ref.shape[0], LANES), 1) < HD_B

    def half(sec, head, want_hi):
        t = w_ref[:, sec + (head // heads_per_tile) * LANES:sec + (head // heads_per_tile + 1) * LANES]
        return t if (head % heads_per_tile == 1) == want_hi else pltpu.roll(t, HD_B, 1)

    for c0, c1, permute in ((0, O_Q, False), (O_Q, O_K, True), (O_K, O_GB, False),
                            (O_GB, O_UC, True), (O_UC, D_IN, False)):
        if not permute:
            o_ref[:, c0:c1] = w_ref[:, c0:c1].astype(BF16)
            continue
        for j in range(W_B // LANES):
            lo = half(c0, _rep_major_source(heads_per_tile * j), False)
            hi = half(c0, _rep_major_source(heads_per_tile * j + 1), True)
            o_ref[:, c0 + j * LANES:c0 + (j + 1) * LANES] = jnp.where(lane_lo, lo, hi).astype(BF16)


def _w_out_prep_kernel(w_ref, o_ref):
    o_ref[0:W_A, :] = w_ref[0:W_A, :].astype(BF16)
    for new_head in range(H_B):
        src = W_A + _rep_major_source(new_head) * HD_B
        dst = W_A + new_head * HD_B
        o_ref[dst:dst + HD_B, :] = w_ref[src:src + HD_B, :].astype(BF16)
    o_ref[W_A + W_B:, :] = w_ref[W_A + W_B:, :].astype(BF16)


def _prep_first_weights(w_in, w_out):
    w_in_b = pl.pallas_call(
        _w_in_prep_kernel,
        grid=(D_MODEL // W_PREP_ROWS,),
        in_specs=[pl.BlockSpec((None, W_PREP_ROWS, D_IN), lambda i: (0, i, 0))],
        out_specs=pl.BlockSpec((W_PREP_ROWS, D_IN), lambda i: (i, 0)),
        out_shape=jax.ShapeDtypeStruct((D_MODEL, D_IN), BF16),
        compiler_params=pltpu.CompilerParams(
            dimension_semantics=("arbitrary",), vmem_limit_bytes=VMEM_LIMIT),
        name="w_in_prep",
    )(w_in)
    w_out_b = pl.pallas_call(
        _w_out_prep_kernel,
        grid=(D_MODEL // W_PREP_COLS,),
        in_specs=[pl.BlockSpec((None, D_MODEL, W_PREP_COLS), lambda i: (0, 0, i))],
        out_specs=pl.BlockSpec((D_MODEL, W_PREP_COLS), lambda i: (0, i)),
        out_shape=jax.ShapeDtypeStruct((D_MODEL, D_MODEL), BF16),
        compiler_params=pltpu.CompilerParams(
            dimension_semantics=("arbitrary",), vmem_limit_bytes=VMEM_LIMIT),
        name="w_out_prep",
    )(w_out)
    return w_in_b, w_out_b


def _store_block_diag(out_ref, lane0, x3):
    per_tile = LANES // P_C
    lane_slot = lax.broadcasted_iota(jnp.int32, (GC, LANES), 1) // P_C
    for g in range(G_C):
        xg = x3[g]
        pair = jnp.concatenate([xg] * per_tile, axis=1)
        tile = jnp.where(lane_slot == g % per_tile, pair, 0.0)
        l0 = lane0 + ((g % SLAB_GROUPS) // per_tile) * LANES
        out_ref[g * GC:(g + 1) * GC, l0:l0 + LANES] = tile.astype(BF16)


def _s5_prep_kernel(are_ref, aim_ref, ldt_ref, btr_ref, bti_ref, cre_ref, cim_ref,
                    abr_ref, abi_ref, pwr_ref, pwi_ref, bcat_ref, ccat_ref):
    a_re, a_im = are_ref[...], aim_ref[...]
    dt = jnp.exp(ldt_ref[...])
    mag = jnp.exp(a_re * dt)
    ab_re = mag * jnp.cos(a_im * dt)
    ab_im = mag * jnp.sin(a_im * dt)
    nr, ni = ab_re - 1.0, ab_im
    den = a_re * a_re + a_im * a_im
    f_re = (nr * a_re + ni * a_im) / den
    f_im = (ni * a_re - nr * a_im) / den
    br, bi = btr_ref[...], bti_ref[...]
    bcat_ref[...] = jnp.zeros_like(bcat_ref)
    ccat_ref[...] = jnp.zeros_like(ccat_ref)
    _store_block_diag(bcat_ref, 0, f_re * br - f_im * bi)
    _store_block_diag(bcat_ref, SLAB_STATES, f_re * bi + f_im * br)
    _store_block_diag(ccat_ref, 0, cre_ref[...])
    _store_block_diag(ccat_ref, SLAB_STATES, -cim_ref[...])
    abr_ref[...] = ab_re
    abi_ref[...] = ab_im
    p_re, p_im = ab_re, ab_im
    for j in range(SUB_T):
        pwr_ref[:, j:j + 1, :] = p_re
        pwi_ref[:, j:j + 1, :] = p_im
        p_re, p_im = p_re * ab_re - p_im * ab_im, p_re * ab_im + p_im * ab_re


def _s5_prep(a_re, a_im, log_dt, b_re, b_im, c_re, c_im):
    a4 = lambda a: a.reshape(DEPTH, G_C, 1, P_C)
    bt = lambda b: jnp.transpose(b, (0, 1, 3, 2))
    vec = pl.BlockSpec((None, G_C, 1, P_C), lambda l: (l, 0, 0, 0))
    dts = pl.BlockSpec((None, G_C, 1, 1), lambda l: (l, 0, 0, 0))
    mat = pl.BlockSpec((None, G_C, GC, P_C), lambda l: (l, 0, 0, 0))
    pws = pl.BlockSpec((None, G_C, SUB_T, P_C), lambda l: (l, 0, 0, 0))
    dense = pl.BlockSpec((None, W_C, 2 * SLAB_STATES), lambda l: (l, 0, 0))
    vshape = jax.ShapeDtypeStruct((DEPTH, G_C, 1, P_C), F32)
    pshape = jax.ShapeDtypeStruct((DEPTH, G_C, SUB_T, P_C), F32)
    dshape = jax.ShapeDtypeStruct((DEPTH, W_C, 2 * SLAB_STATES), BF16)
    return pl.pallas_call(
        _s5_prep_kernel,
        grid=(DEPTH,),
        in_specs=[vec, vec, dts, mat, mat, mat, mat],
        out_specs=[vec, vec, pws, pws, dense, dense],
        out_shape=[vshape, vshape, pshape, pshape, dshape, dshape],
        name="s5_prep",
    )(a4(a_re), a4(a_im), log_dt.reshape(DEPTH, G_C, 1, 1), bt(b_re), bt(b_im), c_re, c_im)


def _rope(x, tab_ref):
    c, s_lo, s_hi = tab_ref[0], tab_ref[1], tab_ref[2]
    half = ROPE_DIM // 2
    tiles = []
    for j in range(x.shape[1] // LANES):
        t = x[:, j * LANES:(j + 1) * LANES]
        tiles.append(t * c + pltpu.roll(t, half, 1) * s_lo + pltpu.roll(t, LANES - half, 1) * s_hi)
    return jnp.concatenate(tiles, axis=1) if len(tiles) > 1 else tiles[0]


def _chunk_mlp(mix_w, bias_ref, z_ref):
    outs = []
    for h in range(H_A):
        v = z_ref[:, O_VA + h * C_A:O_VA + (h + 1) * C_A].astype(BF16)
        zz = jnp.dot(mix_w(h), v, preferred_element_type=F32) + bias_ref[:, h:h + 1]
        u = z_ref[:, O_UA + h * C_A:O_UA + (h + 1) * C_A]
        g = z_ref[:, O_GA + h * C_A:O_GA + (h + 1) * C_A]
        outs.append(u * zz * jax.nn.silu(g))
    return jnp.concatenate(outs, axis=1)


def _drain(steps):
    try:
        while True:
            next(steps)
    except StopIteration as done:
        return done.value


def _attention(q, kcat, vcat, sink, prev_off):
    t_len = q.shape[0]
    lane_grp = lax.broadcasted_iota(jnp.int32, (t_len, KV_W), 1) // HD_B
    pieces = []
    for r in range(REP_B):
        chunk = q[:, r * KV_W:(r + 1) * KV_W]
        for g in range(KV_B):
            pieces.append(jnp.where(lane_grp == g, chunk, 0.0))
    qbd = jnp.concatenate(pieces, axis=0).astype(BF16)
    s_all = lax.dot_general(qbd, kcat, _NT, preferred_element_type=F32)
    row = lax.broadcasted_iota(jnp.int32, (t_len, 2 * WINDOW), 0)
    col = lax.broadcasted_iota(jnp.int32, (t_len, 2 * WINDOW), 1)
    visible = jnp.where(col < WINDOW, col - row - prev_off, row - col + WINDOW + 1) > 0
    heads = [(r, g) for r in range(REP_B) for g in range(KV_B)]
    masked = lambda hh: jnp.where(visible, s_all[hh * t_len:(hh + 1) * t_len], NEG)
    yield VPU_RUN_NEXT
    tops = [jnp.maximum(jnp.max(masked(hh), axis=-1, keepdims=True), sink(g, r))
            for hh, (r, g) in enumerate(heads)]
    yield
    probs, scales = [], []
    for hh, (r, g) in enumerate(heads):
        p = jnp.exp(masked(hh) - tops[hh])
        den = jnp.sum(p, axis=-1, keepdims=True) + jnp.exp(sink(g, r) - tops[hh])
        probs.append(p.astype(BF16))
        scales.append(1.0 / den)
        if hh == len(heads) // 2 - 1:
            yield VPU_RUN_NEXT
    yield
    p_all = jnp.concatenate(probs, axis=0)
    o_all = jnp.dot(p_all, vcat, preferred_element_type=F32)
    yield
    outs = []
    for r in range(REP_B):
        acc = None
        for g in range(KV_B):
            hh = r * KV_B + g
            o = o_all[hh * t_len:(hh + 1) * t_len] * scales[hh]
            acc = o if acc is None else jnp.where(lane_grp == g, o, acc)
        outs.append(acc)
    return jnp.concatenate(outs, axis=1)


def _lane_tile(n):
    return slice(n * LANES, (n + 1) * LANES)


def _seq_major_perm(n_seq, n_step, transpose):
    m = n_seq * n_step
    row = lax.broadcasted_iota(jnp.int32, (m, m), 0)
    col = lax.broadcasted_iota(jnp.int32, (m, m), 1)
    if transpose:
        hit = row == (col % n_seq) * n_step + col // n_seq
    else:
        hit = col == (row % n_seq) * n_step + row // n_seq
    return jnp.where(hit, 1.0, 0.0).astype(BF16)


def _h_lanes(n):
    base = (n // SLAB_TILES) * 2 * SLAB_STATES + (n % SLAB_TILES) * LANES
    return slice(base, base + LANES), slice(base + SLAB_STATES, base + SLAB_STATES + LANES)


def _s5_drive(h_ref, z_ref, bcat_ref, n_seq, n_step):
    u = z_ref[:, O_UC:O_UC + W_C].astype(BF16)
    u = jnp.dot(_seq_major_perm(n_seq, n_step, False), u, preferred_element_type=F32).astype(BF16)
    for s in range(N_SLAB):
        h_ref[:, s * 2 * SLAB_STATES:(s + 1) * 2 * SLAB_STATES] = jnp.dot(
            u[:, _lane_tile(s)], bcat_ref[_lane_tile(s), :], preferred_element_type=F32)


def _s5_scan(h_ref, a_ref, n_seq, n_step, init):
    group = max(1, SCAN_VREGS * SUBLANES // n_seq)
    finals = [None] * N_TILE
    for n0 in range(0, N_TILE, group):
        tiles = range(n0, n0 + group)
        state = {n: init(n) for n in tiles}
        for j in range(n_step):
            rows = slice(j * n_seq, (j + 1) * n_seq)
            for n in tiles:
                re_l, im_l = _h_lanes(n)
                ar, ai = a_ref[0:1, _lane_tile(n)], a_ref[1:2, _lane_tile(n)]
                hr, hi = state[n]
                hr, hi = (ar * hr - ai * hi + h_ref[rows, re_l],
                          ar * hi + ai * hr + h_ref[rows, im_l])
                h_ref[rows, re_l] = hr
                h_ref[rows, im_l] = hi
                state[n] = (hr, hi)
        for n in tiles:
            finals[n] = state[n]
    return finals


def _s5_readout(h_ref, z_ref, ccat_ref, d_ref, wglu_ref, bglu_ref, n_seq, n_step):
    y = jnp.concatenate([
        lax.dot_general(h_ref[:, s * 2 * SLAB_STATES:(s + 1) * 2 * SLAB_STATES].astype(BF16),
                        ccat_ref[_lane_tile(s), :], _NT, preferred_element_type=F32)
        for s in range(N_SLAB)], axis=1)
    y_hi = y.astype(BF16)
    y_lo = (y - y_hi.astype(F32)).astype(BF16)
    back = _seq_major_perm(n_seq, n_step, True)
    y = (jnp.dot(back, y_hi, preferred_element_type=F32)
         + jnp.dot(back, y_lo, preferred_element_type=F32))
    y = jax.nn.gelu(y + d_ref[...] * z_ref[:, O_UC:O_UC + W_C])
    gate = jnp.dot(y.astype(BF16), wglu_ref[...], preferred_element_type=F32) + bglu_ref[...]
    y = y * jax.nn.sigmoid(gate)
    return y * jax.nn.silu(z_ref[:, O_GC:O_GC + W_C])


def _mix_block_steps(i, sinks_ref, z_ref, rope_ref, ws_ref, bias_ref, a_ref, pwr_ref, pwi_ref,
                     bcat_ref, ccat_ref, d_ref, wglu_ref, bglu_ref,
                     mixed_ref, klast_ref, vlast_ref, hfin_ref,
                     prev_kv, h_ref, hin_ref, carry_ref, layer):
    r_i = lax.broadcasted_iota(jnp.int32, (CHUNK, CHUNK), 0)
    c_i = lax.broadcasted_iota(jnp.int32, (CHUNK, CHUNK), 1)
    causal_w = lambda h: jnp.where(c_i <= r_i, ws_ref[h], 0.0).astype(BF16)
    mixed_ref[:, 0:W_A] = _chunk_mlp(causal_w, bias_ref, z_ref).astype(BF16)
    yield VPU_RUN_NEXT

    q = _rope(z_ref[:, O_Q:O_Q + W_B], rope_ref) * (HD_B ** -0.5)
    k = _rope(z_ref[:, O_K:O_K + KV_W], rope_ref)
    v = z_ref[:, O_V:O_V + KV_W]
    klast_ref[...] = k
    vlast_ref[...] = v
    kb, vb = k.astype(BF16), v.astype(BF16)
    kcat = jnp.concatenate([prev_kv[0], kb], axis=0)
    vcat = jnp.concatenate([prev_kv[1], vb], axis=0)
    prev_off = jnp.where(i > 0, 0, 2 * WINDOW)
    sink = lambda g, r: sinks_ref[layer, g * REP_B + r]
    yield
    o_b = yield from _attention(q, kcat, vcat, sink, prev_off)
    mixed_ref[:, W_A:W_A + W_B] = (o_b * jax.nn.silu(z_ref[:, O_GB:O_GB + W_B])).astype(BF16)
    yield

    _s5_drive(h_ref, z_ref, bcat_ref, SUBLANES, SUB_T)
    yield VPU_RUN_NEXT
    zero = jnp.zeros((SUBLANES, LANES), F32)
    finals = _s5_scan(h_ref, a_ref, SUBLANES, SUB_T, lambda n: (zero, zero))
    yield
    for n in range(N_TILE):
        st_l = _lane_tile(n)
        re_l, im_l = _h_lanes(n)
        end_r, end_i = finals[n]
        a_t_r, a_t_i = pwr_ref[SUB_T - 1:SUB_T, st_l], pwi_ref[SUB_T - 1:SUB_T, st_l]
        cr, ci = carry_ref[0:1, st_l], carry_ref[1:2, st_l]
        for r in range(SUBLANES):
            hin_ref[r:r + 1, re_l] = cr
            hin_ref[r:r + 1, im_l] = ci
            cr, ci = (end_r[r:r + 1] + a_t_r * cr - a_t_i * ci,
                      end_i[r:r + 1] + a_t_r * ci + a_t_i * cr)
        carry_ref[0:1, st_l] = cr
        carry_ref[1:2, st_l] = ci
        in_r, in_i = hin_ref[:, re_l], hin_ref[:, im_l]
        for j in range(SUB_T):
            rows = slice(j * SUBLANES, (j + 1) * SUBLANES)
            p_r, p_i = pwr_ref[j:j + 1, st_l], pwi_ref[j:j + 1, st_l]
            h_ref[rows, re_l] = h_ref[rows, re_l] + (p_r * in_r - p_i * in_i)
            h_ref[rows, im_l] = h_ref[rows, im_l] + (p_r * in_i + p_i * in_r)
        if n == N_TILE // 2 - 1:
            yield VPU_RUN_NEXT
    yield
    hfin_ref[...] = carry_ref[...]
    mixed_ref[:, W_A + W_B:] = _s5_readout(
        h_ref, z_ref, ccat_ref, d_ref, wglu_ref, bglu_ref, SUBLANES, SUB_T).astype(BF16)
    return kb, vb


def _prompt_layer_kernel(*refs, layer, halves_per_seq, n_half, prep_next):
    refs = list(refs)
    (sinks_ref, x_ref, xs_ref, g_ref, w_ref, rope_ref, ws_ref, bias_ref, a_ref, pwr_ref, pwi_ref,
     bcat_ref, ccat_ref, d_ref, wglu_ref, bglu_ref) = refs[:16]
    del refs[:16]
    wn_ref = refs.pop(0) if prep_next else None
    mixed_ref, klast_ref, vlast_ref, hfin_ref, zs_ref = refs[:5]
    del refs[:5]
    wnb_ref = refs.pop(0) if prep_next else None
    z_a, z_b, kprev_ref, vprev_ref, h_ref, hin_ref, carry_ref, zs_sem = refs
    s = pl.program_id(0)
    half = jnp.maximum(s - 1, 0) % halves_per_seq

    @pl.when(s == 0)
    def _():
        z_b[...] = jnp.zeros_like(z_b)

    @pl.when(half == 0)
    def _():
        kprev_ref[...] = jnp.zeros_like(kprev_ref)
        vprev_ref[...] = jnp.zeros_like(vprev_ref)
        carry_ref[...] = jnp.zeros_like(carry_ref)

    def step(z_new, z_cur):
        if prep_next:
            _w_in_prep_kernel(wn_ref, wnb_ref)
        x = jnp.where(s == n_half, xs_ref[...], x_ref[...])
        xn = _rms(x, g_ref[...]).astype(BF16)
        starts = list(range(0, D_IN, PROJ_COLS))

        def project():
            if starts:
                n0 = starts.pop(0)
                z_new[:, n0:n0 + PROJ_COLS] = jnp.dot(
                    xn, w_ref[:, n0:n0 + PROJ_COLS], preferred_element_type=F32)

        prev_kv = (kprev_ref[...], vprev_ref[...])
        for sb in range(FUSE_ROWS // CHUNK):
            rows = pl.ds(sb * CHUNK, CHUNK)
            steps = _mix_block_steps(
                half * (FUSE_ROWS // CHUNK) + sb, sinks_ref, z_cur.at[rows],
                rope_ref.at[:, rows], ws_ref, bias_ref, a_ref, pwr_ref, pwi_ref, bcat_ref,
                ccat_ref, d_ref, wglu_ref, bglu_ref, mixed_ref.at[rows], klast_ref,
                vlast_ref, hfin_ref, prev_kv, h_ref, hin_ref, carry_ref, layer)
            while True:
                try:
                    vpu_run_next = next(steps)
                except StopIteration as done:
                    prev_kv = done.value
                    break
                if vpu_run_next:
                    project()
        kprev_ref[...], vprev_ref[...] = prev_kv
        while starts:
            project()

    pl.when(s % 2 == 0)(lambda: step(z_a, z_b))
    pl.when(s % 2 == 1)(lambda: step(z_b, z_a))

    @pl.when(s == n_half)
    def _():
        copy = pltpu.make_async_copy((z_a, z_b)[n_half % 2], zs_ref, zs_sem)
        copy.start()
        copy.wait()


def _prompt_layer(x2d, xs2d, seq, layer, g_all, w_b, w_in_f32, sinks, rope_tab, ws, bias_t, a_rows,
                  pw_re, pw_im, bcat, ccat, d, wglu, bglu):
    m = x2d.shape[0]
    assert xs2d.shape == (FUSE_ROWS, D_MODEL), xs2d.shape
    bsz = m // seq
    n_half = m // FUSE_ROWS
    halves_per_seq = seq // FUSE_ROWS
    prep_next = w_in_f32 is not None
    slab = D_MODEL // n_half
    assert slab * n_half == D_MODEL and slab % (2 * SUBLANES) == 0, (n_half, slab)
    per_layer = lambda *shape, **kw: _layer_spec(shape, layer, **kw)
    once = dict(pipeline_mode=pl.Buffered(1))
    done = lambda s: jnp.maximum(s - 1, 0)
    last = lambda s: jnp.minimum(s, n_half - 1)
    next_in = [pl.BlockSpec((None, slab, D_IN), lambda s: (layer + 1, last(s), 0))] * prep_next
    next_out = [pl.BlockSpec((slab, D_IN), lambda s: (last(s), 0))] * prep_next
    next_shape = [jax.ShapeDtypeStruct((D_MODEL, D_IN), BF16)] * prep_next
    return pl.pallas_call(
        functools.partial(_prompt_layer_kernel, layer=layer, halves_per_seq=halves_per_seq,
                          n_half=n_half, prep_next=prep_next),
        grid=(n_half + 1,),
        in_specs=[
            pl.BlockSpec(memory_space=pltpu.SMEM),
            pl.BlockSpec((FUSE_ROWS, D_MODEL), lambda s: (last(s), 0)),
            pl.BlockSpec((FUSE_ROWS, D_MODEL), lambda s: (0, 0), **once),
            per_layer(1, D_MODEL),
            pl.BlockSpec((D_MODEL, D_IN), lambda s: (0, 0), **once),
            pl.BlockSpec((3, FUSE_ROWS, LANES), lambda s: (0, done(s) % halves_per_seq, 0)),
            per_layer(H_A, CHUNK, CHUNK),
            per_layer(CHUNK, H_A),
            per_layer(2, N_STATE),
            per_layer(SUB_T, N_STATE),
            per_layer(SUB_T, N_STATE),
            per_layer(W_C, 2 * SLAB_STATES, **once),
            per_layer(W_C, 2 * SLAB_STATES, **once),
            per_layer(1, W_C),
            per_layer(W_C, W_C, **once),
            per_layer(1, W_C),
        ] + next_in,
        out_specs=[
            pl.BlockSpec((FUSE_ROWS, D_MODEL), lambda s: (done(s), 0)),
            pl.BlockSpec((None, WINDOW, KV_W), lambda s: (done(s) // halves_per_seq, 0, 0)),
            pl.BlockSpec((None, WINDOW, KV_W), lambda s: (done(s) // halves_per_seq, 0, 0)),
            pl.BlockSpec((None, 2, N_STATE), lambda s: (done(s) // halves_per_seq, 0, 0)),
            pl.BlockSpec(memory_space=pl.ANY),
        ] + next_out,
        out_shape=[
            jax.ShapeDtypeStruct((m, D_MODEL), BF16),
            jax.ShapeDtypeStruct((bsz, WINDOW, KV_W), F32),
            jax.ShapeDtypeStruct((bsz, WINDOW, KV_W), F32),
            jax.ShapeDtypeStruct((bsz, 2, N_STATE), F32),
            jax.ShapeDtypeStruct((FUSE_ROWS, D_IN), F32),
        ] + next_shape,
        scratch_shapes=[
            pltpu.VMEM((FUSE_ROWS, D_IN), F32),
            pltpu.VMEM((FUSE_ROWS, D_IN), F32),
            pltpu.VMEM((WINDOW, KV_W), BF16),
            pltpu.VMEM((WINDOW, KV_W), BF16),
            pltpu.VMEM((CHUNK, 2 * N_STATE), F32),
            pltpu.VMEM((SUBLANES, 2 * N_STATE), F32),
            pltpu.VMEM((2, N_STATE), F32),
            pltpu.SemaphoreType.DMA(()),
        ],
        compiler_params=pltpu.CompilerParams(
            dimension_semantics=("arbitrary",), vmem_limit_bytes=VMEM_LIMIT),
        name="prompt_layer",
    )(sinks, x2d, xs2d, g_all, w_b, rope_tab, ws, bias_t, a_rows, pw_re, pw_im, bcat, ccat, d,
      wglu, bglu, *([w_in_f32] * prep_next))


def _mix_sample_kernel(sinks_ref, z_ref, rope_ref, ck_ref, cv_ref, ws_ref, bias_ref, a_ref,
                       h0r_ref, h0i_ref, bcat_ref, ccat_ref, d_ref, wglu_ref, bglu_ref,
                       mixed_ref, knew_ref, hre_ref, him_ref,
                       q_ref, h_ref, *, layer, n_batch, n_tok):
    b = pl.program_id(0)
    m = n_batch * n_tok

    @pl.when(b == 0)
    def _():
        r_i = lax.broadcasted_iota(jnp.int32, (m, m), 0)
        c_i = lax.broadcasted_iota(jnp.int32, (m, m), 1)
        keep = (r_i // n_tok == c_i // n_tok) & (c_i <= r_i)
        spread = jnp.where(lax.broadcasted_iota(jnp.int32, (m, CHUNK), 0) % n_tok
                           == lax.broadcasted_iota(jnp.int32, (m, CHUNK), 1), 1.0, 0.0).astype(BF16)
        spread_t = jnp.where(lax.broadcasted_iota(jnp.int32, (CHUNK, m), 1) % n_tok
                             == lax.broadcasted_iota(jnp.int32, (CHUNK, m), 0), 1.0, 0.0).astype(BF16)

        def tiled_w(h):
            rows = jnp.dot(spread, ws_ref[h].astype(BF16), preferred_element_type=F32)
            full = jnp.dot(rows.astype(BF16), spread_t, preferred_element_type=F32)
            return jnp.where(keep, full, 0.0).astype(BF16)

        mixed_ref[:, 0:W_A] = _chunk_mlp(tiled_w, bias_ref, z_ref)
        q_ref[...] = _rope(z_ref[:, O_Q:O_Q + W_B], rope_ref) * (HD_B ** -0.5)
        knew_ref[...] = _rope(z_ref[:, O_K:O_K + KV_W], rope_ref)
        _s5_drive(h_ref, z_ref, bcat_ref, n_batch, n_tok)
        finals = _s5_scan(h_ref, a_ref, n_batch, n_tok,
                          lambda n: (h0r_ref[:, _lane_tile(n)], h0i_ref[:, _lane_tile(n)]))
        for n in range(N_TILE):
            hre_ref[:, _lane_tile(n)] = finals[n][0]
            him_ref[:, _lane_tile(n)] = finals[n][1]
        mixed_ref[:, W_A + W_B:] = _s5_readout(
            h_ref, z_ref, ccat_ref, d_ref, wglu_ref, bglu_ref, n_batch, n_tok)

    pad = jnp.zeros((WINDOW - n_tok, KV_W), F32)
    sink = lambda g, r: sinks_ref[layer, g * REP_B + r]
    row_sets, running = [], []
    for j in range(SAMPLE_GROUP):
        rows = pl.ds(pl.multiple_of((b * SAMPLE_GROUP + j) * n_tok, n_tok), n_tok)
        kcat = jnp.concatenate([ck_ref[j], knew_ref[rows, :], pad], axis=0).astype(BF16)
        vcat = jnp.concatenate([cv_ref[j], z_ref[rows, O_V:O_V + KV_W], pad], axis=0).astype(BF16)
        row_sets.append(rows)
        running.append(_attention(q_ref[rows, :], kcat, vcat, sink, 0))
    results = [None] * SAMPLE_GROUP
    while any(r is None for r in results):
        for j, steps in enumerate(running):
            if results[j] is None:
                try:
                    next(steps)
                except StopIteration as done:
                    results[j] = done.value
    for rows, o_b in zip(row_sets, results):
        mixed_ref[rows, W_A:W_A + W_B] = o_b * jax.nn.silu(z_ref[rows, O_GB:O_GB + W_B])


def _mix_sample(z2, layer, sinks, rope_tab, ck, cv, ws, bias_t, a_rows, h0r, h0i,
                bcat, ccat, d, wglu, bglu, n_batch, n_tok):
    m = n_batch * n_tok
    const = lambda *shape: pl.BlockSpec(shape, lambda b: (0,) * len(shape))
    per_layer = lambda *shape: _layer_spec(shape, layer)
    return pl.pallas_call(
        functools.partial(_mix_sample_kernel, layer=layer, n_batch=n_batch, n_tok=n_tok),
        grid=(n_batch // SAMPLE_GROUP,),
        in_specs=[
            pl.BlockSpec(memory_space=pltpu.SMEM),
            const(m, D_IN),
            const(3, m, LANES),
            pl.BlockSpec((None, SAMPLE_GROUP, WINDOW, KV_W), lambda b: (layer, b, 0, 0)),
            pl.BlockSpec((None, SAMPLE_GROUP, WINDOW, KV_W), lambda b: (layer, b, 0, 0)),
            per_layer(H_A, CHUNK, CHUNK),
            per_layer(m, H_A),
            per_layer(2, N_STATE),
            per_layer(n_batch, N_STATE),
            per_layer(n_batch, N_STATE),
            per_layer(W_C, 2 * SLAB_STATES),
            per_layer(W_C, 2 * SLAB_STATES),
            per_layer(1, W_C),
            per_layer(W_C, W_C),
            per_layer(1, W_C),
        ],
        out_specs=[
            const(m, D_MODEL),
            const(m, KV_W),
            const(n_batch, N_STATE),
            const(n_batch, N_STATE),
        ],
        out_shape=[
            jax.ShapeDtypeStruct((m, D_MODEL), F32),
            jax.ShapeDtypeStruct((m, KV_W), F32),
            jax.ShapeDtypeStruct((n_batch, N_STATE), F32),
            jax.ShapeDtypeStruct((n_batch, N_STATE), F32),
        ],
        scratch_shapes=[
            pltpu.VMEM((m, W_B), F32),
            pltpu.VMEM((m, 2 * N_STATE), F32),
        ],
        compiler_params=pltpu.CompilerParams(
            dimension_semantics=("arbitrary",), vmem_limit_bytes=VMEM_LIMIT),
        name="mix_sample",
    )(sinks, z2, rope_tab, ck, cv, ws, bias_t, a_rows, h0r, h0i, bcat, ccat, d, wglu, bglu)


def _rope_table(pos):
    half = ROPE_DIM // 2
    inv = ROPE_THETA ** (-jnp.arange(half, dtype=F32) * 2.0 / ROPE_DIM)
    ang = pos[:, None] * inv[None, :]
    cos, sin = jnp.cos(ang), jnp.sin(ang)
    n = pos.shape[0]
    rest = HD_B - ROPE_DIM
    c = jnp.concatenate([cos, cos, jnp.ones((n, rest), F32)], axis=1)
    s_lo = jnp.concatenate([jnp.zeros((n, half), F32), sin, jnp.zeros((n, rest), F32)], axis=1)
    s_hi = jnp.concatenate([-sin, jnp.zeros((n, half + rest), F32)], axis=1)
    tab = jnp.stack([c, s_lo, s_hi])
    return jnp.concatenate([tab] * (LANES // HD_B), axis=2)


def kernel(x_prompt, x_sample, cache_swa_k, cache_swa_v, state_ssm_re, state_ssm_im, norm_g,
           final_norm_g, w_in, w_out, chunk_w_s, chunk_b_s, attn_sinks, ssm_a_re, ssm_a_im,
           ssm_log_dt, ssm_b_re, ssm_b_im, ssm_c_re, ssm_c_im, ssm_d, glu_w, glu_b):
    bsz, seq, _ = x_prompt.shape
    n_batch, n_tok, _ = x_sample.shape
    m_s = n_batch * n_tok

    w_in_b, w_out_b = _prep_first_weights(w_in, w_out)
    wglu_b = glu_w.astype(BF16)

    ab_re, ab_im, pw_re, pw_im, bcat, ccat = _s5_prep(
        ssm_a_re, ssm_a_im, ssm_log_dt, ssm_b_re, ssm_b_im, ssm_c_re, ssm_c_im)
    a_rows = jnp.concatenate([ab_re.reshape(DEPTH, 1, N_STATE),
                              ab_im.reshape(DEPTH, 1, N_STATE)], axis=1)
    pw_re = jnp.transpose(pw_re, (0, 2, 1, 3)).reshape(DEPTH, SUB_T, N_STATE)
    pw_im = jnp.transpose(pw_im, (0, 2, 1, 3)).reshape(DEPTH, SUB_T, N_STATE)

    rope_p = _rope_table(jnp.arange(seq, dtype=F32))
    rope_s = jnp.tile(_rope_table(jnp.arange(n_tok, dtype=F32) + PAST_LEN), (1, n_batch, 1))
    bias_p = jnp.transpose(chunk_b_s, (0, 2, 1))
    bias_s = jnp.tile(bias_p[:, :n_tok], (1, n_batch, 1))
    ck = cache_swa_k.reshape(DEPTH, n_batch, WINDOW, KV_W)
    cv = cache_swa_v.reshape(DEPTH, n_batch, WINDOW, KV_W)
    h0r = state_ssm_re.reshape(DEPTH, n_batch, N_STATE)
    h0i = state_ssm_im.reshape(DEPTH, n_batch, N_STATE)
    g_all = norm_g.reshape(DEPTH, 1, D_MODEL)
    d_all = ssm_d.reshape(DEPTH, 1, W_C)
    bglu_all = glu_b.reshape(DEPTH, 1, W_C)
    fg = final_norm_g.reshape(1, D_MODEL)

    xp = x_prompt.reshape(bsz * seq, D_MODEL)
    xs = x_sample.reshape(m_s, D_MODEL)
    outs = [[] for _ in range(9)]
    for l in range(DEPTH):
        more = l + 1 < DEPTH
        mixed_p, k_last, v_last, h_fin, zs, *w_in_next = _prompt_layer(
            xp, xs, seq, l, g_all, w_in_b, w_in if more else None, attn_sinks, rope_p, chunk_w_s,
            bias_p, a_rows, pw_re, pw_im, bcat, ccat, d_all, wglu_b, bglu_all)
        if more:
            xp, w_out_next = _out_proj(mixed_p, xp, w_out_b, fg, l, w_out)
        else:
            xp = _out_proj(mixed_p, xp, w_out_b, fg, l)

        mixed_s, k_new, h_re_s, h_im_s = _mix_sample(
            zs, l, attn_sinks, rope_s, ck, cv, chunk_w_s, bias_s, a_rows, h0r, h0i,
            bcat, ccat, d_all, wglu_b, bglu_all, n_batch, n_tok)
        xs = _out_proj(mixed_s, xs, w_out_b, fg, l)
        if more:
            w_in_b, w_out_b = w_in_next[0], w_out_next

        outs[0].append(k_last.reshape(bsz, WINDOW, KV_B, HD_B))
        outs[1].append(v_last.reshape(bsz, WINDOW, KV_B, HD_B))
        outs[2].append(k_new.reshape(n_batch, n_tok, KV_B, HD_B))
        outs[3].append(zs[:, O_V:O_V + KV_W].reshape(n_batch, n_tok, KV_B, HD_B))
        outs[4].append(h_fin[:, 0].reshape(bsz, G_C, P_C))
        outs[5].append(h_fin[:, 1].reshape(bsz, G_C, P_C))
        outs[6].append(h_re_s.reshape(n_batch, G_C, P_C))
        outs[7].append(h_im_s.reshape(n_batch, G_C, P_C))
        outs[8].append(zs[:, O_VA:O_VA + W_A].reshape(n_batch, n_tok, W_A))

    return (xp.reshape(bsz, seq, D_MODEL), xs.reshape(n_batch, n_tok, D_MODEL),
            *[jnp.stack(o) for o in outs])
```

```python
import functools

import jax
import jax.numpy as jnp
from jax import lax
from jax.experimental import pallas as pl
from jax.experimental.pallas import tpu as pltpu

F32 = jnp.float32
BF16 = jnp.bfloat16

D_MODEL = 2048
DEPTH = 4
PAST_LEN = 16384
CHUNK = 128
W_A = 512
H_A = 4
C_A = W_A // H_A
W_B = 1024
HD_B = 64
H_B = W_B // HD_B
KV_B = 4
REP_B = H_B // KV_B
KV_W = KV_B * HD_B
WINDOW = 128
ROPE_DIM = HD_B // 4
ROPE_THETA = 500000.0
W_C = 512
GC = 16
G_C = W_C // GC
P_C = 64
N_STATE = G_C * P_C
D_IN = 3 * W_A + 2 * W_B + 2 * KV_W + 2 * W_C
EPS = 1e-5
NEG = -1e30

O_UA, O_VA, O_GA = 0, W_A, 2 * W_A
O_Q = 3 * W_A
O_K = O_Q + W_B
O_V = O_K + KV_W
O_GB = O_V + KV_W
O_UC = O_GB + W_B
O_GC = O_UC + W_C

LANES = 128
SUBLANES = 8
N_TILE = N_STATE // LANES
N_SLAB = W_C // LANES
SLAB_GROUPS = G_C // N_SLAB
SLAB_STATES = SLAB_GROUPS * P_C
SLAB_TILES = SLAB_STATES // LANES
VMEM_LIMIT = 60 * 1024 * 1024

OUT_ROWS = 512
PROJ_COLS = 512
FUSE_ROWS = 256
SAMPLE_GROUP = 8
VPU_RUN_NEXT = True
W_PREP_ROWS = 256
W_PREP_COLS = 512
SCAN_VREGS = 4
SUB_T = CHUNK // SUBLANES

_NT = (((1,), (1,)), ((), ()))


def _rms(x, g):
    return x * lax.rsqrt(jnp.mean(x * x, axis=-1, keepdims=True) + EPS) * g


def _layer_spec(shape, layer, **kw):
    zeros = (0,) * len(shape)
    return pl.BlockSpec((None,) + tuple(shape), lambda *_: (layer,) + zeros, **kw)


def _out_proj_kernel(m_ref, x_ref, w_ref, fg_ref, *rest, final, prep_next):
    if prep_next:
        *wn_refs, y_ref, wnb_ref = rest
        for j, wn_ref in enumerate(wn_refs):
            wnb_ref[j * HD_B:(j + 1) * HD_B, :] = wn_ref[...].astype(BF16)
    else:
        y_ref, = rest
    y = x_ref[...] + jnp.dot(m_ref[...].astype(BF16), w_ref[...], preferred_element_type=F32)
    if final:
        y = _rms(y, fg_ref[...])
    y_ref[...] = y


def _w_out_source_block(i):
    first, last = W_A // HD_B, (W_A + W_B) // HD_B
    head = i - first
    src = first + (head % KV_B) * REP_B + head // KV_B
    return jnp.where((i >= first) & (i < last), src, i)


def _out_proj(mixed, x2d, w_b, fg, layer, w_out_f32=None):
    m = x2d.shape[0]
    tm = min(OUT_ROWS, m)
    final = layer == DEPTH - 1
    prep_next = w_out_f32 is not None
    in_specs = [
        pl.BlockSpec((tm, D_MODEL), lambda i: (i, 0)),
        pl.BlockSpec((tm, D_MODEL), lambda i: (i, 0)),
        pl.BlockSpec((D_MODEL, D_MODEL), lambda i: (0, 0), pipeline_mode=pl.Buffered(1)),
        pl.BlockSpec((1, D_MODEL), lambda i: (0, 0)),
    ]
    out_specs = [pl.BlockSpec((tm, D_MODEL), lambda i: (i, 0))]
    out_shape = [jax.ShapeDtypeStruct((m, D_MODEL), F32)]
    args = [mixed, x2d, w_b, fg]
    if prep_next:
        per_step = (D_MODEL // HD_B) // (m // tm)
        assert per_step * (m // tm) * HD_B == D_MODEL, (m, tm)
        for j in range(per_step):
            in_specs.append(pl.BlockSpec(
                (None, HD_B, D_MODEL),
                lambda i, j=j: (layer + 1, _w_out_source_block(i * per_step + j), 0)))
            args.append(w_out_f32)
        out_specs.append(pl.BlockSpec((per_step * HD_B, D_MODEL), lambda i: (i, 0)))
        out_shape.append(jax.ShapeDtypeStruct((D_MODEL, D_MODEL), BF16))
    out = pl.pallas_call(
        functools.partial(_out_proj_kernel, final=final, prep_next=prep_next),
        grid=(m // tm,),
        in_specs=in_specs,
        out_specs=out_specs,
        out_shape=out_shape,
        compiler_params=pltpu.CompilerParams(
            dimension_semantics=("arbitrary",), vmem_limit_bytes=VMEM_LIMIT),
        name="out_proj_final" if final else "out_proj",
    )(*args)
    return out if prep_next else out[0]


def _rep_major_source(new_head):
    r, g = divmod(new_head, KV_B)
    return g * REP_B + r


def _w_in_prep_kernel(w_ref, o_ref):
    heads_per_tile = LANES // HD_B
    lane_lo = lax.broadcasted_iota(jnp.int32, (w_ref.shape[0], LANES), 1) < HD_B

    def half(sec, head, want_hi):
        t = w_ref[:, sec + (head // heads_per_tile) * LANES:sec + (head // heads_per_tile + 1) * LANES]
        return t if (head % heads_per_tile == 1) == want_hi else pltpu.roll(t, HD_B, 1)

    for c0, c1, permute in ((0, O_Q, False), (O_Q, O_K, True), (O_K, O_GB, False),
                            (O_GB, O_UC, True), (O_UC, D_IN, False)):
        if not permute:
            o_ref[:, c0:c1] = w_ref[:, c0:c1].astype(BF16)
            continue
        for j in range(W_B // LANES):
            lo = half(c0, _rep_major_source(heads_per_tile * j), False)
            hi = half(c0, _rep_major_source(heads_per_tile * j + 1), True)
            o_ref[:, c0 + j * LANES:c0 + (j + 1) * LANES] = jnp.where(lane_lo, lo, hi).astype(BF16)


def _w_out_prep_kernel(w_ref, o_ref):
    o_ref[0:W_A, :] = w_ref[0:W_A, :].astype(BF16)
    for new_head in range(H_B):
        src = W_A + _rep_major_source(new_head) * HD_B
        dst = W_A + new_head * HD_B
        o_ref[dst:dst + HD_B, :] = w_ref[src:src + HD_B, :].astype(BF16)
    o_ref[W_A + W_B:, :] = w_ref[W_A + W_B:, :].astype(BF16)


def _prep_first_weights(w_in, w_out):
    w_in_b = pl.pallas_call(
        _w_in_prep_kernel,
        grid=(D_MODEL // W_PREP_ROWS,),
        in_specs=[pl.BlockSpec((None, W_PREP_ROWS, D_IN), lambda i: (0, i, 0))],
        out_specs=pl.BlockSpec((W_PREP_ROWS, D_IN), lambda i: (i, 0)),
        out_shape=jax.ShapeDtypeStruct((D_MODEL, D_IN), BF16),
        compiler_params=pltpu.CompilerParams(
            dimension_semantics=("arbitrary",), vmem_limit_bytes=VMEM_LIMIT),
        name="w_in_prep",
    )(w_in)
    w_out_b = pl.pallas_call(
        _w_out_prep_kernel,
        grid=(D_MODEL // W_PREP_COLS,),
        in_specs=[pl.BlockSpec((None, D_MODEL, W_PREP_COLS), lambda i: (0, 0, i))],
        out_specs=pl.BlockSpec((D_MODEL, W_PREP_COLS), lambda i: (0, i)),
        out_shape=jax.ShapeDtypeStruct((D_MODEL, D_MODEL), BF16),
        compiler_params=pltpu.CompilerParams(
            dimension_semantics=("arbitrary",), vmem_limit_bytes=VMEM_LIMIT),
        name="w_out_prep",
    )(w_out)
    return w_in_b, w_out_b


def _store_block_diag(out_ref, lane0, x3):
    per_tile = LANES // P_C
    lane_slot = lax.broadcasted_iota(jnp.int32, (GC, LANES), 1) // P_C
    for g in range(G_C):
        xg = x3[g]
        pair = jnp.concatenate([xg] * per_tile, axis=1)
        tile = jnp.where(lane_slot == g % per_tile, pair, 0.0)
        l0 = lane0 + ((g % SLAB_GROUPS) // per_tile) * LANES
        out_ref[g * GC:(g + 1) * GC, l0:l0 + LANES] = tile.astype(BF16)


def _s5_prep_kernel(are_ref, aim_ref, ldt_ref, btr_ref, bti_ref, cre_ref, cim_ref,
                    abr_ref, abi_ref, pwr_ref, pwi_ref, bcat_ref, ccat_ref):
    a_re, a_im = are_ref[...], aim_ref[...]
    dt = jnp.exp(ldt_ref[...])
    mag = jnp.exp(a_re * dt)
    ab_re = mag * jnp.cos(a_im * dt)
    ab_im = mag * jnp.sin(a_im * dt)
    nr, ni = ab_re - 1.0, ab_im
    den = a_re * a_re + a_im * a_im
    f_re = (nr * a_re + ni * a_im) / den
    f_im = (ni * a_re - nr * a_im) / den
    br, bi = btr_ref[...], bti_ref[...]
    bcat_ref[...] = jnp.zeros_like(bcat_ref)
    ccat_ref[...] = jnp.zeros_like(ccat_ref)
    _store_block_diag(bcat_ref, 0, f_re * br - f_im * bi)
    _store_block_diag(bcat_ref, SLAB_STATES, f_re * bi + f_im * br)
    _store_block_diag(ccat_ref, 0, cre_ref[...])
    _store_block_diag(ccat_ref, SLAB_STATES, -cim_ref[...])
    abr_ref[...] = ab_re
    abi_ref[...] = ab_im
    p_re, p_im = ab_re, ab_im
    for j in range(SUB_T):
        pwr_ref[:, j:j + 1, :] = p_re
        pwi_ref[:, j:j + 1, :] = p_im
        p_re, p_im = p_re * ab_re - p_im * ab_im, p_re * ab_im + p_im * ab_re


def _s5_prep(a_re, a_im, log_dt, b_re, b_im, c_re, c_im):
    a4 = lambda a: a.reshape(DEPTH, G_C, 1, P_C)
    bt = lambda b: jnp.transpose(b, (0, 1, 3, 2))
    vec = pl.BlockSpec((None, G_C, 1, P_C), lambda l: (l, 0, 0, 0))
    dts = pl.BlockSpec((None, G_C, 1, 1), lambda l: (l, 0, 0, 0))
    mat = pl.BlockSpec((None, G_C, GC, P_C), lambda l: (l, 0, 0, 0))
    pws = pl.BlockSpec((None, G_C, SUB_T, P_C), lambda l: (l, 0, 0, 0))
    dense = pl.BlockSpec((None, W_C, 2 * SLAB_STATES), lambda l: (l, 0, 0))
    vshape = jax.ShapeDtypeStruct((DEPTH, G_C, 1, P_C), F32)
    pshape = jax.ShapeDtypeStruct((DEPTH, G_C, SUB_T, P_C), F32)
    dshape = jax.ShapeDtypeStruct((DEPTH, W_C, 2 * SLAB_STATES), BF16)
    return pl.pallas_call(
        _s5_prep_kernel,
        grid=(DEPTH,),
        in_specs=[vec, vec, dts, mat, mat, mat, mat],
        out_specs=[vec, vec, pws, pws, dense, dense],
        out_shape=[vshape, vshape, pshape, pshape, dshape, dshape],
        name="s5_prep",
    )(a4(a_re), a4(a_im), log_dt.reshape(DEPTH, G_C, 1, 1), bt(b_re), bt(b_im), c_re, c_im)


def _rope(x, tab_ref):
    c, s_lo, s_hi = tab_ref[0], tab_ref[1], tab_ref[2]
    half = ROPE_DIM // 2
    tiles = []
    for j in range(x.shape[1] // LANES):
        t = x[:, j * LANES:(j + 1) * LANES]
        tiles.append(t * c + pltpu.roll(t, half, 1) * s_lo + pltpu.roll(t, LANES - half, 1) * s_hi)
    return jnp.concatenate(tiles, axis=1) if len(tiles) > 1 else tiles[0]


def _chunk_mlp(mix_w, bias_ref, z_ref):
    outs = []
    for h in range(H_A):
        v = z_ref[:, O_VA + h * C_A:O_VA + (h + 1) * C_A].astype(BF16)
        zz = jnp.dot(mix_w(h), v, preferred_element_type=F32) + bias_ref[:, h:h + 1]
        u = z_ref[:, O_UA + h * C_A:O_UA + (h + 1) * C_A]
        g = z_ref[:, O_GA + h * C_A:O_GA + (h + 1) * C_A]
        outs.append(u * zz * jax.nn.silu(g))
    return jnp.concatenate(outs, axis=1)


def _drain(steps):
    try:
        while True:
            next(steps)
    except StopIteration as done:
        return done.value


def _attention(q, kcat, vcat, sink, prev_off):
    t_len = q.shape[0]
    lane_grp = lax.broadcasted_iota(jnp.int32, (t_len, KV_W), 1) // HD_B
    pieces = []
    for r in range(REP_B):
        chunk = q[:, r * KV_W:(r + 1) * KV_W]
        for g in range(KV_B):
            pieces.append(jnp.where(lane_grp == g, chunk, 0.0))
    qbd = jnp.concatenate(pieces, axis=0).astype(BF16)
    s_all = lax.dot_general(qbd, kcat, _NT, preferred_element_type=F32)
    row = lax.broadcasted_iota(jnp.int32, (t_len, 2 * WINDOW), 0)
    col = lax.broadcasted_iota(jnp.int32, (t_len, 2 * WINDOW), 1)
    visible = jnp.where(col < WINDOW, col - row - prev_off, row - col + WINDOW + 1) > 0
    heads = [(r, g) for r in range(REP_B) for g in range(KV_B)]
    masked = lambda hh: jnp.where(visible, s_all[hh * t_len:(hh + 1) * t_len], NEG)
    yield VPU_RUN_NEXT
    tops = [jnp.maximum(jnp.max(masked(hh), axis=-1, keepdims=True), sink(g, r))
            for hh, (r, g) in enumerate(heads)]
    yield
    probs, scales = [], []
    for hh, (r, g) in enumerate(heads):
        p = jnp.exp(masked(hh) - tops[hh])
        den = jnp.sum(p, axis=-1, keepdims=True) + jnp.exp(sink(g, r) - tops[hh])
        probs.append(p.astype(BF16))
        scales.append(1.0 / den)
        if hh == len(heads) // 2 - 1:
            yield VPU_RUN_NEXT
    yield
    p_all = jnp.concatenate(probs, axis=0)
    o_all = jnp.dot(p_all, vcat, preferred_element_type=F32)
    yield
    outs = []
    for r in range(REP_B):
        acc = None
        for g in range(KV_B):
            hh = r * KV_B + g
            o = o_all[hh * t_len:(hh + 1) * t_len] * scales[hh]
            acc = o if acc is None else jnp.where(lane_grp == g, o, acc)
        outs.append(acc)
    return jnp.concatenate(outs, axis=1)


def _lane_tile(n):
    return slice(n * LANES, (n + 1) * LANES)


def _seq_major_perm(n_seq, n_step, transpose):
    m = n_seq * n_step
    row = lax.broadcasted_iota(jnp.int32, (m, m), 0)
    col = lax.broadcasted_iota(jnp.int32, (m, m), 1)
    if transpose:
        hit = row == (col % n_seq) * n_step + col // n_seq
    else:
        hit = col == (row % n_seq) * n_step + row // n_seq
    return jnp.where(hit, 1.0, 0.0).astype(BF16)


def _h_lanes(n):
    base = (n // SLAB_TILES) * 2 * SLAB_STATES + (n % SLAB_TILES) * LANES
    return slice(base, base + LANES), slice(base + SLAB_STATES, base + SLAB_STATES + LANES)


def _s5_drive(h_ref, z_ref, bcat_ref, n_seq, n_step):
    u = z_ref[:, O_UC:O_UC + W_C].astype(BF16)
    u = jnp.dot(_seq_major_perm(n_seq, n_step, False), u, preferred_element_type=F32).astype(BF16)
    for s in range(N_SLAB):
        h_ref[:, s * 2 * SLAB_STATES:(s + 1) * 2 * SLAB_STATES] = jnp.dot(
            u[:, _lane_tile(s)], bcat_ref[_lane_tile(s), :], preferred_element_type=F32)


def _s5_scan(h_ref, a_ref, n_seq, n_step, init):
    group = max(1, SCAN_VREGS * SUBLANES // n_seq)
    finals = [None] * N_TILE
    for n0 in range(0, N_TILE, group):
        tiles = range(n0, n0 + group)
        state = {n: init(n) for n in tiles}
        for j in range(n_step):
            rows = slice(j * n_seq, (j + 1) * n_seq)
            for n in tiles:
                re_l, im_l = _h_lanes(n)
                ar, ai = a_ref[0:1, _lane_tile(n)], a_ref[1:2, _lane_tile(n)]
                hr, hi = state[n]
                hr, hi = (ar * hr - ai * hi + h_ref[rows, re_l],
                          ar * hi + ai * hr + h_ref[rows, im_l])
                h_ref[rows, re_l] = hr
                h_ref[rows, im_l] = hi
                state[n] = (hr, hi)
        for n in tiles:
            finals[n] = state[n]
    return finals


def _s5_readout(h_ref, z_ref, ccat_ref, d_ref, wglu_ref, bglu_ref, n_seq, n_step):
    y = jnp.concatenate([
        lax.dot_general(h_ref[:, s * 2 * SLAB_STATES:(s + 1) * 2 * SLAB_STATES].astype(BF16),
                        ccat_ref[_lane_tile(s), :], _NT, preferred_element_type=F32)
        for s in range(N_SLAB)], axis=1)
    y_hi = y.astype(BF16)
    y_lo = (y - y_hi.astype(F32)).astype(BF16)
    back = _seq_major_perm(n_seq, n_step, True)
    y = (jnp.dot(back, y_hi, preferred_element_type=F32)
         + jnp.dot(back, y_lo, preferred_element_type=F32))
    y = jax.nn.gelu(y + d_ref[...] * z_ref[:, O_UC:O_UC + W_C])
    gate = jnp.dot(y.astype(BF16), wglu_ref[...], preferred_element_type=F32) + bglu_ref[...]
    y = y * jax.nn.sigmoid(gate)
    return y * jax.nn.silu(z_ref[:, O_GC:O_GC + W_C])


def _mix_block_steps(i, sinks_ref, z_ref, rope_ref, ws_ref, bias_ref, a_ref, pwr_ref, pwi_ref,
                     bcat_ref, ccat_ref, d_ref, wglu_ref, bglu_ref,
                     mixed_ref, klast_ref, vlast_ref, hfin_ref,
                     prev_kv, h_ref, hin_ref, carry_ref, layer):
    r_i = lax.broadcasted_iota(jnp.int32, (CHUNK, CHUNK), 0)
    c_i = lax.broadcasted_iota(jnp.int32, (CHUNK, CHUNK), 1)
    causal_w = lambda h: jnp.where(c_i <= r_i, ws_ref[h], 0.0).astype(BF16)
    mixed_ref[:, 0:W_A] = _chunk_mlp(causal_w, bias_ref, z_ref).astype(BF16)
    yield VPU_RUN_NEXT

    q = _rope(z_ref[:, O_Q:O_Q + W_B], rope_ref) * (HD_B ** -0.5)
    k = _rope(z_ref[:, O_K:O_K + KV_W], rope_ref)
    v = z_ref[:, O_V:O_V + KV_W]
    klast_ref[...] = k
    vlast_ref[...] = v
    kb, vb = k.astype(BF16), v.astype(BF16)
    kcat = jnp.concatenate([prev_kv[0], kb], axis=0)
    vcat = jnp.concatenate([prev_kv[1], vb], axis=0)
    prev_off = jnp.where(i > 0, 0, 2 * WINDOW)
    sink = lambda g, r: sinks_ref[layer, g * REP_B + r]
    yield
    o_b = yield from _attention(q, kcat, vcat, sink, prev_off)
    mixed_ref[:, W_A:W_A + W_B] = (o_b * jax.nn.silu(z_ref[:, O_GB:O_GB + W_B])).astype(BF16)
    yield

    _s5_drive(h_ref, z_ref, bcat_ref, SUBLANES, SUB_T)
    yield VPU_RUN_NEXT
    zero = jnp.zeros((SUBLANES, LANES), F32)
    finals = _s5_scan(h_ref, a_ref, SUBLANES, SUB_T, lambda n: (zero, zero))
    yield
    for n in range(N_TILE):
        st_l = _lane_tile(n)
        re_l, im_l = _h_lanes(n)
        end_r, end_i = finals[n]
        a_t_r, a_t_i = pwr_ref[SUB_T - 1:SUB_T, st_l], pwi_ref[SUB_T - 1:SUB_T, st_l]
        cr, ci = carry_ref[0:1, st_l], carry_ref[1:2, st_l]
        for r in range(SUBLANES):
            hin_ref[r:r + 1, re_l] = cr
            hin_ref[r:r + 1, im_l] = ci
            cr, ci = (end_r[r:r + 1] + a_t_r * cr - a_t_i * ci,
                      end_i[r:r + 1] + a_t_r * ci + a_t_i * cr)
        carry_ref[0:1, st_l] = cr
        carry_ref[1:2, st_l] = ci
        in_r, in_i = hin_ref[:, re_l], hin_ref[:, im_l]
        for j in range(SUB_T):
            rows = slice(j * SUBLANES, (j + 1) * SUBLANES)
            p_r, p_i = pwr_ref[j:j + 1, st_l], pwi_ref[j:j + 1, st_l]
            h_ref[rows, re_l] = h_ref[rows, re_l] + (p_r * in_r - p_i * in_i)
            h_ref[rows, im_l] = h_ref[rows, im_l] + (p_r * in_i + p_i * in_r)
        if n == N_TILE // 2 - 1:
            yield VPU_RUN_NEXT
    yield
    hfin_ref[...] = carry_ref[...]
    mixed_ref[:, W_A + W_B:] = _s5_readout(
        h_ref, z_ref, ccat_ref, d_ref, wglu_ref, bglu_ref, SUBLANES, SUB_T).astype(BF16)
    return kb, vb


def _prompt_layer_kernel(*refs, layer, halves_per_seq, n_half, prep_next):
    refs = list(refs)
    (sinks_ref, x_ref, xs_ref, g_ref, w_ref, rope_ref, ws_ref, bias_ref, a_ref, pwr_ref, pwi_ref,
     bcat_ref, ccat_ref, d_ref, wglu_ref, bglu_ref) = refs[:16]
    del refs[:16]
    wn_ref = refs.pop(0) if prep_next else None
    mixed_ref, klast_ref, vlast_ref, hfin_ref, zs_ref = refs[:5]
    del refs[:5]
    wnb_ref = refs.pop(0) if prep_next else None
    z_a, z_b, kprev_ref, vprev_ref, h_ref, hin_ref, carry_ref, zs_sem = refs
    s = pl.program_id(0)
    half = jnp.maximum(s - 1, 0) % halves_per_seq

    @pl.when(s == 0)
    def _():
        z_b[...] = jnp.zeros_like(z_b)

    @pl.when(half == 0)
    def _():
        kprev_ref[...] = jnp.zeros_like(kprev_ref)
        vprev_ref[...] = jnp.zeros_like(vprev_ref)
        carry_ref[...] = jnp.zeros_like(carry_ref)

    def step(z_new, z_cur):
        if prep_next:
            _w_in_prep_kernel(wn_ref, wnb_ref)
        x = jnp.where(s == n_half, xs_ref[...], x_ref[...])
        xn = _rms(x, g_ref[...]).astype(BF16)
        starts = list(range(0, D_IN, PROJ_COLS))

        def project():
            if starts:
                n0 = starts.pop(0)
                z_new[:, n0:n0 + PROJ_COLS] = jnp.dot(
                    xn, w_ref[:, n0:n0 + PROJ_COLS], preferred_element_type=F32)

        prev_kv = (kprev_ref[...], vprev_ref[...])
        for sb in range(FUSE_ROWS // CHUNK):
            rows = pl.ds(sb * CHUNK, CHUNK)
            steps = _mix_block_steps(
                half * (FUSE_ROWS // CHUNK) + sb, sinks_ref, z_cur.at[rows],
                rope_ref.at[:, rows], ws_ref, bias_ref, a_ref, pwr_ref, pwi_ref, bcat_ref,
                ccat_ref, d_ref, wglu_ref, bglu_ref, mixed_ref.at[rows], klast_ref,
                vlast_ref, hfin_ref, prev_kv, h_ref, hin_ref, carry_ref, layer)
            while True:
                try:
                    vpu_run_next = next(steps)
                except StopIteration as done:
                    prev_kv = done.value
                    break
                if vpu_run_next:
                    project()
        kprev_ref[...], vprev_ref[...] = prev_kv
        while starts:
            project()

    pl.when(s % 2 == 0)(lambda: step(z_a, z_b))
    pl.when(s % 2 == 1)(lambda: step(z_b, z_a))

    @pl.when(s == n_half)
    def _():
        copy = pltpu.make_async_copy((z_a, z_b)[n_half % 2], zs_ref, zs_sem)
        copy.start()
        copy.wait()


def _prompt_layer(x2d, xs2d, seq, layer, g_all, w_b, w_in_f32, sinks, rope_tab, ws, bias_t, a_rows,
                  pw_re, pw_im, bcat, ccat, d, wglu, bglu):
    m = x2d.shape[0]
    assert xs2d.shape == (FUSE_ROWS, D_MODEL), xs2d.shape
    bsz = m // seq
    n_half = m // FUSE_ROWS
    halves_per_seq = seq // FUSE_ROWS
    prep_next = w_in_f32 is not None
    slab = D_MODEL // n_half
    assert slab * n_half == D_MODEL and slab % (2 * SUBLANES) == 0, (n_half, slab)
    per_layer = lambda *shape, **kw: _layer_spec(shape, layer, **kw)
    once = dict(pipeline_mode=pl.Buffered(1))
    done = lambda s: jnp.maximum(s - 1, 0)
    last = lambda s: jnp.minimum(s, n_half - 1)
    next_in = [pl.BlockSpec((None, slab, D_IN), lambda s: (layer + 1, last(s), 0))] * prep_next
    next_out = [pl.BlockSpec((slab, D_IN), lambda s: (last(s), 0))] * prep_next
    next_shape = [jax.ShapeDtypeStruct((D_MODEL, D_IN), BF16)] * prep_next
    return pl.pallas_call(
        functools.partial(_prompt_layer_kernel, layer=layer, halves_per_seq=halves_per_seq,
                          n_half=n_half, prep_next=prep_next),
        grid=(n_half + 1,),
        in_specs=[
            pl.BlockSpec(memory_space=pltpu.SMEM),
            pl.BlockSpec((FUSE_ROWS, D_MODEL), lambda s: (last(s), 0)),
            pl.BlockSpec((FUSE_ROWS, D_MODEL), lambda s: (0, 0), **once),
            per_layer(1, D_MODEL),
            pl.BlockSpec((D_MODEL, D_IN), lambda s: (0, 0), **once),
            pl.BlockSpec((3, FUSE_ROWS, LANES), lambda s: (0, done(s) % halves_per_seq, 0)),
            per_layer(H_A, CHUNK, CHUNK),
            per_layer(CHUNK, H_A),
            per_layer(2, N_STATE),
            per_layer(SUB_T, N_STATE),
            per_layer(SUB_T, N_STATE),
            per_layer(W_C, 2 * SLAB_STATES, **once),
            per_layer(W_C, 2 * SLAB_STATES, **once),
            per_layer(1, W_C),
            per_layer(W_C, W_C, **once),
            per_layer(1, W_C),
        ] + next_in,
        out_specs=[
            pl.BlockSpec((FUSE_ROWS, D_MODEL), lambda s: (done(s), 0)),
            pl.BlockSpec((None, WINDOW, KV_W), lambda s: (done(s) // halves_per_seq, 0, 0)),
            pl.BlockSpec((None, WINDOW, KV_W), lambda s: (done(s) // halves_per_seq, 0, 0)),
            pl.BlockSpec((None, 2, N_STATE), lambda s: (done(s) // halves_per_seq, 0, 0)),
            pl.BlockSpec(memory_space=pl.ANY),
        ] + next_out,
        out_shape=[
            jax.ShapeDtypeStruct((m, D_MODEL), BF16),
            jax.ShapeDtypeStruct((bsz, WINDOW, KV_W), F32),
            jax.ShapeDtypeStruct((bsz, WINDOW, KV_W), F32),
            jax.ShapeDtypeStruct((bsz, 2, N_STATE), F32),
            jax.ShapeDtypeStruct((FUSE_ROWS, D_IN), F32),
        ] + next_shape,
        scratch_shapes=[
            pltpu.VMEM((FUSE_ROWS, D_IN), F32),
            pltpu.VMEM((FUSE_ROWS, D_IN), F32),
            pltpu.VMEM((WINDOW, KV_W), BF16),
            pltpu.VMEM((WINDOW, KV_W), BF16),
            pltpu.VMEM((CHUNK, 2 * N_STATE), F32),
            pltpu.VMEM((SUBLANES, 2 * N_STATE), F32),
            pltpu.VMEM((2, N_STATE), F32),
            pltpu.SemaphoreType.DMA(()),
        ],
        compiler_params=pltpu.CompilerParams(
            dimension_semantics=("arbitrary",), vmem_limit_bytes=VMEM_LIMIT),
        name="prompt_layer",
    )(sinks, x2d, xs2d, g_all, w_b, rope_tab, ws, bias_t, a_rows, pw_re, pw_im, bcat, ccat, d,
      wglu, bglu, *([w_in_f32] * prep_next))


def _mix_sample_kernel(sinks_ref, z_ref, rope_ref, ck_ref, cv_ref, ws_ref, bias_ref, a_ref,
                       h0r_ref, h0i_ref, bcat_ref, ccat_ref, d_ref, wglu_ref, bglu_ref,
                       mixed_ref, knew_ref, hre_ref, him_ref,
                       q_ref, h_ref, *, layer, n_batch, n_tok):
    b = pl.program_id(0)
    m = n_batch * n_tok

    @pl.when(b == 0)
    def _():
        r_i = lax.broadcasted_iota(jnp.int32, (m, m), 0)
        c_i = lax.broadcasted_iota(jnp.int32, (m, m), 1)
        keep = (r_i // n_tok == c_i // n_tok) & (c_i <= r_i)
        spread = jnp.where(lax.broadcasted_iota(jnp.int32, (m, CHUNK), 0) % n_tok
                           == lax.broadcasted_iota(jnp.int32, (m, CHUNK), 1), 1.0, 0.0).astype(BF16)
        spread_t = jnp.where(lax.broadcasted_iota(jnp.int32, (CHUNK, m), 1) % n_tok
                             == lax.broadcasted_iota(jnp.int32, (CHUNK, m), 0), 1.0, 0.0).astype(BF16)

        def tiled_w(h):
            rows = jnp.dot(spread, ws_ref[h].astype(BF16), preferred_element_type=F32)
            full = jnp.dot(rows.astype(BF16), spread_t, preferred_element_type=F32)
            return jnp.where(keep, full, 0.0).astype(BF16)

        mixed_ref[:, 0:W_A] = _chunk_mlp(tiled_w, bias_ref, z_ref)
        q_ref[...] = _rope(z_ref[:, O_Q:O_Q + W_B], rope_ref) * (HD_B ** -0.5)
        knew_ref[...] = _rope(z_ref[:, O_K:O_K + KV_W], rope_ref)
        _s5_drive(h_ref, z_ref, bcat_ref, n_batch, n_tok)
        finals = _s5_scan(h_ref, a_ref, n_batch, n_tok,
                          lambda n: (h0r_ref[:, _lane_tile(n)], h0i_ref[:, _lane_tile(n)]))
        for n in range(N_TILE):
            hre_ref[:, _lane_tile(n)] = finals[n][0]
            him_ref[:, _lane_tile(n)] = finals[n][1]
        mixed_ref[:, W_A + W_B:] = _s5_readout(
            h_ref, z_ref, ccat_ref, d_ref, wglu_ref, bglu_ref, n_batch, n_tok)

    pad = jnp.zeros((WINDOW - n_tok, KV_W), F32)
    sink = lambda g, r: sinks_ref[layer, g * REP_B + r]
    row_sets, running = [], []
    for j in range(SAMPLE_GROUP):
        rows = pl.ds(pl.multiple_of((b * SAMPLE_GROUP + j) * n_tok, n_tok), n_tok)
        kcat = jnp.concatenate([ck_ref[j], knew_ref[rows, :], pad], axis=0).astype(BF16)
        vcat = jnp.concatenate([cv_ref[j], z_ref[rows, O_V:O_V + KV_W], pad], axis=0).astype(BF16)
        row_sets.append(rows)
        running.append(_attention(q_ref[rows, :], kcat, vcat, sink, 0))
    results = [None] * SAMPLE_GROUP
    while any(r is None for r in results):
        for j, steps in enumerate(running):
            if results[j] is None:
                try:
                    next(steps)
                except StopIteration as done:
                    results[j] = done.value
    for rows, o_b in zip(row_sets, results):
        mixed_ref[rows, W_A:W_A + W_B] = o_b * jax.nn.silu(z_ref[rows, O_GB:O_GB + W_B])


def _mix_sample(z2, layer, sinks, rope_tab, ck, cv, ws, bias_t, a_rows, h0r, h0i,
                bcat, ccat, d, wglu, bglu, n_batch, n_tok):
    m = n_batch * n_tok
    const = lambda *shape: pl.BlockSpec(shape, lambda b: (0,) * len(shape))
    per_layer = lambda *shape: _layer_spec(shape, layer)
    return pl.pallas_call(
        functools.partial(_mix_sample_kernel, layer=layer, n_batch=n_batch, n_tok=n_tok),
        grid=(n_batch // SAMPLE_GROUP,),
        in_specs=[
            pl.BlockSpec(memory_space=pltpu.SMEM),
            const(m, D_IN),
            const(3, m, LANES),
            pl.BlockSpec((None, SAMPLE_GROUP, WINDOW, KV_W), lambda b: (layer, b, 0, 0)),
            pl.BlockSpec((None, SAMPLE_GROUP, WINDOW, KV_W), lambda b: (layer, b, 0, 0)),
            per_layer(H_A, CHUNK, CHUNK),
            per_layer(m, H_A),
            per_layer(2, N_STATE),
            per_layer(n_batch, N_STATE),
            per_layer(n_batch, N_STATE),
            per_layer(W_C, 2 * SLAB_STATES),
            per_layer(W_C, 2 * SLAB_STATES),
            per_layer(1, W_C),
            per_layer(W_C, W_C),
            per_layer(1, W_C),
        ],
        out_specs=[
            const(m, D_MODEL),
            const(m, KV_W),
            const(n_batch, N_STATE),
            const(n_batch, N_STATE),
        ],
        out_shape=[
            jax.ShapeDtypeStruct((m, D_MODEL), F32),
            jax.ShapeDtypeStruct((m, KV_W), F32),
            jax.ShapeDtypeStruct((n_batch, N_STATE), F32),
            jax.ShapeDtypeStruct((n_batch, N_STATE), F32),
        ],
        scratch_shapes=[
            pltpu.VMEM((m, W_B), F32),
            pltpu.VMEM((m, 2 * N_STATE), F32),
        ],
        compiler_params=pltpu.CompilerParams(
            dimension_semantics=("arbitrary",), vmem_limit_bytes=VMEM_LIMIT),
        name="mix_sample",
    )(sinks, z2, rope_tab, ck, cv, ws, bias_t, a_rows, h0r, h0i, bcat, ccat, d, wglu, bglu)


def _rope_table(pos):
    half = ROPE_DIM // 2
    inv = ROPE_THETA ** (-jnp.arange(half, dtype=F32) * 2.0 / ROPE_DIM)
    ang = pos[:, None] * inv[None, :]
    cos, sin = jnp.cos(ang), jnp.sin(ang)
    n = pos.shape[0]
    rest = HD_B - ROPE_DIM
    c = jnp.concatenate([cos, cos, jnp.ones((n, rest), F32)], axis=1)
    s_lo = jnp.concatenate([jnp.zeros((n, half), F32), sin, jnp.zeros((n, rest), F32)], axis=1)
    s_hi = jnp.concatenate([-sin, jnp.zeros((n, half + rest), F32)], axis=1)
    tab = jnp.stack([c, s_lo, s_hi])
    return jnp.concatenate([tab] * (LANES // HD_B), axis=2)


def kernel(x_prompt, x_sample, cache_swa_k, cache_swa_v, state_ssm_re, state_ssm_im, norm_g,
           final_norm_g, w_in, w_out, chunk_w_s, chunk_b_s, attn_sinks, ssm_a_re, ssm_a_im,
           ssm_log_dt, ssm_b_re, ssm_b_im, ssm_c_re, ssm_c_im, ssm_d, glu_w, glu_b):
    bsz, seq, _ = x_prompt.shape
    n_batch, n_tok, _ = x_sample.shape
    m_s = n_batch * n_tok

    w_in_b, w_out_b = _prep_first_weights(w_in, w_out)
    wglu_b = glu_w.astype(BF16)

    ab_re, ab_im, pw_re, pw_im, bcat, ccat = _s5_prep(
        ssm_a_re, ssm_a_im, ssm_log_dt, ssm_b_re, ssm_b_im, ssm_c_re, ssm_c_im)
    a_rows = jnp.concatenate([ab_re.reshape(DEPTH, 1, N_STATE),
                              ab_im.reshape(DEPTH, 1, N_STATE)], axis=1)
    pw_re = jnp.transpose(pw_re, (0, 2, 1, 3)).reshape(DEPTH, SUB_T, N_STATE)
    pw_im = jnp.transpose(pw_im, (0, 2, 1, 3)).reshape(DEPTH, SUB_T, N_STATE)

    rope_p = _rope_table(jnp.arange(seq, dtype=F32))
    rope_s = jnp.tile(_rope_table(jnp.arange(n_tok, dtype=F32) + PAST_LEN), (1, n_batch, 1))
    bias_p = jnp.transpose(chunk_b_s, (0, 2, 1))
    bias_s = jnp.tile(bias_p[:, :n_tok], (1, n_batch, 1))
    ck = cache_swa_k.reshape(DEPTH, n_batch, WINDOW, KV_W)
    cv = cache_swa_v.reshape(DEPTH, n_batch, WINDOW, KV_W)
    h0r = state_ssm_re.reshape(DEPTH, n_batch, N_STATE)
    h0i = state_ssm_im.reshape(DEPTH, n_batch, N_STATE)
    g_all = norm_g.reshape(DEPTH, 1, D_MODEL)
    d_all = ssm_d.reshape(DEPTH, 1, W_C)
    bglu_all = glu_b.reshape(DEPTH, 1, W_C)
    fg = final_norm_g.reshape(1, D_MODEL)

    xp = x_prompt.reshape(bsz * seq, D_MODEL)
    xs = x_sample.reshape(m_s, D_MODEL)
    outs = [[] for _ in range(9)]
    for l in range(DEPTH):
        more = l + 1 < DEPTH
        mixed_p, k_last, v_last, h_fin, zs, *w_in_next = _prompt_layer(
            xp, xs, seq, l, g_all, w_in_b, w_in if more else None, attn_sinks, rope_p, chunk_w_s,
            bias_p, a_rows, pw_re, pw_im, bcat, ccat, d_all, wglu_b, bglu_all)
        if more:
            xp, w_out_next = _out_proj(mixed_p, xp, w_out_b, fg, l, w_out)
        else:
            xp = _out_proj(mixed_p, xp, w_out_b, fg, l)

        mixed_s, k_new, h_re_s, h_im_s = _mix_sample(
            zs, l, attn_sinks, rope_s, ck, cv, chunk_w_s, bias_s, a_rows, h0r, h0i,
            bcat, ccat, d_all, wglu_b, bglu_all, n_batch, n_tok)
        xs = _out_proj(mixed_s, xs, w_out_b, fg, l)
        if more:
            w_in_b, w_out_b = w_in_next[0], w_out_next

        outs[0].append(k_last.reshape(bsz, WINDOW, KV_B, HD_B))
        outs[1].append(v_last.reshape(bsz, WINDOW, KV_B, HD_B))
        outs[2].append(k_new.reshape(n_batch, n_tok, KV_B, HD_B))
        outs[3].append(zs[:, O_V:O_V + KV_W].reshape(n_batch, n_tok, KV_B, HD_B))
        outs[4].append(h_fin[:, 0].reshape(bsz, G_C, P_C))
        outs[5].append(h_fin[:, 1].reshape(bsz, G_C, P_C))
        outs[6].append(h_re_s.reshape(n_batch, G_C, P_C))
        outs[7].append(h_im_s.reshape(n_batch, G_C, P_C))
        outs[8].append(zs[:, O_VA:O_VA + W_A].reshape(n_batch, n_tok, W_A))

    return (xp.reshape(bsz, seq, D_MODEL), xs.reshape(n_batch, n_tok, D_MODEL),
            *[jnp.stack(o) for o in outs])
```

```python
import functools

import jax
import jax.numpy as jnp
from jax import lax
from jax.experimental import pallas as pl
from jax.experimental.pallas import tpu as pltpu

F32 = jnp.float32
BF16 = jnp.bfloat16

D_MODEL = 2048
DEPTH = 4
PAST_LEN = 16384
CHUNK = 128
W_A = 512
H_A = 4
C_A = W_A // H_A
W_B = 1024
HD_B = 64
H_B = W_B // HD_B
KV_B = 4
REP_B = H_B // KV_B
KV_W = KV_B * HD_B
WINDOW = 128
ROPE_DIM = HD_B // 4
ROPE_THETA = 500000.0
W_C = 512
GC = 16
G_C = W_C // GC
P_C = 64
N_STATE = G_C * P_C
D_IN = 3 * W_A + 2 * W_B + 2 * KV_W + 2 * W_C
EPS = 1e-5
NEG = -1e30

O_UA, O_VA, O_GA = 0, W_A, 2 * W_A
O_Q = 3 * W_A
O_K = O_Q + W_B
O_V = O_K + KV_W
O_GB = O_V + KV_W
O_UC = O_GB + W_B
O_GC = O_UC + W_C

LANES = 128
SUBLANES = 8
N_TILE = N_STATE // LANES
N_SLAB = W_C // LANES
SLAB_GROUPS = G_C // N_SLAB
SLAB_STATES = SLAB_GROUPS * P_C
SLAB_TILES = SLAB_STATES // LANES
VMEM_LIMIT = 60 * 1024 * 1024

OUT_ROWS = 512
PROJ_COLS = 512
FUSE_ROWS = 256
HEAD_WAVES = 2
SAMPLE_GROUP = 4
VPU_RUN_NEXT = True
W_PREP_ROWS = 256
W_PREP_COLS = 512
SCAN_VREGS = 4
SUB_T = CHUNK // SUBLANES

_NT = (((1,), (1,)), ((), ()))


def _rms(x, g):
    return x * lax.rsqrt(jnp.mean(x * x, axis=-1, keepdims=True) + EPS) * g


def _layer_spec(shape, layer, **kw):
    zeros = (0,) * len(shape)
    return pl.BlockSpec((None,) + tuple(shape), lambda *_: (layer,) + zeros, **kw)


def _out_proj_kernel(m_ref, x_ref, w_ref, fg_ref, *rest, final, prep_next):
    if prep_next:
        *wn_refs, y_ref, wnb_ref = rest
        for j, wn_ref in enumerate(wn_refs):
            wnb_ref[j * HD_B:(j + 1) * HD_B, :] = wn_ref[...].astype(BF16)
    else:
        y_ref, = rest
    y = x_ref[...] + jnp.dot(m_ref[...].astype(BF16), w_ref[...], preferred_element_type=F32)
    if final:
        y = _rms(y, fg_ref[...])
    y_ref[...] = y


def _w_out_source_block(i):
    first, last = W_A // HD_B, (W_A + W_B) // HD_B
    head = i - first
    src = first + (head % KV_B) * REP_B + head // KV_B
    return jnp.where((i >= first) & (i < last), src, i)


def _out_proj(mixed, x2d, w_b, fg, layer, w_out_f32=None):
    m = x2d.shape[0]
    tm = min(OUT_ROWS, m)
    final = layer == DEPTH - 1
    prep_next = w_out_f32 is not None
    in_specs = [
        pl.BlockSpec((tm, D_MODEL), lambda i: (i, 0)),
        pl.BlockSpec((tm, D_MODEL), lambda i: (i, 0)),
        pl.BlockSpec((D_MODEL, D_MODEL), lambda i: (0, 0), pipeline_mode=pl.Buffered(1)),
        pl.BlockSpec((1, D_MODEL), lambda i: (0, 0)),
    ]
    out_specs = [pl.BlockSpec((tm, D_MODEL), lambda i: (i, 0))]
    out_shape = [jax.ShapeDtypeStruct((m, D_MODEL), F32)]
    args = [mixed, x2d, w_b, fg]
    if prep_next:
        per_step = (D_MODEL // HD_B) // (m // tm)
        assert per_step * (m // tm) * HD_B == D_MODEL, (m, tm)
        for j in range(per_step):
            in_specs.append(pl.BlockSpec(
                (None, HD_B, D_MODEL),
                lambda i, j=j: (layer + 1, _w_out_source_block(i * per_step + j), 0)))
            args.append(w_out_f32)
        out_specs.append(pl.BlockSpec((per_step * HD_B, D_MODEL), lambda i: (i, 0)))
        out_shape.append(jax.ShapeDtypeStruct((D_MODEL, D_MODEL), BF16))
    out = pl.pallas_call(
        functools.partial(_out_proj_kernel, final=final, prep_next=prep_next),
        grid=(m // tm,),
        in_specs=in_specs,
        out_specs=out_specs,
        out_shape=out_shape,
        compiler_params=pltpu.CompilerParams(
            dimension_semantics=("arbitrary",), vmem_limit_bytes=VMEM_LIMIT),
        name="out_proj_final" if final else "out_proj",
    )(*args)
    return out if prep_next else out[0]


def _rep_major_source(new_head):
    r, g = divmod(new_head, KV_B)
    return g * REP_B + r


def _w_in_prep_kernel(w_ref, o_ref):
    heads_per_tile = LANES // HD_B
    lane_lo = lax.broadcasted_iota(jnp.int32, (w_ref.shape[0], LANES), 1) < HD_B

    def half(sec, head, want_hi):
        t = w_ref[:, sec + (head // heads_per_tile) * LANES:sec + (head // heads_per_tile + 1) * LANES]
        return t if (head % heads_per_tile == 1) == want_hi else pltpu.roll(t, HD_B, 1)

    for c0, c1, permute in ((0, O_Q, False), (O_Q, O_K, True), (O_K, O_GB, False),
                            (O_GB, O_UC, True), (O_UC, D_IN, False)):
        if not permute:
            o_ref[:, c0:c1] = w_ref[:, c0:c1].astype(BF16)
            continue
        for j in range(W_B // LANES):
            lo = half(c0, _rep_major_source(heads_per_tile * j), False)
            hi = half(c0, _rep_major_source(heads_per_tile * j + 1), True)
            o_ref[:, c0 + j * LANES:c0 + (j + 1) * LANES] = jnp.where(lane_lo, lo, hi).astype(BF16)


def _w_out_prep_kernel(w_ref, o_ref):
    o_ref[0:W_A, :] = w_ref[0:W_A, :].astype(BF16)
    for new_head in range(H_B):
        src = W_A + _rep_major_source(new_head) * HD_B
        dst = W_A + new_head * HD_B
        o_ref[dst:dst + HD_B, :] = w_ref[src:src + HD_B, :].astype(BF16)
    o_ref[W_A + W_B:, :] = w_ref[W_A + W_B:, :].astype(BF16)


def _prep_first_weights(w_in, w_out):
    w_in_b = pl.pallas_call(
        _w_in_prep_kernel,
        grid=(D_MODEL // W_PREP_ROWS,),
        in_specs=[pl.BlockSpec((None, W_PREP_ROWS, D_IN), lambda i: (0, i, 0))],
        out_specs=pl.BlockSpec((W_PREP_ROWS, D_IN), lambda i: (i, 0)),
        out_shape=jax.ShapeDtypeStruct((D_MODEL, D_IN), BF16),
        compiler_params=pltpu.CompilerParams(
            dimension_semantics=("arbitrary",), vmem_limit_bytes=VMEM_LIMIT),
        name="w_in_prep",
    )(w_in)
    w_out_b = pl.pallas_call(
        _w_out_prep_kernel,
        grid=(D_MODEL // W_PREP_COLS,),
        in_specs=[pl.BlockSpec((None, D_MODEL, W_PREP_COLS), lambda i: (0, 0, i))],
        out_specs=pl.BlockSpec((D_MODEL, W_PREP_COLS), lambda i: (0, i)),
        out_shape=jax.ShapeDtypeStruct((D_MODEL, D_MODEL), BF16),
        compiler_params=pltpu.CompilerParams(
            dimension_semantics=("arbitrary",), vmem_limit_bytes=VMEM_LIMIT),
        name="w_out_prep",
    )(w_out)
    return w_in_b, w_out_b


def _store_block_diag(out_ref, lane0, x3):
    per_tile = LANES // P_C
    lane_slot = lax.broadcasted_iota(jnp.int32, (GC, LANES), 1) // P_C
    for g in range(G_C):
        xg = x3[g]
        pair = jnp.concatenate([xg] * per_tile, axis=1)
        tile = jnp.where(lane_slot == g % per_tile, pair, 0.0)
        l0 = lane0 + ((g % SLAB_GROUPS) // per_tile) * LANES
        out_ref[g * GC:(g + 1) * GC, l0:l0 + LANES] = tile.astype(BF16)


def _s5_prep_kernel(are_ref, aim_ref, ldt_ref, btr_ref, bti_ref, cre_ref, cim_ref,
                    abr_ref, abi_ref, pwr_ref, pwi_ref, bcat_ref, ccat_ref):
    a_re, a_im = are_ref[...], aim_ref[...]
    dt = jnp.exp(ldt_ref[...])
    mag = jnp.exp(a_re * dt)
    ab_re = mag * jnp.cos(a_im * dt)
    ab_im = mag * jnp.sin(a_im * dt)
    nr, ni = ab_re - 1.0, ab_im
    den = a_re * a_re + a_im * a_im
    f_re = (nr * a_re + ni * a_im) / den
    f_im = (ni * a_re - nr * a_im) / den
    br, bi = btr_ref[...], bti_ref[...]
    bcat_ref[...] = jnp.zeros_like(bcat_ref)
    ccat_ref[...] = jnp.zeros_like(ccat_ref)
    _store_block_diag(bcat_ref, 0, f_re * br - f_im * bi)
    _store_block_diag(bcat_ref, SLAB_STATES, f_re * bi + f_im * br)
    _store_block_diag(ccat_ref, 0, cre_ref[...])
    _store_block_diag(ccat_ref, SLAB_STATES, -cim_ref[...])
    abr_ref[...] = ab_re
    abi_ref[...] = ab_im
    p_re, p_im = ab_re, ab_im
    for j in range(SUB_T):
        pwr_ref[:, j:j + 1, :] = p_re
        pwi_ref[:, j:j + 1, :] = p_im
        p_re, p_im = p_re * ab_re - p_im * ab_im, p_re * ab_im + p_im * ab_re


def _s5_prep(a_re, a_im, log_dt, b_re, b_im, c_re, c_im):
    a4 = lambda a: a.reshape(DEPTH, G_C, 1, P_C)
    bt = lambda b: jnp.transpose(b, (0, 1, 3, 2))
    vec = pl.BlockSpec((None, G_C, 1, P_C), lambda l: (l, 0, 0, 0))
    dts = pl.BlockSpec((None, G_C, 1, 1), lambda l: (l, 0, 0, 0))
    mat = pl.BlockSpec((None, G_C, GC, P_C), lambda l: (l, 0, 0, 0))
    pws = pl.BlockSpec((None, G_C, SUB_T, P_C), lambda l: (l, 0, 0, 0))
    dense = pl.BlockSpec((None, W_C, 2 * SLAB_STATES), lambda l: (l, 0, 0))
    vshape = jax.ShapeDtypeStruct((DEPTH, G_C, 1, P_C), F32)
    pshape = jax.ShapeDtypeStruct((DEPTH, G_C, SUB_T, P_C), F32)
    dshape = jax.ShapeDtypeStruct((DEPTH, W_C, 2 * SLAB_STATES), BF16)
    return pl.pallas_call(
        _s5_prep_kernel,
        grid=(DEPTH,),
        in_specs=[vec, vec, dts, mat, mat, mat, mat],
        out_specs=[vec, vec, pws, pws, dense, dense],
        out_shape=[vshape, vshape, pshape, pshape, dshape, dshape],
        name="s5_prep",
    )(a4(a_re), a4(a_im), log_dt.reshape(DEPTH, G_C, 1, 1), bt(b_re), bt(b_im), c_re, c_im)


def _rope(x, tab_ref):
    c, s_lo, s_hi = tab_ref[0], tab_ref[1], tab_ref[2]
    half = ROPE_DIM // 2
    tiles = []
    for j in range(x.shape[1] // LANES):
        t = x[:, j * LANES:(j + 1) * LANES]
        tiles.append(t * c + pltpu.roll(t, half, 1) * s_lo + pltpu.roll(t, LANES - half, 1) * s_hi)
    return jnp.concatenate(tiles, axis=1) if len(tiles) > 1 else tiles[0]


def _chunk_mlp(mix_w, bias_ref, z_ref):
    outs = []
    for h in range(H_A):
        v = z_ref[:, O_VA + h * C_A:O_VA + (h + 1) * C_A].astype(BF16)
        zz = jnp.dot(mix_w(h), v, preferred_element_type=F32) + bias_ref[:, h:h + 1]
        u = z_ref[:, O_UA + h * C_A:O_UA + (h + 1) * C_A]
        g = z_ref[:, O_GA + h * C_A:O_GA + (h + 1) * C_A]
        outs.append(u * zz * jax.nn.silu(g))
    return jnp.concatenate(outs, axis=1)


def _drain(steps):
    try:
        while True:
            next(steps)
    except StopIteration as done:
        return done.value


def _attention(q, kcat, vcat, sink, prev_off):
    t_len = q.shape[0]
    lane_grp = lax.broadcasted_iota(jnp.int32, (t_len, KV_W), 1) // HD_B
    pieces = []
    for r in range(REP_B):
        chunk = q[:, r * KV_W:(r + 1) * KV_W]
        for g in range(KV_B):
            pieces.append(jnp.where(lane_grp == g, chunk, 0.0))
    row = lax.broadcasted_iota(jnp.int32, (t_len, 2 * WINDOW), 0)
    col = lax.broadcasted_iota(jnp.int32, (t_len, 2 * WINDOW), 1)
    visible = jnp.where(col < WINDOW, col - row - prev_off, row - col + WINDOW + 1) > 0
    heads = [(r, g) for r in range(REP_B) for g in range(KV_B)]
    wave = len(heads) // HEAD_WAVES
    o_waves, scales = [], []
    for h0 in range(0, len(heads), wave):
        qbd = jnp.concatenate(pieces[h0:h0 + wave], axis=0).astype(BF16)
        s_w = lax.dot_general(qbd, kcat, _NT, preferred_element_type=F32)
        masked = lambda j, s_w=s_w: jnp.where(visible, s_w[j * t_len:(j + 1) * t_len], NEG)
        yield VPU_RUN_NEXT
        tops = [jnp.maximum(jnp.max(masked(j), axis=-1, keepdims=True), sink(g, r))
                for j, (r, g) in enumerate(heads[h0:h0 + wave])]
        probs = []
        for j, (r, g) in enumerate(heads[h0:h0 + wave]):
            p = jnp.exp(masked(j) - tops[j])
            den = jnp.sum(p, axis=-1, keepdims=True) + jnp.exp(sink(g, r) - tops[j])
            probs.append(p.astype(BF16))
            scales.append(1.0 / den)
        yield
        o_waves.append(jnp.dot(jnp.concatenate(probs, axis=0), vcat,
                               preferred_element_type=F32))
        yield
    outs = []
    for r in range(REP_B):
        acc = None
        for g in range(KV_B):
            hh = r * KV_B + g
            j = hh % wave
            o = o_waves[hh // wave][j * t_len:(j + 1) * t_len] * scales[hh]
            acc = o if acc is None else jnp.where(lane_grp == g, o, acc)
        outs.append(acc)
    return jnp.concatenate(outs, axis=1)


def _lane_tile(n):
    return slice(n * LANES, (n + 1) * LANES)


def _seq_major_perm(n_seq, n_step, transpose):
    m = n_seq * n_step
    row = lax.broadcasted_iota(jnp.int32, (m, m), 0)
    col = lax.broadcasted_iota(jnp.int32, (m, m), 1)
    if transpose:
        hit = row == (col % n_seq) * n_step + col // n_seq
    else:
        hit = col == (row % n_seq) * n_step + row // n_seq
    return jnp.where(hit, 1.0, 0.0).astype(BF16)


def _h_lanes(n):
    base = (n // SLAB_TILES) * 2 * SLAB_STATES + (n % SLAB_TILES) * LANES
    return slice(base, base + LANES), slice(base + SLAB_STATES, base + SLAB_STATES + LANES)


def _s5_drive(h_ref, z_ref, bcat_ref, n_seq, n_step):
    u = z_ref[:, O_UC:O_UC + W_C].astype(BF16)
    u = jnp.dot(_seq_major_perm(n_seq, n_step, False), u, preferred_element_type=F32).astype(BF16)
    for s in range(N_SLAB):
        h_ref[:, s * 2 * SLAB_STATES:(s + 1) * 2 * SLAB_STATES] = jnp.dot(
            u[:, _lane_tile(s)], bcat_ref[_lane_tile(s), :], preferred_element_type=F32)


def _s5_scan(h_ref, a_ref, n_seq, n_step, init):
    group = max(1, SCAN_VREGS * SUBLANES // n_seq)
    finals = [None] * N_TILE
    for n0 in range(0, N_TILE, group):
        tiles = range(n0, n0 + group)
        state = {n: init(n) for n in tiles}
        for j in range(n_step):
            rows = slice(j * n_seq, (j + 1) * n_seq)
            for n in tiles:
                re_l, im_l = _h_lanes(n)
                ar, ai = a_ref[0:1, _lane_tile(n)], a_ref[1:2, _lane_tile(n)]
                hr, hi = state[n]
                hr, hi = (ar * hr - ai * hi + h_ref[rows, re_l],
                          ar * hi + ai * hr + h_ref[rows, im_l])
                h_ref[rows, re_l] = hr
                h_ref[rows, im_l] = hi
                state[n] = (hr, hi)
        for n in tiles:
            finals[n] = state[n]
    return finals


def _s5_readout(h_ref, z_ref, ccat_ref, d_ref, wglu_ref, bglu_ref, n_seq, n_step):
    y = jnp.concatenate([
        lax.dot_general(h_ref[:, s * 2 * SLAB_STATES:(s + 1) * 2 * SLAB_STATES].astype(BF16),
                        ccat_ref[_lane_tile(s), :], _NT, preferred_element_type=F32)
        for s in range(N_SLAB)], axis=1)
    y_hi = y.astype(BF16)
    y_lo = (y - y_hi.astype(F32)).astype(BF16)
    back = _seq_major_perm(n_seq, n_step, True)
    y = (jnp.dot(back, y_hi, preferred_element_type=F32)
         + jnp.dot(back, y_lo, preferred_element_type=F32))
    y = jax.nn.gelu(y + d_ref[...] * z_ref[:, O_UC:O_UC + W_C])
    gate = jnp.dot(y.astype(BF16), wglu_ref[...], preferred_element_type=F32) + bglu_ref[...]
    y = y * jax.nn.sigmoid(gate)
    return y * jax.nn.silu(z_ref[:, O_GC:O_GC + W_C])


def _mix_block_steps(i, sinks_ref, z_ref, rope_ref, ws_ref, bias_ref, a_ref, pwr_ref, pwi_ref,
                     bcat_ref, ccat_ref, d_ref, wglu_ref, bglu_ref,
                     mixed_ref, klast_ref, vlast_ref, hfin_ref,
                     prev_kv, h_ref, hin_ref, carry_ref, layer):
    r_i = lax.broadcasted_iota(jnp.int32, (CHUNK, CHUNK), 0)
    c_i = lax.broadcasted_iota(jnp.int32, (CHUNK, CHUNK), 1)
    causal_w = lambda h: jnp.where(c_i <= r_i, ws_ref[h], 0.0).astype(BF16)
    mixed_ref[:, 0:W_A] = _chunk_mlp(causal_w, bias_ref, z_ref).astype(BF16)
    yield VPU_RUN_NEXT

    q = _rope(z_ref[:, O_Q:O_Q + W_B], rope_ref) * (HD_B ** -0.5)
    k = _rope(z_ref[:, O_K:O_K + KV_W], rope_ref)
    v = z_ref[:, O_V:O_V + KV_W]
    klast_ref[...] = k
    vlast_ref[...] = v
    kb, vb = k.astype(BF16), v.astype(BF16)
    kcat = jnp.concatenate([prev_kv[0], kb], axis=0)
    vcat = jnp.concatenate([prev_kv[1], vb], axis=0)
    prev_off = jnp.where(i > 0, 0, 2 * WINDOW)
    sink = lambda g, r: sinks_ref[layer, g * REP_B + r]
    yield
    o_b = yield from _attention(q, kcat, vcat, sink, prev_off)
    mixed_ref[:, W_A:W_A + W_B] = (o_b * jax.nn.silu(z_ref[:, O_GB:O_GB + W_B])).astype(BF16)
    yield

    _s5_drive(h_ref, z_ref, bcat_ref, SUBLANES, SUB_T)
    yield VPU_RUN_NEXT
    zero = jnp.zeros((SUBLANES, LANES), F32)
    finals = _s5_scan(h_ref, a_ref, SUBLANES, SUB_T, lambda n: (zero, zero))
    yield
    for n in range(N_TILE):
        st_l = _lane_tile(n)
        re_l, im_l = _h_lanes(n)
        end_r, end_i = finals[n]
        a_t_r, a_t_i = pwr_ref[SUB_T - 1:SUB_T, st_l], pwi_ref[SUB_T - 1:SUB_T, st_l]
        cr, ci = carry_ref[0:1, st_l], carry_ref[1:2, st_l]
        for r in range(SUBLANES):
            hin_ref[r:r + 1, re_l] = cr
            hin_ref[r:r + 1, im_l] = ci
            cr, ci = (end_r[r:r + 1] + a_t_r * cr - a_t_i * ci,
                      end_i[r:r + 1] + a_t_r * ci + a_t_i * cr)
        carry_ref[0:1, st_l] = cr
        carry_ref[1:2, st_l] = ci
        in_r, in_i = hin_ref[:, re_l], hin_ref[:, im_l]
        for j in range(SUB_T):
            rows = slice(j * SUBLANES, (j + 1) * SUBLANES)
            p_r, p_i = pwr_ref[j:j + 1, st_l], pwi_ref[j:j + 1, st_l]
            h_ref[rows, re_l] = h_ref[rows, re_l] + (p_r * in_r - p_i * in_i)
            h_ref[rows, im_l] = h_ref[rows, im_l] + (p_r * in_i + p_i * in_r)
        if n == N_TILE // 2 - 1:
            yield VPU_RUN_NEXT
    yield
    hfin_ref[...] = carry_ref[...]
    mixed_ref[:, W_A + W_B:] = _s5_readout(
        h_ref, z_ref, ccat_ref, d_ref, wglu_ref, bglu_ref, SUBLANES, SUB_T).astype(BF16)
    return kb, vb


def _prompt_layer_kernel(*refs, layer, halves_per_seq, n_half, prep_next):
    refs = list(refs)
    (sinks_ref, x_ref, xs_ref, g_ref, w_ref, rope_ref, ws_ref, bias_ref, a_ref, pwr_ref, pwi_ref,
     bcat_ref, ccat_ref, d_ref, wglu_ref, bglu_ref) = refs[:16]
    del refs[:16]
    wn_ref = refs.pop(0) if prep_next else None
    mixed_ref, klast_ref, vlast_ref, hfin_ref, zs_ref = refs[:5]
    del refs[:5]
    wnb_ref = refs.pop(0) if prep_next else None
    z_a, z_b, kprev_ref, vprev_ref, h_ref, hin_ref, carry_ref, zs_sem = refs
    s = pl.program_id(0)
    half = jnp.maximum(s - 1, 0) % halves_per_seq

    @pl.when(s == 0)
    def _():
        z_b[...] = jnp.zeros_like(z_b)

    @pl.when(half == 0)
    def _():
        kprev_ref[...] = jnp.zeros_like(kprev_ref)
        vprev_ref[...] = jnp.zeros_like(vprev_ref)
        carry_ref[...] = jnp.zeros_like(carry_ref)

    def step(z_new, z_cur):
        if prep_next:
            _w_in_prep_kernel(wn_ref, wnb_ref)
        x = jnp.where(s == n_half, xs_ref[...], x_ref[...])
        xn = _rms(x, g_ref[...]).astype(BF16)
        starts = list(range(0, D_IN, PROJ_COLS))

        def project():
            if starts:
                n0 = starts.pop(0)
                z_new[:, n0:n0 + PROJ_COLS] = jnp.dot(
                    xn, w_ref[:, n0:n0 + PROJ_COLS], preferred_element_type=F32)

        prev_kv = (kprev_ref[...], vprev_ref[...])
        for sb in range(FUSE_ROWS // CHUNK):
            rows = pl.ds(sb * CHUNK, CHUNK)
            steps = _mix_block_steps(
                half * (FUSE_ROWS // CHUNK) + sb, sinks_ref, z_cur.at[rows],
                rope_ref.at[:, rows], ws_ref, bias_ref, a_ref, pwr_ref, pwi_ref, bcat_ref,
                ccat_ref, d_ref, wglu_ref, bglu_ref, mixed_ref.at[rows], klast_ref,
                vlast_ref, hfin_ref, prev_kv, h_ref, hin_ref, carry_ref, layer)
            while True:
                try:
                    vpu_run_next = next(steps)
                except StopIteration as done:
                    prev_kv = done.value
                    break
                if vpu_run_next:
                    project()
        kprev_ref[...], vprev_ref[...] = prev_kv
        while starts:
            project()

    pl.when(s % 2 == 0)(lambda: step(z_a, z_b))
    pl.when(s % 2 == 1)(lambda: step(z_b, z_a))

    @pl.when(s == n_half)
    def _():
        copy = pltpu.make_async_copy((z_a, z_b)[n_half % 2], zs_ref, zs_sem)
        copy.start()
        copy.wait()


def _prompt_layer(x2d, xs2d, seq, layer, g_all, w_b, w_in_f32, sinks, rope_tab, ws, bias_t, a_rows,
                  pw_re, pw_im, bcat, ccat, d, wglu, bglu):
    m = x2d.shape[0]
    assert xs2d.shape == (FUSE_ROWS, D_MODEL), xs2d.shape
    bsz = m // seq
    n_half = m // FUSE_ROWS
    halves_per_seq = seq // FUSE_ROWS
    prep_next = w_in_f32 is not None
    slab = D_MODEL // n_half
    assert slab * n_half == D_MODEL and slab % (2 * SUBLANES) == 0, (n_half, slab)
    per_layer = lambda *shape, **kw: _layer_spec(shape, layer, **kw)
    once = dict(pipeline_mode=pl.Buffered(1))
    done = lambda s: jnp.maximum(s - 1, 0)
    last = lambda s: jnp.minimum(s, n_half - 1)
    next_in = [pl.BlockSpec((None, slab, D_IN), lambda s: (layer + 1, last(s), 0))] * prep_next
    next_out = [pl.BlockSpec((slab, D_IN), lambda s: (last(s), 0))] * prep_next
    next_shape = [jax.ShapeDtypeStruct((D_MODEL, D_IN), BF16)] * prep_next
    return pl.pallas_call(
        functools.partial(_prompt_layer_kernel, layer=layer, halves_per_seq=halves_per_seq,
                          n_half=n_half, prep_next=prep_next),
        grid=(n_half + 1,),
        in_specs=[
            pl.BlockSpec(memory_space=pltpu.SMEM),
            pl.BlockSpec((FUSE_ROWS, D_MODEL), lambda s: (last(s), 0)),
            pl.BlockSpec((FUSE_ROWS, D_MODEL), lambda s: (0, 0), **once),
            per_layer(1, D_MODEL),
            pl.BlockSpec((D_MODEL, D_IN), lambda s: (0, 0), **once),
            pl.BlockSpec((3, FUSE_ROWS, LANES), lambda s: (0, done(s) % halves_per_seq, 0)),
            per_layer(H_A, CHUNK, CHUNK),
            per_layer(CHUNK, H_A),
            per_layer(2, N_STATE),
            per_layer(SUB_T, N_STATE),
            per_layer(SUB_T, N_STATE),
            per_layer(W_C, 2 * SLAB_STATES, **once),
            per_layer(W_C, 2 * SLAB_STATES, **once),
            per_layer(1, W_C),
            per_layer(W_C, W_C, **once),
            per_layer(1, W_C),
        ] + next_in,
        out_specs=[
            pl.BlockSpec((FUSE_ROWS, D_MODEL), lambda s: (done(s), 0)),
            pl.BlockSpec((None, WINDOW, KV_W), lambda s: (done(s) // halves_per_seq, 0, 0)),
            pl.BlockSpec((None, WINDOW, KV_W), lambda s: (done(s) // halves_per_seq, 0, 0)),
            pl.BlockSpec((None, 2, N_STATE), lambda s: (done(s) // halves_per_seq, 0, 0)),
            pl.BlockSpec(memory_space=pl.ANY),
        ] + next_out,
        out_shape=[
            jax.ShapeDtypeStruct((m, D_MODEL), BF16),
            jax.ShapeDtypeStruct((bsz, WINDOW, KV_W), F32),
            jax.ShapeDtypeStruct((bsz, WINDOW, KV_W), F32),
            jax.ShapeDtypeStruct((bsz, 2, N_STATE), F32),
            jax.ShapeDtypeStruct((FUSE_ROWS, D_IN), F32),
        ] + next_shape,
        scratch_shapes=[
            pltpu.VMEM((FUSE_ROWS, D_IN), F32),
            pltpu.VMEM((FUSE_ROWS, D_IN), F32),
            pltpu.VMEM((WINDOW, KV_W), BF16),
            pltpu.VMEM((WINDOW, KV_W), BF16),
            pltpu.VMEM((CHUNK, 2 * N_STATE), F32),
            pltpu.VMEM((SUBLANES, 2 * N_STATE), F32),
            pltpu.VMEM((2, N_STATE), F32),
            pltpu.SemaphoreType.DMA(()),
        ],
        compiler_params=pltpu.CompilerParams(
            dimension_semantics=("arbitrary",), vmem_limit_bytes=VMEM_LIMIT),
        name="prompt_layer",
    )(sinks, x2d, xs2d, g_all, w_b, rope_tab, ws, bias_t, a_rows, pw_re, pw_im, bcat, ccat, d,
      wglu, bglu, *([w_in_f32] * prep_next))


def _mix_sample_kernel(sinks_ref, z_ref, rope_ref, ck_ref, cv_ref, ws_ref, bias_ref, a_ref,
                       h0r_ref, h0i_ref, bcat_ref, ccat_ref, d_ref, wglu_ref, bglu_ref,
                       mixed_ref, knew_ref, hre_ref, him_ref,
                       q_ref, h_ref, *, layer, n_batch, n_tok):
    b = pl.program_id(0)
    m = n_batch * n_tok

    @pl.when(b == 0)
    def _():
        r_i = lax.broadcasted_iota(jnp.int32, (m, m), 0)
        c_i = lax.broadcasted_iota(jnp.int32, (m, m), 1)
        keep = (r_i // n_tok == c_i // n_tok) & (c_i <= r_i)
        spread = jnp.where(lax.broadcasted_iota(jnp.int32, (m, CHUNK), 0) % n_tok
                           == lax.broadcasted_iota(jnp.int32, (m, CHUNK), 1), 1.0, 0.0).astype(BF16)
        spread_t = jnp.where(lax.broadcasted_iota(jnp.int32, (CHUNK, m), 1) % n_tok
                             == lax.broadcasted_iota(jnp.int32, (CHUNK, m), 0), 1.0, 0.0).astype(BF16)

        def tiled_w(h):
            rows = jnp.dot(spread, ws_ref[h].astype(BF16), preferred_element_type=F32)
            full = jnp.dot(rows.astype(BF16), spread_t, preferred_element_type=F32)
            return jnp.where(keep, full, 0.0).astype(BF16)

        mixed_ref[:, 0:W_A] = _chunk_mlp(tiled_w, bias_ref, z_ref)
        q_ref[...] = _rope(z_ref[:, O_Q:O_Q + W_B], rope_ref) * (HD_B ** -0.5)
        knew_ref[...] = _rope(z_ref[:, O_K:O_K + KV_W], rope_ref)
        _s5_drive(h_ref, z_ref, bcat_ref, n_batch, n_tok)
        finals = _s5_scan(h_ref, a_ref, n_batch, n_tok,
                          lambda n: (h0r_ref[:, _lane_tile(n)], h0i_ref[:, _lane_tile(n)]))
        for n in range(N_TILE):
            hre_ref[:, _lane_tile(n)] = finals[n][0]
            him_ref[:, _lane_tile(n)] = finals[n][1]
        mixed_ref[:, W_A + W_B:] = _s5_readout(
            h_ref, z_ref, ccat_ref, d_ref, wglu_ref, bglu_ref, n_batch, n_tok)

    pad = jnp.zeros((WINDOW - n_tok, KV_W), F32)
    sink = lambda g, r: sinks_ref[layer, g * REP_B + r]
    row_sets, running = [], []
    for j in range(SAMPLE_GROUP):
        rows = pl.ds(pl.multiple_of((b * SAMPLE_GROUP + j) * n_tok, n_tok), n_tok)
        kcat = jnp.concatenate([ck_ref[j], knew_ref[rows, :], pad], axis=0).astype(BF16)
        vcat = jnp.concatenate([cv_ref[j], z_ref[rows, O_V:O_V + KV_W], pad], axis=0).astype(BF16)
        row_sets.append(rows)
        running.append(_attention(q_ref[rows, :], kcat, vcat, sink, 0))
    results = [None] * SAMPLE_GROUP
    while any(r is None for r in results):
        for j, steps in enumerate(running):
            if results[j] is None:
                try:
                    next(steps)
                except StopIteration as done:
                    results[j] = done.value
    for rows, o_b in zip(row_sets, results):
        mixed_ref[rows, W_A:W_A + W_B] = o_b * jax.nn.silu(z_ref[rows, O_GB:O_GB + W_B])


def _mix_sample(z2, layer, sinks, rope_tab, ck, cv, ws, bias_t, a_rows, h0r, h0i,
                bcat, ccat, d, wglu, bglu, n_batch, n_tok):
    m = n_batch * n_tok
    const = lambda *shape: pl.BlockSpec(shape, lambda b: (0,) * len(shape))
    per_layer = lambda *shape: _layer_spec(shape, layer)
    return pl.pallas_call(
        functools.partial(_mix_sample_kernel, layer=layer, n_batch=n_batch, n_tok=n_tok),
        grid=(n_batch // SAMPLE_GROUP,),
        in_specs=[
            pl.BlockSpec(memory_space=pltpu.SMEM),
            const(m, D_IN),
            const(3, m, LANES),
            pl.BlockSpec((None, SAMPLE_GROUP, WINDOW, KV_W), lambda b: (layer, b, 0, 0)),
            pl.BlockSpec((None, SAMPLE_GROUP, WINDOW, KV_W), lambda b: (layer, b, 0, 0)),
            per_layer(H_A, CHUNK, CHUNK),
            per_layer(m, H_A),
            per_layer(2, N_STATE),
            per_layer(n_batch, N_STATE),
            per_layer(n_batch, N_STATE),
            per_layer(W_C, 2 * SLAB_STATES),
            per_layer(W_C, 2 * SLAB_STATES),
            per_layer(1, W_C),
            per_layer(W_C, W_C),
            per_layer(1, W_C),
        ],
        out_specs=[
            const(m, D_MODEL),
            const(m, KV_W),
            const(n_batch, N_STATE),
            const(n_batch, N_STATE),
        ],
        out_shape=[
            jax.ShapeDtypeStruct((m, D_MODEL), F32),
            jax.ShapeDtypeStruct((m, KV_W), F32),
            jax.ShapeDtypeStruct((n_batch, N_STATE), F32),
            jax.ShapeDtypeStruct((n_batch, N_STATE), F32),
        ],
        scratch_shapes=[
            pltpu.VMEM((m, W_B), F32),
            pltpu.VMEM((m, 2 * N_STATE), F32),
        ],
        compiler_params=pltpu.CompilerParams(
            dimension_semantics=("arbitrary",), vmem_limit_bytes=VMEM_LIMIT),
        name="mix_sample",
    )(sinks, z2, rope_tab, ck, cv, ws, bias_t, a_rows, h0r, h0i, bcat, ccat, d, wglu, bglu)


def _rope_table(pos):
    half = ROPE_DIM // 2
    inv = ROPE_THETA ** (-jnp.arange(half, dtype=F32) * 2.0 / ROPE_DIM)
    ang = pos[:, None] * inv[None, :]
    cos, sin = jnp.cos(ang), jnp.sin(ang)
    n = pos.shape[0]
    rest = HD_B - ROPE_DIM
    c = jnp.concatenate([cos, cos, jnp.ones((n, rest), F32)], axis=1)
    s_lo = jnp.concatenate([jnp.zeros((n, half), F32), sin, jnp.zeros((n, rest), F32)], axis=1)
    s_hi = jnp.concatenate([-sin, jnp.zeros((n, half + rest), F32)], axis=1)
    tab = jnp.stack([c, s_lo, s_hi])
    return jnp.concatenate([tab] * (LANES // HD_B), axis=2)


def kernel(x_prompt, x_sample, cache_swa_k, cache_swa_v, state_ssm_re, state_ssm_im, norm_g,
           final_norm_g, w_in, w_out, chunk_w_s, chunk_b_s, attn_sinks, ssm_a_re, ssm_a_im,
           ssm_log_dt, ssm_b_re, ssm_b_im, ssm_c_re, ssm_c_im, ssm_d, glu_w, glu_b):
    bsz, seq, _ = x_prompt.shape
    n_batch, n_tok, _ = x_sample.shape
    m_s = n_batch * n_tok

    w_in_b, w_out_b = _prep_first_weights(w_in, w_out)
    wglu_b = glu_w.astype(BF16)

    ab_re, ab_im, pw_re, pw_im, bcat, ccat = _s5_prep(
        ssm_a_re, ssm_a_im, ssm_log_dt, ssm_b_re, ssm_b_im, ssm_c_re, ssm_c_im)
    a_rows = jnp.concatenate([ab_re.reshape(DEPTH, 1, N_STATE),
                              ab_im.reshape(DEPTH, 1, N_STATE)], axis=1)
    pw_re = jnp.transpose(pw_re, (0, 2, 1, 3)).reshape(DEPTH, SUB_T, N_STATE)
    pw_im = jnp.transpose(pw_im, (0, 2, 1, 3)).reshape(DEPTH, SUB_T, N_STATE)

    rope_p = _rope_table(jnp.arange(seq, dtype=F32))
    rope_s = jnp.tile(_rope_table(jnp.arange(n_tok, dtype=F32) + PAST_LEN), (1, n_batch, 1))
    bias_p = jnp.transpose(chunk_b_s, (0, 2, 1))
    bias_s = jnp.tile(bias_p[:, :n_tok], (1, n_batch, 1))
    ck = cache_swa_k.reshape(DEPTH, n_batch, WINDOW, KV_W)
    cv = cache_swa_v.reshape(DEPTH, n_batch, WINDOW, KV_W)
    h0r = state_ssm_re.reshape(DEPTH, n_batch, N_STATE)
    h0i = state_ssm_im.reshape(DEPTH, n_batch, N_STATE)
    g_all = norm_g.reshape(DEPTH, 1, D_MODEL)
    d_all = ssm_d.reshape(DEPTH, 1, W_C)
    bglu_all = glu_b.reshape(DEPTH, 1, W_C)
    fg = final_norm_g.reshape(1, D_MODEL)

    xp = x_prompt.reshape(bsz * seq, D_MODEL)
    xs = x_sample.reshape(m_s, D_MODEL)
    outs = [[] for _ in range(9)]
    for l in range(DEPTH):
        more = l + 1 < DEPTH
        mixed_p, k_last, v_last, h_fin, zs, *w_in_next = _prompt_layer(
            xp, xs, seq, l, g_all, w_in_b, w_in if more else None, attn_sinks, rope_p, chunk_w_s,
            bias_p, a_rows, pw_re, pw_im, bcat, ccat, d_all, wglu_b, bglu_all)
        if more:
            xp, w_out_next = _out_proj(mixed_p, xp, w_out_b, fg, l, w_out)
        else:
            xp = _out_proj(mixed_p, xp, w_out_b, fg, l)

        mixed_s, k_new, h_re_s, h_im_s = _mix_sample(
            zs, l, attn_sinks, rope_s, ck, cv, chunk_w_s, bias_s, a_rows, h0r, h0i,
            bcat, ccat, d_all, wglu_b, bglu_all, n_batch, n_tok)
        xs = _out_proj(mixed_s, xs, w_out_b, fg, l)
        if more:
            w_in_b, w_out_b = w_in_next[0], w_out_next

        outs[0].append(k_last.reshape(bsz, WINDOW, KV_B, HD_B))
        outs[1].append(v_last.reshape(bsz, WINDOW, KV_B, HD_B))
        outs[2].append(k_new.reshape(n_batch, n_tok, KV_B, HD_B))
        outs[3].append(zs[:, O_V:O_V + KV_W].reshape(n_batch, n_tok, KV_B, HD_B))
        outs[4].append(h_fin[:, 0].reshape(bsz, G_C, P_C))
        outs[5].append(h_fin[:, 1].reshape(bsz, G_C, P_C))
        outs[6].append(h_re_s.reshape(n_batch, G_C, P_C))
        outs[7].append(h_im_s.reshape(n_batch, G_C, P_C))
        outs[8].append(zs[:, O_VA:O_VA + W_A].reshape(n_batch, n_tok, W_A))

    return (xp.reshape(bsz, seq, D_MODEL), xs.reshape(n_batch, n_tok, D_MODEL),
            *[jnp.stack(o) for o in outs])
```
